```python
import math
import jax, jax.numpy as jnp
from jax import lax
import numpy as np

D_MODEL = 1024
BATCH = 8
SEQ = 2048
DEPTH = 1

CHUNK = 64
N_META = 16
Q_BLOCK = 128

D_MIX = D_MODEL
D_POOL = D_MIX // 2
POOL_WINDOWS = (2, 4, 8, 16)
N_POOL_GROUPS = len(POOL_WINDOWS)
POOL_GROUP = D_POOL // N_POOL_GROUPS

N_HEADS = 4
QK_NOPE = 128
QK_ROPE = 64
V_HEAD = 128
D_ATTN = N_HEADS * V_HEAD
Q_LORA = 256
KV_LORA = 128
ROPE_THETA = 10000.0
EPS = 1e-6

SPLIT_POINTS = (D_POOL, 2 * D_POOL, 2 * D_POOL + Q_LORA,
                2 * D_POOL + Q_LORA + KV_LORA,
                2 * D_POOL + Q_LORA + KV_LORA + QK_ROPE)
D_IN = 2 * D_POOL + Q_LORA + KV_LORA + QK_ROPE + D_ATTN

kernel_name = "hymba_pool_mla_hybrid"


def rmsnorm(x, g):
    xf = x.astype(jnp.float32)
    y = xf * lax.rsqrt(jnp.mean(xf * xf, axis=-1, keepdims=True) + EPS)
    return y.astype(x.dtype) * g


def rope_tables(length):
    half = QK_ROPE // 2
    inv_freq = 1.0 / (ROPE_THETA ** (jnp.arange(half, dtype=jnp.float32) / half))
    ang = jnp.arange(length, dtype=jnp.float32)[:, None] * inv_freq[None, :]
    return jnp.cos(ang), jnp.sin(ang)


def apply_rope(x, cos, sin):
    cos = cos.astype(x.dtype)
    sin = sin.astype(x.dtype)
    x1, x2 = jnp.split(x, 2, axis=-1)
    return jnp.concatenate([x1 * cos - x2 * sin, x1 * sin + x2 * cos], axis=-1)


def pool_mixer(u, pool_w, pool_scale):
    B, L, _ = u.shape
    uf = u.astype(jnp.float32)
    groups = jnp.split(uf, N_POOL_GROUPS, axis=-1)
    count_max = jnp.arange(1, L + 1, dtype=jnp.float32)[None, :, None]
    outs = []
    for g, w in zip(groups, POOL_WINDOWS):
        c = jnp.cumsum(g, axis=1)
        c_prev = jnp.pad(c, ((0, 0), (w, 0), (0, 0)))[:, :L]
        mean = (c - c_prev) / jnp.minimum(count_max, float(w))
        outs.append(mean - g)
    pooled = jnp.stack(outs, axis=2).astype(u.dtype)
    mixed = jnp.einsum("blgc,gcd->blgd", pooled, pool_w)
    return mixed.reshape(B, L, D_POOL) * pool_scale


def _attend(qn, qr, q_ids, kn, kr, vv, k_ids):
    scale = (QK_NOPE + QK_ROPE) ** -0.5
    s = (jnp.einsum("bqhd,bkhd->bhqk", qn, kn, preferred_element_type=jnp.float32)
         + jnp.einsum("bqhd,bkd->bhqk", qr, kr, preferred_element_type=jnp.float32)) * scale
    mask = k_ids[None, :] <= q_ids[:, None]
    s = jnp.where(mask[None, None], s, jnp.finfo(jnp.float32).min)
    p = jax.nn.softmax(s, axis=-1).astype(vv.dtype)
    return jnp.einsum("bhqk,bkhd->bqhd", p, vv)


def mla_attention(q_nope, q_rope, k_nope, k_rope, v, chunk_id):
    B, L = q_nope.shape[0], q_nope.shape[1]
    n_blk = (L - N_META) // Q_BLOCK

    def blockify(t):
        t = t[:, N_META:]
        t = t.reshape((B, n_blk, Q_BLOCK) + t.shape[2:])
        return jnp.moveaxis(t, 1, 0)

    ids_b = chunk_id[N_META:].reshape(n_blk, Q_BLOCK)
    out_real = lax.map(
        lambda a: _attend(a[0], a[1], a[2], k_nope, k_rope, v, chunk_id),
        (blockify(q_nope), blockify(q_rope), ids_b))
    out_real = jnp.moveaxis(out_real, 0, 1).reshape(B, L - N_META, N_HEADS, V_HEAD)
    m_ids = chunk_id[:N_META]
    out_meta = _attend(q_nope[:, :N_META], q_rope[:, :N_META], m_ids,
                       k_nope[:, :N_META], k_rope[:, :N_META], v[:, :N_META], m_ids)
    return jnp.concatenate([out_meta, out_real], axis=1)


def hybrid_layer(h, cos, sin, chunk_id, norm_g, w_in, q_norm_g, w_q_b,
                 kv_norm_g, w_kv_b, pool_w, pool_scale, w_out):
    B, L, _ = h.shape
    u = rmsnorm(h, norm_g) @ w_in
    pool_in, pool_gate, c_q, c_kv, k_r, attn_gate = jnp.split(u, SPLIT_POINTS, axis=-1)

    pool_out = jax.nn.silu(pool_gate) * pool_mixer(pool_in, pool_w, pool_scale)

    q = (rmsnorm(c_q, q_norm_g) @ w_q_b).reshape(B, L, N_HEADS, QK_NOPE + QK_ROPE)
    q_nope = q[..., :QK_NOPE]
    q_rope = apply_rope(q[..., QK_NOPE:], cos[:, None, :], sin[:, None, :])
    kv = (rmsnorm(c_kv, kv_norm_g) @ w_kv_b).reshape(B, L, N_HEADS, QK_NOPE + V_HEAD)
    k_nope = kv[..., :QK_NOPE]
    v = kv[..., QK_NOPE:]
    k_rope = apply_rope(k_r, cos, sin)
    attn = mla_attention(q_nope, q_rope, k_nope, k_rope, v, chunk_id).reshape(B, L, D_ATTN)
    attn_out = jax.nn.silu(attn_gate) * attn

    mix = jnp.concatenate([pool_out, attn_out], axis=-1) @ w_out
    return h + mix


def setup_inputs(seed: int = 0) -> dict:
    key = jax.random.key(seed)
    ks = jax.random.split(key, 16)
    f32 = jnp.float32

    def nrm(k, shape, scale):
        return jax.random.normal(k, shape, f32) * scale

    return {
        "x": nrm(ks[0], (BATCH, SEQ, D_MODEL), 1.0),
        "meta_tokens": nrm(ks[1], (N_META, D_MODEL), 1.0),
        "norm_g": 1.0 + nrm(ks[2], (DEPTH, D_MODEL), 0.02),
        "w_in": nrm(ks[3], (DEPTH, D_MODEL, D_IN), D_MODEL ** -0.5),
        "q_norm_g": 1.0 + nrm(ks[4], (DEPTH, Q_LORA), 0.02),
        "w_q_b": nrm(ks[5], (DEPTH, Q_LORA, N_HEADS * (QK_NOPE + QK_ROPE)), Q_LORA ** -0.5),
        "kv_norm_g": 1.0 + nrm(ks[6], (DEPTH, KV_LORA), 0.02),
        "w_kv_b": nrm(ks[7], (DEPTH, KV_LORA, N_HEADS * (QK_NOPE + V_HEAD)), KV_LORA ** -0.5),
        "pool_w": nrm(ks[8], (DEPTH, N_POOL_GROUPS, POOL_GROUP, POOL_GROUP), POOL_GROUP ** -0.5),
        "pool_scale": 1.0 + nrm(ks[9], (DEPTH, D_POOL), 0.02),
        "w_out": nrm(ks[10], (DEPTH, D_MIX, D_MODEL), D_MIX ** -0.5),
        "final_norm_g": 1.0 + nrm(ks[11], (D_MODEL,), 0.02),
    }


def reference(x, meta_tokens, norm_g, w_in, q_norm_g, w_q_b, kv_norm_g, w_kv_b,
              pool_w, pool_scale, w_out, final_norm_g):
    B, S, D = x.shape
    L = S + N_META
    meta = jnp.broadcast_to(meta_tokens.astype(x.dtype)[None], (B, N_META, D))
    h = jnp.concatenate([meta, x], axis=1)
    chunk_id = jnp.concatenate([jnp.zeros((N_META,), jnp.int32),
                                1 + jnp.arange(S, dtype=jnp.int32) // CHUNK])
    cos, sin = rope_tables(L)
    for i in range(DEPTH):
        h = hybrid_layer(h, cos, sin, chunk_id, norm_g[i], w_in[i], q_norm_g[i], w_q_b[i],
                         kv_norm_g[i], w_kv_b[i], pool_w[i], pool_scale[i], w_out[i])
    return rmsnorm(h, final_norm_g)[:, N_META:]
```

```python
import functools

import jax
import jax.numpy as jnp
from jax import lax
from jax.experimental import pallas as pl
from jax.experimental.pallas import tpu as pltpu

D_MODEL = 1024
N_META = 16
CHUNK = 64
D_POOL = 512
POOL_WINDOWS = (2, 4, 8, 16)
POOL_GROUP = 128
N_HEADS = 4
QK_NOPE = 128
QK_ROPE = 64
V_HEAD = 128
D_ATTN = N_HEADS * V_HEAD
Q_LORA = 256
KV_LORA = 128
ROPE_THETA = 10000.0
EPS = 1e-6
SM_SCALE = (QK_NOPE + QK_ROPE) ** -0.5

LANES = 128
QK_PAD = 2 * LANES
META_PAD = LANES
HALO = 16
ROW_TILE = 512
MASK_VALUE = -1e30
VMEM_LIMIT_BYTES = 56 * 1024 * 1024

F32 = jnp.float32
BF16 = jnp.bfloat16


def _dot(a, b):
    return jnp.dot(a, b, preferred_element_type=F32)


def _dot_nt(a, b):
    return lax.dot_general(a, b, (((1,), (1,)), ((), ())), preferred_element_type=F32)


def _rms(x, g):
    return x * lax.rsqrt(jnp.mean(x * x, axis=-1, keepdims=True) + EPS) * g


def _silu(x):
    return x * (1.0 / (1.0 + jnp.exp(-x)))


def _rope_block(x, cos_t, sin_t):
    lane = lax.broadcasted_iota(jnp.int32, x.shape, 1)
    first_half = (lane % QK_ROPE) < (QK_ROPE // 2)
    partner = jnp.where(first_half,
                        pltpu.roll(x, LANES - QK_ROPE // 2, axis=1),
                        pltpu.roll(x, QK_ROPE // 2, axis=1))
    return x * cos_t + partner * sin_t


def _latent_kv(xn, w_ckv_ref, kv_g_ref, w_kvb_ref, cos_t, sin_t):
    ckvr = _dot(xn, w_ckv_ref[...])
    ckv_n = _rms(ckvr[:, :KV_LORA], kv_g_ref[...]).astype(BF16)
    kv = _dot(ckv_n, w_kvb_ref[...])
    k_rope = _rope_block(ckvr[:, KV_LORA:], cos_t, sin_t)
    return kv[:, :N_HEADS * QK_NOPE], kv[:, N_HEADS * QK_NOPE:], k_rope


def _fused_kernel(x_ref, meta_ref, cos_ref, sin_ref, mcos_ref, msin_ref,
                  norm_g_ref, w_pi_ref, w_pg_ref, w_cq_ref, w_ckv_ref, w_ag_ref,
                  q_g_ref, w_qb_ref, kv_g_ref, w_kvb_ref, pool_w_ref, pool_scale_ref,
                  w_out_ref, final_g_ref,
                  out_ref,
                  k_scr, v_scr, kmeta_scr, vmeta_scr, pool_scr, halo_scr,
                  m_scr, l_scr, acc_scr):
    b = pl.program_id(0)
    i = pl.program_id(1)
    tm = ROW_TILE

    @pl.when((b == 0) & (i == 0))
    def _meta_prologue():
        xn = _rms(meta_ref[...], norm_g_ref[...]).astype(BF16)
        halo_scr[...] = _dot(xn, w_pi_ref[...])
        k_nope, v, k_rope = _latent_kv(xn, w_ckv_ref, kv_g_ref, w_kvb_ref,
                                       mcos_ref[...], msin_ref[...])
        kmeta_scr[...] = jnp.zeros_like(kmeta_scr)
        vmeta_scr[...] = jnp.zeros_like(vmeta_scr)
        for h in range(N_HEADS):
            kmeta_scr[h, 0:N_META, 0:LANES] = k_nope[:, h * QK_NOPE:(h + 1) * QK_NOPE].astype(BF16)
            kmeta_scr[h, 0:N_META, LANES:QK_PAD] = k_rope.astype(BF16)
            vmeta_scr[h, 0:N_META, :] = v[:, h * V_HEAD:(h + 1) * V_HEAD].astype(BF16)

    @pl.when(i == 0)
    def _seed_halo():
        pool_scr[0:HALO, :] = halo_scr[...]

    x = x_ref[...]
    xn = _rms(x, norm_g_ref[...]).astype(BF16)

    pool_scr[HALO:HALO + tm, :] = _dot(xn, w_pi_ref[...])
    mixed = []
    for g, w in enumerate(POOL_WINDOWS):
        cols = slice(g * POOL_GROUP, (g + 1) * POOL_GROUP)
        cur = pool_scr[HALO:HALO + tm, cols]
        win = cur
        for k in range(1, w):
            win = win + pool_scr[HALO - k:HALO - k + tm, cols]
        pooled = win * (1.0 / w) - cur
        mixed.append(_dot(pooled.astype(BF16), pool_w_ref[g]))
    pool_scr[0:HALO, :] = pool_scr[tm:tm + HALO, :]
    pool_mix = jnp.concatenate(mixed, axis=1) * pool_scale_ref[...]
    pool_out = (_silu(_dot(xn, w_pg_ref[...])) * pool_mix).astype(BF16)

    cos_t = cos_ref[...]
    sin_t = sin_ref[...]
    k_nope, v, k_rope = _latent_kv(xn, w_ckv_ref, kv_g_ref, w_kvb_ref, cos_t, sin_t)
    row0 = pl.multiple_of(i * tm, tm)
    k_rope_bf = k_rope.astype(BF16)
    for h in range(N_HEADS):
        k_scr[h, pl.ds(row0, tm), 0:LANES] = k_nope[:, h * QK_NOPE:(h + 1) * QK_NOPE].astype(BF16)
        k_scr[h, pl.ds(row0, tm), LANES:QK_PAD] = k_rope_bf
        v_scr[h, pl.ds(row0, tm), :] = v[:, h * V_HEAD:(h + 1) * V_HEAD].astype(BF16)

    cq_n = _rms(_dot(xn, w_cq_ref[...]), q_g_ref[...]).astype(BF16)
    q = _dot(cq_n, w_qb_ref[...]) * SM_SCALE
    q_rope = [_rope_block(q[:, N_HEADS * QK_NOPE + p * LANES:N_HEADS * QK_NOPE + (p + 1) * LANES], cos_t, sin_t)
              for p in range(N_HEADS // 2)]

    lane = lax.broadcasted_iota(jnp.int32, (tm, LANES), 1)
    meta_valid = lane < N_META
    row_chunk = lax.broadcasted_iota(jnp.int32, (tm, tm), 0) // CHUNK
    col_chunk = lax.broadcasted_iota(jnp.int32, (tm, tm), 1) // CHUNK
    diag_valid = col_chunk <= row_chunk

    attn = []
    for h in range(N_HEADS):
        own_half = (lane // QK_ROPE) == (h % 2)
        qh = jnp.concatenate(
            [q[:, h * QK_NOPE:(h + 1) * QK_NOPE], jnp.where(own_half, q_rope[h // 2], 0.0)],
            axis=1).astype(BF16)

        s = jnp.where(meta_valid, _dot_nt(qh, kmeta_scr[h]), MASK_VALUE)
        m0 = jnp.max(s, axis=1, keepdims=True)
        p = jnp.exp(s - m0)
        m_scr[...] = m0
        l_scr[...] = jnp.sum(p, axis=1, keepdims=True)
        acc_scr[...] = _dot(p.astype(BF16), vmeta_scr[h])

        def _online_update(s, rows, h=h):
            m_prev = m_scr[...]
            m_new = jnp.maximum(m_prev, jnp.max(s, axis=1, keepdims=True))
            alpha = jnp.exp(m_prev - m_new)
            p = jnp.exp(s - m_new)
            l_scr[...] = alpha * l_scr[...] + jnp.sum(p, axis=1, keepdims=True)
            acc_scr[...] = alpha * acc_scr[...] + _dot(p.astype(BF16), v_scr[h, rows, :])
            m_scr[...] = m_new

        def _full_tile(j, carry, h=h, qh=qh, update=_online_update):
            rows = pl.ds(pl.multiple_of(j * tm, tm), tm)
            update(_dot_nt(qh, k_scr[h, rows, :]), rows)
            return carry

        lax.fori_loop(0, i, _full_tile, 0)

        rows = pl.ds(row0, tm)
        _online_update(jnp.where(diag_valid, _dot_nt(qh, k_scr[h, rows, :]), MASK_VALUE), rows)
        attn.append(acc_scr[...] * (1.0 / l_scr[...]))

    attn_out = (_silu(_dot(xn, w_ag_ref[...])) * jnp.concatenate(attn, axis=1)).astype(BF16)

    mix = _dot(jnp.concatenate([pool_out, attn_out], axis=1), w_out_ref[...])
    out_ref[...] = _rms(x + mix, final_g_ref[...])


def _rope_tables(length):
    half = QK_ROPE // 2
    inv_freq = 1.0 / (ROPE_THETA ** (jnp.arange(half, dtype=F32) / half))
    ang = jnp.arange(length, dtype=F32)[:, None] * inv_freq[None, :]
    cos, sin = jnp.cos(ang), jnp.sin(ang)
    cos_t = jnp.tile(cos, (1, LANES // half))
    sin_t = jnp.tile(jnp.concatenate([-sin, sin], axis=1), (1, LANES // QK_ROPE))
    return cos_t, sin_t


def kernel(x, meta_tokens, norm_g, w_in, q_norm_g, w_q_b, kv_norm_g, w_kv_b, pool_w, pool_scale, w_out, final_norm_g):
    batch, seq, d_model = x.shape
    assert d_model == D_MODEL and seq % ROW_TILE == 0 and meta_tokens.shape == (N_META, D_MODEL)
    assert norm_g.shape[0] == 1, "single-layer block"
    tm = ROW_TILE

    w_in0 = w_in[0]
    o_cq = 2 * D_POOL
    o_ckv = o_cq + Q_LORA
    o_kr = o_ckv + KV_LORA
    o_ag = o_kr + QK_ROPE
    w_pi = w_in0[:, :D_POOL].astype(BF16)
    w_pg = w_in0[:, D_POOL:o_cq].astype(BF16)
    w_cq = w_in0[:, o_cq:o_ckv].astype(BF16)
    w_kr = w_in0[:, o_kr:o_ag]
    w_ckv = jnp.concatenate([w_in0[:, o_ckv:o_kr], w_kr, w_kr], axis=1).astype(BF16)
    w_ag = w_in0[:, o_ag:].astype(BF16)

    wq = w_q_b[0].reshape(Q_LORA, N_HEADS, QK_NOPE + QK_ROPE)
    w_qb = jnp.concatenate([wq[:, :, :QK_NOPE].reshape(Q_LORA, N_HEADS * QK_NOPE),
                            wq[:, :, QK_NOPE:].reshape(Q_LORA, N_HEADS * QK_ROPE)], axis=1).astype(BF16)
    wkv = w_kv_b[0].reshape(KV_LORA, N_HEADS, QK_NOPE + V_HEAD)
    w_kvb = jnp.concatenate([wkv[:, :, :QK_NOPE].reshape(KV_LORA, N_HEADS * QK_NOPE),
                             wkv[:, :, QK_NOPE:].reshape(KV_LORA, N_HEADS * V_HEAD)], axis=1).astype(BF16)

    cos_t, sin_t = _rope_tables(N_META + seq)

    def whole(a):
        return pl.BlockSpec(a.shape, lambda b, i: (0,) * a.ndim)

    row_spec = pl.BlockSpec((None, tm, D_MODEL), lambda b, i: (b, i, 0))
    table_spec = pl.BlockSpec((tm, LANES), lambda b, i: (i, 0))

    operands = [
        (x, row_spec),
        (meta_tokens, None),
        (cos_t[N_META:], table_spec),
        (sin_t[N_META:], table_spec),
        (cos_t[:N_META], None),
        (sin_t[:N_META], None),
        (norm_g, None),
        (w_pi, None), (w_pg, None), (w_cq, None), (w_ckv, None), (w_ag, None),
        (q_norm_g, None), (w_qb, None),
        (kv_norm_g, None), (w_kvb, None),
        (pool_w[0].astype(BF16), None),
        (pool_scale, None),
        (w_out[0].astype(BF16), None),
        (final_norm_g.reshape(1, D_MODEL), None),
    ]
    args = [a for a, _ in operands]
    in_specs = [whole(a) if s is None else s for a, s in operands]

    return pl.pallas_call(
        _fused_kernel,
        out_shape=jax.ShapeDtypeStruct(x.shape, x.dtype),
        grid=(batch, seq // tm),
        in_specs=in_specs,
        out_specs=row_spec,
        scratch_shapes=[
            pltpu.VMEM((N_HEADS, seq, QK_PAD), BF16),
            pltpu.VMEM((N_HEADS, seq, V_HEAD), BF16),
            pltpu.VMEM((N_HEADS, META_PAD, QK_PAD), BF16),
            pltpu.VMEM((N_HEADS, META_PAD, V_HEAD), BF16),
            pltpu.VMEM((HALO + tm, D_POOL), F32),
            pltpu.VMEM((HALO, D_POOL), F32),
            pltpu.VMEM((tm, 1), F32),
            pltpu.VMEM((tm, 1), F32),
            pltpu.VMEM((tm, V_HEAD), F32),
        ],
        compiler_params=pltpu.CompilerParams(
            dimension_semantics=("arbitrary", "arbitrary"),
            vmem_limit_bytes=VMEM_LIMIT_BYTES,
        ),
        name="hybrid_pool_mla_block",
    )(*args)
```

```python
import math

import jax
import jax.numpy as jnp
from jax import lax
from jax.experimental import pallas as pl
from jax.experimental.pallas import tpu as pltpu

D_MODEL = 1024
N_META = 16
CHUNK = 64
D_POOL = 512
POOL_WINDOWS = (2, 4, 8, 16)
POOL_GROUP = 128
N_HEADS = 4
QK_NOPE = 128
QK_ROPE = 64
V_HEAD = 128
D_ATTN = N_HEADS * V_HEAD
Q_LORA = 256
KV_LORA = 128
ROPE_THETA = 10000.0
EPS = 1e-6
Q_SCALE = (QK_NOPE + QK_ROPE) ** -0.5 * math.log2(math.e)

LANES = 128
QK_PAD = 2 * LANES
META_PAD = LANES
HALO = 16
ROW_TILE = 512
MASK_VALUE = -1e30
VMEM_LIMIT_BYTES = 56 * 1024 * 1024

F32 = jnp.float32
BF16 = jnp.bfloat16


def _dot(a, b):
    return jnp.dot(a, b, preferred_element_type=F32)


def _dot_nt(a, b):
    return lax.dot_general(a, b, (((1,), (1,)), ((), ())), preferred_element_type=F32)


def _rms(x, g):
    return x * lax.rsqrt(jnp.mean(x * x, axis=-1, keepdims=True) + EPS) * g


def _silu(x):
    return x * (1.0 / (1.0 + jnp.exp(-x)))


def _rope_lanes(x, cos_t, sin_t):
    lane = lax.broadcasted_iota(jnp.int32, x.shape, 1)
    first_half = (lane % QK_ROPE) < (QK_ROPE // 2)
    partner = jnp.where(first_half,
                        pltpu.roll(x, LANES - QK_ROPE // 2, axis=1),
                        pltpu.roll(x, QK_ROPE // 2, axis=1))
    return x * cos_t + partner * sin_t


def _latent_kv(xn, w_ckv_ref, kv_g_ref, w_kb_ref, w_vbt_ref, cos_t, sin_t):
    ckvr = _dot(xn, w_ckv_ref[...])
    ckv_n = _rms(ckvr[:, :KV_LORA], kv_g_ref[...]).astype(BF16)
    k_nope = _dot(ckv_n, w_kb_ref[...])
    v_t = _dot_nt(w_vbt_ref[...], ckv_n)
    k_rope = _rope_lanes(ckvr[:, KV_LORA:], cos_t, sin_t)
    return k_nope, v_t, k_rope


def _fused_kernel(x_ref, meta_ref, kcos_ref, ksin_ref, mcos_ref, msin_ref, qcos_ref, qsin_ref,
                  norm_g_ref, w_pi_ref, w_pg_ref, w_cq_ref, w_ckv_ref, w_ag_ref,
                  q_g_ref, w_qbt_ref, kv_g_ref, w_kb_ref, w_vbt_ref, pool_w_ref, pool_scale_ref,
                  w_out_ref, final_g_ref,
                  out_ref,
                  k_scr, vt_scr, kmeta_scr, vtmeta_scr, pool_scr, halo_scr,
                  qt_scr, m_scr, l_scr, acc_scr):
    b = pl.program_id(0)
    i = pl.program_id(1)
    tm = ROW_TILE

    @pl.when((b == 0) & (i == 0))
    def _meta_prologue():
        xn = _rms(meta_ref[...], norm_g_ref[...]).astype(BF16)
        halo_scr[...] = _dot(xn, w_pi_ref[...])
        k_nope, v_t, k_rope = _latent_kv(xn, w_ckv_ref, kv_g_ref, w_kb_ref, w_vbt_ref,
                                         mcos_ref[...], msin_ref[...])
        kmeta_scr[...] = jnp.zeros_like(kmeta_scr)
        vtmeta_scr[...] = jnp.zeros_like(vtmeta_scr)
        for h in range(N_HEADS):
            kmeta_scr[h, 0:N_META, 0:LANES] = k_nope[:, h * QK_NOPE:(h + 1) * QK_NOPE].astype(BF16)
            kmeta_scr[h, 0:N_META, LANES:QK_PAD] = k_rope.astype(BF16)
            vtmeta_scr[h, :, 0:N_META] = v_t[h * V_HEAD:(h + 1) * V_HEAD, :].astype(BF16)

    @pl.when(i == 0)
    def _seed_halo():
        pool_scr[0:HALO, :] = halo_scr[...]

    x = x_ref[...]
    xn = _rms(x, norm_g_ref[...]).astype(BF16)

    pool_scr[HALO:HALO + tm, :] = _dot(xn, w_pi_ref[...])
    mixed = []
    for g, w in enumerate(POOL_WINDOWS):
        cols = slice(g * POOL_GROUP, (g + 1) * POOL_GROUP)
        cur = pool_scr[HALO:HALO + tm, cols]
        win = cur
        for k in range(1, w):
            win = win + pool_scr[HALO - k:HALO - k + tm, cols]
        pooled = win * (1.0 / w) - cur
        mixed.append(_dot(pooled.astype(BF16), pool_w_ref[g]))
    pool_scr[0:HALO, :] = pool_scr[tm:tm + HALO, :]
    pool_mix = jnp.concatenate(mixed, axis=1) * pool_scale_ref[...]
    pool_out = (_silu(_dot(xn, w_pg_ref[...])) * pool_mix).astype(BF16)

    k_nope, v_t, k_rope = _latent_kv(xn, w_ckv_ref, kv_g_ref, w_kb_ref, w_vbt_ref,
                                     kcos_ref[...], ksin_ref[...])
    row0 = pl.multiple_of(i * tm, tm)
    tile_rows = pl.ds(row0, tm)
    k_rope_bf = k_rope.astype(BF16)
    for h in range(N_HEADS):
        k_scr[h, tile_rows, 0:LANES] = k_nope[:, h * QK_NOPE:(h + 1) * QK_NOPE].astype(BF16)
        k_scr[h, tile_rows, LANES:QK_PAD] = k_rope_bf
        vt_scr[h, :, tile_rows] = v_t[h * V_HEAD:(h + 1) * V_HEAD, :].astype(BF16)

    cq_n = _rms(_dot(xn, w_cq_ref[...]), q_g_ref[...]).astype(BF16)
    q_t = _dot_nt(w_qbt_ref[...], cq_n) * Q_SCALE
    cos_q = qcos_ref[...]
    sin_q = qsin_ref[...]
    half = QK_ROPE // 2
    for h in range(N_HEADS):
        base = h * (QK_NOPE + QK_ROPE)
        x1 = q_t[base + QK_NOPE:base + QK_NOPE + half, :]
        x2 = q_t[base + QK_NOPE + half:base + QK_NOPE + QK_ROPE, :]
        qt_scr[h, 0:QK_NOPE, :] = q_t[base:base + QK_NOPE, :].astype(BF16)
        qt_scr[h, QK_NOPE:QK_NOPE + half, :] = (x1 * cos_q - x2 * sin_q).astype(BF16)
        qt_scr[h, QK_NOPE + half:QK_NOPE + QK_ROPE, :] = (x1 * sin_q + x2 * cos_q).astype(BF16)
        qt_scr[h, QK_NOPE + QK_ROPE:QK_PAD, :] = jnp.zeros((QK_PAD - QK_NOPE - QK_ROPE, tm), BF16)

    meta_valid = lax.broadcasted_iota(jnp.int32, (META_PAD, tm), 0) < N_META
    for h in range(N_HEADS):
        s = jnp.where(meta_valid, _dot(kmeta_scr[h], qt_scr[h]), MASK_VALUE)
        m0 = jnp.max(s, axis=0, keepdims=True)
        p = jnp.exp2(s - m0)
        m_scr[h] = m0
        l_scr[h] = jnp.sum(p, axis=0, keepdims=True)
        acc_scr[h] = _dot(vtmeta_scr[h], p.astype(BF16))

    def _attend(h, rows, valid):
        s = _dot(k_scr[h, rows, :], qt_scr[h])
        if valid is not None:
            s = jnp.where(valid, s, MASK_VALUE)
        m_prev = m_scr[h]
        m_new = jnp.maximum(m_prev, jnp.max(s, axis=0, keepdims=True))
        alpha = jnp.exp2(m_prev - m_new)
        p = jnp.exp2(s - m_new)
        l_scr[h] = alpha * l_scr[h] + jnp.sum(p, axis=0, keepdims=True)
        acc_scr[h] = alpha * acc_scr[h] + _dot(vt_scr[h, :, rows], p.astype(BF16))
        m_scr[h] = m_new

    def _earlier_tile(j, carry):
        rows = pl.ds(pl.multiple_of(j * tm, tm), tm)
        for h in range(N_HEADS):
            _attend(h, rows, None)
        return carry

    lax.fori_loop(0, i, _earlier_tile, 0)

    key_chunk = lax.broadcasted_iota(jnp.int32, (tm, tm), 0) // CHUNK
    query_chunk = lax.broadcasted_iota(jnp.int32, (tm, tm), 1) // CHUNK
    diag_valid = key_chunk <= query_chunk
    attn = []
    for h in range(N_HEADS):
        _attend(h, tile_rows, diag_valid)
        attn.append((acc_scr[h] * (1.0 / l_scr[h])).T)

    attn_out = (_silu(_dot(xn, w_ag_ref[...])) * jnp.concatenate(attn, axis=1)).astype(BF16)

    mix = _dot(jnp.concatenate([pool_out, attn_out], axis=1), w_out_ref[...])
    out_ref[...] = _rms(x + mix, final_g_ref[...])


def _rope_tables(length):
    half = QK_ROPE // 2
    inv_freq = 1.0 / (ROPE_THETA ** (jnp.arange(half, dtype=F32) / half))
    ang = jnp.arange(length, dtype=F32)[:, None] * inv_freq[None, :]
    return jnp.cos(ang), jnp.sin(ang)


def kernel(x, meta_tokens, norm_g, w_in, q_norm_g, w_q_b, kv_norm_g, w_kv_b, pool_w, pool_scale, w_out, final_norm_g):
    batch, seq, d_model = x.shape
    assert d_model == D_MODEL and seq % ROW_TILE == 0 and meta_tokens.shape == (N_META, D_MODEL)
    assert norm_g.shape[0] == 1, "single-layer block"
    tm = ROW_TILE

    w_in0 = w_in[0]
    o_cq = 2 * D_POOL
    o_ckv = o_cq + Q_LORA
    o_kr = o_ckv + KV_LORA
    o_ag = o_kr + QK_ROPE
    w_pi = w_in0[:, :D_POOL].astype(BF16)
    w_pg = w_in0[:, D_POOL:o_cq].astype(BF16)
    w_cq = w_in0[:, o_cq:o_ckv].astype(BF16)
    w_kr = w_in0[:, o_kr:o_ag]
    w_ckv = jnp.concatenate([w_in0[:, o_ckv:o_kr], w_kr, w_kr], axis=1).astype(BF16)
    w_ag = w_in0[:, o_ag:].astype(BF16)

    w_qbt = w_q_b[0].T.astype(BF16)
    wkv = w_kv_b[0].reshape(KV_LORA, N_HEADS, QK_NOPE + V_HEAD)
    w_kb = wkv[:, :, :QK_NOPE].reshape(KV_LORA, N_HEADS * QK_NOPE).astype(BF16)
    w_vbt = wkv[:, :, QK_NOPE:].reshape(KV_LORA, N_HEADS * V_HEAD).T.astype(BF16)

    cos, sin = _rope_tables(N_META + seq)
    kcos = jnp.tile(cos, (1, LANES // (QK_ROPE // 2)))
    ksin = jnp.tile(jnp.concatenate([-sin, sin], axis=1), (1, LANES // QK_ROPE))
    qcos = cos[N_META:].T
    qsin = sin[N_META:].T

    def whole(a):
        return pl.BlockSpec(a.shape, lambda b, i: (0,) * a.ndim)

    row_spec = pl.BlockSpec((None, tm, D_MODEL), lambda b, i: (b, i, 0))
    ktable_spec = pl.BlockSpec((tm, LANES), lambda b, i: (i, 0))
    qtable_spec = pl.BlockSpec((QK_ROPE // 2, tm), lambda b, i: (0, i))

    operands = [
        (x, row_spec),
        (meta_tokens, None),
        (kcos[N_META:], ktable_spec),
        (ksin[N_META:], ktable_spec),
        (kcos[:N_META], None),
        (ksin[:N_META], None),
        (qcos, qtable_spec),
        (qsin, qtable_spec),
        (norm_g, None),
        (w_pi, None), (w_pg, None), (w_cq, None), (w_ckv, None), (w_ag, None),
        (q_norm_g, None), (w_qbt, None),
        (kv_norm_g, None), (w_kb, None), (w_vbt, None),
        (pool_w[0].astype(BF16), None),
        (pool_scale, None),
        (w_out[0].astype(BF16), None),
        (final_norm_g.reshape(1, D_MODEL), None),
    ]
    args = [a for a, _ in operands]
    in_specs = [whole(a) if s is None else s for a, s in operands]

    return pl.pallas_call(
        _fused_kernel,
        out_shape=jax.ShapeDtypeStruct(x.shape, x.dtype),
        grid=(batch, seq // tm),
        in_specs=in_specs,
        out_specs=row_spec,
        scratch_shapes=[
            pltpu.VMEM((N_HEADS, seq, QK_PAD), BF16),
            pltpu.VMEM((N_HEADS, V_HEAD, seq), BF16),
            pltpu.VMEM((N_HEADS, META_PAD, QK_PAD), BF16),
            pltpu.VMEM((N_HEADS, V_HEAD, META_PAD), BF16),
            pltpu.VMEM((HALO + tm, D_POOL), F32),
            pltpu.VMEM((HALO, D_POOL), F32),
            pltpu.VMEM((N_HEADS, QK_PAD, tm), BF16),
            pltpu.VMEM((N_HEADS, 1, tm), F32),
            pltpu.VMEM((N_HEADS, 1, tm), F32),
            pltpu.VMEM((N_HEADS, V_HEAD, tm), F32),
        ],
        compiler_params=pltpu.CompilerParams(
            dimension_semantics=("arbitrary", "arbitrary"),
            vmem_limit_bytes=VMEM_LIMIT_BYTES,
        ),
        name="hybrid_pool_mla_block",
    )(*args)
```

```python
import math

import jax
import jax.numpy as jnp
from jax import lax
from jax.experimental import pallas as pl
from jax.experimental.pallas import tpu as pltpu

D_MODEL = 1024
N_META = 16
CHUNK = 64
D_POOL = 512
POOL_WINDOWS = (2, 4, 8, 16)
POOL_GROUP = 128
N_HEADS = 4
QK_NOPE = 128
QK_ROPE = 64
V_HEAD = 128
D_ATTN = N_HEADS * V_HEAD
Q_LORA = 256
KV_LORA = 128
ROPE_THETA = 10000.0
EPS = 1e-6
Q_SCALE = (QK_NOPE + QK_ROPE) ** -0.5 * math.log2(math.e)

LANES = 128
QK_PAD = 2 * LANES
META_PAD = LANES
HALO = 16
ROW_TILE = 512
MASK_VALUE = -1e30
VMEM_LIMIT_BYTES = 56 * 1024 * 1024

F32 = jnp.float32
BF16 = jnp.bfloat16


def _dot(a, b):
    return jnp.dot(a, b, preferred_element_type=F32)


def _dot_nt(a, b):
    return lax.dot_general(a, b, (((1,), (1,)), ((), ())), preferred_element_type=F32)


def _rms(x, g):
    return x * lax.rsqrt(jnp.mean(x * x, axis=-1, keepdims=True) + EPS) * g


def _silu(x):
    h = 0.5 * x
    return h + h * jnp.tanh(h)


def _rope_lanes(x, cos_t, sin_t):
    lane = lax.broadcasted_iota(jnp.int32, x.shape, 1)
    first_half = (lane % QK_ROPE) < (QK_ROPE // 2)
    partner = jnp.where(first_half,
                        pltpu.roll(x, LANES - QK_ROPE // 2, axis=1),
                        pltpu.roll(x, QK_ROPE // 2, axis=1))
    return x * cos_t + partner * sin_t


def _latent_kv(ckvr, kv_g_ref, w_kb_ref, w_vbt_ref, cos_t, sin_t):
    ckv_n = _rms(ckvr[:, :KV_LORA], kv_g_ref[...]).astype(BF16)
    k_nope = _dot(ckv_n, w_kb_ref[...])
    v_t = _dot_nt(w_vbt_ref[...], ckv_n)
    k_rope = _rope_lanes(ckvr[:, KV_LORA:], cos_t, sin_t)
    return k_nope, v_t, k_rope


def _window_sum(xg, w):
    y = xg
    span = 1
    while span < w:
        y = y + pltpu.roll(y, span, axis=0)
        span *= 2
    return y


def _fused_kernel(x_ref, meta_ref, kcos_ref, ksin_ref, mcos_ref, msin_ref, qcos_ref, qsin_ref,
                  norm_g_ref, w_pi_ref, w_pg_ref, w_cq_ref, w_ckv_ref, w_ag_ref,
                  q_g_ref, w_qbt_ref, kv_g_ref, w_kb_ref, w_vbt_ref, pool_w_ref, pool_scale_ref,
                  w_out_ref, final_g_ref,
                  out_ref,
                  k_scr, vt_scr, kmeta_scr, vtmeta_scr, pool_scr, halo_scr,
                  qt_scr, m_scr, l_scr, acc_scr):
    b = pl.program_id(0)
    i = pl.program_id(1)
    tm = ROW_TILE

    @pl.when((b == 0) & (i == 0))
    def _meta_prologue():
        xn = _rms(meta_ref[...], norm_g_ref[...]).astype(BF16)
        halo_scr[...] = _dot(xn, w_pi_ref[...])
        k_nope, v_t, k_rope = _latent_kv(_dot(xn, w_ckv_ref[...]), kv_g_ref, w_kb_ref, w_vbt_ref,
                                         mcos_ref[...], msin_ref[...])
        kmeta_scr[...] = jnp.zeros_like(kmeta_scr)
        vtmeta_scr[...] = jnp.zeros_like(vtmeta_scr)
        for h in range(N_HEADS):
            kmeta_scr[h, 0:N_META, 0:LANES] = k_nope[:, h * QK_NOPE:(h + 1) * QK_NOPE].astype(BF16)
            kmeta_scr[h, 0:N_META, LANES:QK_PAD] = k_rope.astype(BF16)
            vtmeta_scr[h, :, 0:N_META] = v_t[h * V_HEAD:(h + 1) * V_HEAD, :].astype(BF16)

    @pl.when(i == 0)
    def _seed_halo():
        pool_scr[0:HALO, :] = halo_scr[...]

    x = x_ref[...]
    xn = _rms(x, norm_g_ref[...]).astype(BF16)

    pool_scr[HALO:HALO + tm, :] = _dot(xn, w_pi_ref[...])
    cq = _dot(xn, w_cq_ref[...])
    ckvr = _dot(xn, w_ckv_ref[...])

    pooled = []
    for g, w in enumerate(POOL_WINDOWS):
        xg = pool_scr[:, g * POOL_GROUP:(g + 1) * POOL_GROUP]
        win = _window_sum(xg, w)
        pooled.append((win[HALO:, :] * (1.0 / w) - xg[HALO:, :]).astype(BF16))
    pool_scr[0:HALO, :] = pool_scr[tm:tm + HALO, :]

    cq_n = _rms(cq, q_g_ref[...]).astype(BF16)
    q_t = _dot_nt(w_qbt_ref[...], cq_n) * Q_SCALE
    k_nope, v_t, k_rope = _latent_kv(ckvr, kv_g_ref, w_kb_ref, w_vbt_ref, kcos_ref[...], ksin_ref[...])
    pool_gate = _dot(xn, w_pg_ref[...])
    attn_gate = _dot(xn, w_ag_ref[...])

    row0 = pl.multiple_of(i * tm, tm)
    tile_rows = pl.ds(row0, tm)
    k_rope_bf = k_rope.astype(BF16)
    for h in range(N_HEADS):
        k_scr[h, tile_rows, 0:LANES] = k_nope[:, h * QK_NOPE:(h + 1) * QK_NOPE].astype(BF16)
        k_scr[h, tile_rows, LANES:QK_PAD] = k_rope_bf
        vt_scr[h, :, tile_rows] = v_t[h * V_HEAD:(h + 1) * V_HEAD, :].astype(BF16)

    cos_q = qcos_ref[...]
    sin_q = qsin_ref[...]
    half = QK_ROPE // 2
    for h in range(N_HEADS):
        base = h * (QK_NOPE + QK_ROPE)
        x1 = q_t[base + QK_NOPE:base + QK_NOPE + half, :]
        x2 = q_t[base + QK_NOPE + half:base + QK_NOPE + QK_ROPE, :]
        qt_scr[h, 0:QK_NOPE, :] = q_t[base:base + QK_NOPE, :].astype(BF16)
        qt_scr[h, QK_NOPE:QK_NOPE + half, :] = (x1 * cos_q - x2 * sin_q).astype(BF16)
        qt_scr[h, QK_NOPE + half:QK_NOPE + QK_ROPE, :] = (x1 * sin_q + x2 * cos_q).astype(BF16)
        qt_scr[h, QK_NOPE + QK_ROPE:QK_PAD, :] = jnp.zeros((QK_PAD - QK_NOPE - QK_ROPE, tm), BF16)

    mixed = [_dot(pooled[g], pool_w_ref[g]) for g in range(len(POOL_WINDOWS))]
    pool_out = (_silu(pool_gate) * (jnp.concatenate(mixed, axis=1) * pool_scale_ref[...])).astype(BF16)
    gate = _silu(attn_gate)

    def _scores(h, k_blk, valid):
        s = _dot(k_blk, qt_scr[h])
        return s if valid is None else jnp.where(valid, s, MASK_VALUE)

    def _first_update(h, s, vt_blk):
        m0 = jnp.max(s, axis=0, keepdims=True)
        p = jnp.exp2(s - m0)
        m_scr[h] = m0
        l_scr[h] = jnp.sum(p, axis=0, keepdims=True)
        acc_scr[h] = _dot(vt_blk, p.astype(BF16))

    def _update(h, s, vt_blk):
        m_prev = m_scr[h]
        m_new = jnp.maximum(m_prev, jnp.max(s, axis=0, keepdims=True))
        alpha = jnp.exp2(m_prev - m_new)
        p = jnp.exp2(s - m_new)
        l_scr[h] = alpha * l_scr[h] + jnp.sum(p, axis=0, keepdims=True)
        acc_scr[h] = alpha * acc_scr[h] + _dot(vt_blk, p.astype(BF16))
        m_scr[h] = m_new

    def _all_heads(k_of, vt_of, valid, update):
        s_next = _scores(0, k_of(0), valid)
        for h in range(N_HEADS):
            s = s_next
            if h + 1 < N_HEADS:
                s_next = _scores(h + 1, k_of(h + 1), valid)
            update(h, s, vt_of(h))

    meta_valid = lax.broadcasted_iota(jnp.int32, (META_PAD, tm), 0) < N_META
    _all_heads(lambda h: kmeta_scr[h], lambda h: vtmeta_scr[h], meta_valid, _first_update)

    def _earlier_tile(j, carry):
        rows = pl.ds(pl.multiple_of(j * tm, tm), tm)
        _all_heads(lambda h: k_scr[h, rows, :], lambda h: vt_scr[h, :, rows], None, _update)
        return carry

    lax.fori_loop(0, i, _earlier_tile, 0)

    key_chunk = lax.broadcasted_iota(jnp.int32, (tm, tm), 0) // CHUNK
    query_chunk = lax.broadcasted_iota(jnp.int32, (tm, tm), 1) // CHUNK
    _all_heads(lambda h: k_scr[h, tile_rows, :], lambda h: vt_scr[h, :, tile_rows],
               key_chunk <= query_chunk, _update)

    attn = [(acc_scr[h] * (1.0 / l_scr[h])).T for h in range(N_HEADS)]
    attn_out = (gate * jnp.concatenate(attn, axis=1)).astype(BF16)

    mix = _dot(jnp.concatenate([pool_out, attn_out], axis=1), w_out_ref[...])
    out_ref[...] = _rms(x + mix, final_g_ref[...])


def _rope_tables(length):
    half = QK_ROPE // 2
    inv_freq = 1.0 / (ROPE_THETA ** (jnp.arange(half, dtype=F32) / half))
    ang = jnp.arange(length, dtype=F32)[:, None] * inv_freq[None, :]
    return jnp.cos(ang), jnp.sin(ang)


def kernel(x, meta_tokens, norm_g, w_in, q_norm_g, w_q_b, kv_norm_g, w_kv_b, pool_w, pool_scale, w_out, final_norm_g):
    batch, seq, d_model = x.shape
    assert d_model == D_MODEL and seq % ROW_TILE == 0 and meta_tokens.shape == (N_META, D_MODEL)
    assert norm_g.shape[0] == 1, "single-layer block"
    tm = ROW_TILE

    w_in0 = w_in[0]
    o_cq = 2 * D_POOL
    o_ckv = o_cq + Q_LORA
    o_kr = o_ckv + KV_LORA
    o_ag = o_kr + QK_ROPE
    w_pi = w_in0[:, :D_POOL].astype(BF16)
    w_pg = w_in0[:, D_POOL:o_cq].astype(BF16)
    w_cq = w_in0[:, o_cq:o_ckv].astype(BF16)
    w_kr = w_in0[:, o_kr:o_ag]
    w_ckv = jnp.concatenate([w_in0[:, o_ckv:o_kr], w_kr, w_kr], axis=1).astype(BF16)
    w_ag = w_in0[:, o_ag:].astype(BF16)

    w_qbt = w_q_b[0].T.astype(BF16)
    wkv = w_kv_b[0].reshape(KV_LORA, N_HEADS, QK_NOPE + V_HEAD)
    w_kb = wkv[:, :, :QK_NOPE].reshape(KV_LORA, N_HEADS * QK_NOPE).astype(BF16)
    w_vbt = wkv[:, :, QK_NOPE:].reshape(KV_LORA, N_HEADS * V_HEAD).T.astype(BF16)

    cos, sin = _rope_tables(N_META + seq)
    kcos = jnp.tile(cos, (1, LANES // (QK_ROPE // 2)))
    ksin = jnp.tile(jnp.concatenate([-sin, sin], axis=1), (1, LANES // QK_ROPE))
    qcos = cos[N_META:].T
    qsin = sin[N_META:].T

    def whole(a):
        return pl.BlockSpec(a.shape, lambda b, i: (0,) * a.ndim)

    row_spec = pl.BlockSpec((None, tm, D_MODEL), lambda b, i: (b, i, 0))
    ktable_spec = pl.BlockSpec((tm, LANES), lambda b, i: (i, 0))
    qtable_spec = pl.BlockSpec((QK_ROPE // 2, tm), lambda b, i: (0, i))

    operands = [
        (x, row_spec),
        (meta_tokens, None),
        (kcos[N_META:], ktable_spec),
        (ksin[N_META:], ktable_spec),
        (kcos[:N_META], None),
        (ksin[:N_META], None),
        (qcos, qtable_spec),
        (qsin, qtable_spec),
        (norm_g, None),
        (w_pi, None), (w_pg, None), (w_cq, None), (w_ckv, None), (w_ag, None),
        (q_norm_g, None), (w_qbt, None),
        (kv_norm_g, None), (w_kb, None), (w_vbt, None),
        (pool_w[0].astype(BF16), None),
        (pool_scale, None),
        (w_out[0].astype(BF16), None),
        (final_norm_g.reshape(1, D_MODEL), None),
    ]
    args = [a for a, _ in operands]
    in_specs = [whole(a) if s is None else s for a, s in operands]

    return pl.pallas_call(
        _fused_kernel,
        out_shape=jax.ShapeDtypeStruct(x.shape, x.dtype),
        grid=(batch, seq // tm),
        in_specs=in_specs,
        out_specs=row_spec,
        scratch_shapes=[
            pltpu.VMEM((N_HEADS, seq, QK_PAD), BF16),
            pltpu.VMEM((N_HEADS, V_HEAD, seq), BF16),
            pltpu.VMEM((N_HEADS, META_PAD, QK_PAD), BF16),
            pltpu.VMEM((N_HEADS, V_HEAD, META_PAD), BF16),
            pltpu.VMEM((HALO + tm, D_POOL), F32),
            pltpu.VMEM((HALO, D_POOL), F32),
            pltpu.VMEM((N_HEADS, QK_PAD, tm), BF16),
            pltpu.VMEM((N_HEADS, 1, tm), F32),
            pltpu.VMEM((N_HEADS, 1, tm), F32),
            pltpu.VMEM((N_HEADS, V_HEAD, tm), F32),
        ],
        compiler_params=pltpu.CompilerParams(
            dimension_semantics=("arbitrary", "arbitrary"),
            vmem_limit_bytes=VMEM_LIMIT_BYTES,
        ),
        name="hybrid_pool_mla_block",
    )(*args)
```

```python
import math

import jax
import jax.numpy as jnp
from jax import lax
from jax.experimental import pallas as pl
from jax.experimental.pallas import tpu as pltpu

D_MODEL = 1024
N_META = 16
CHUNK = 64
D_POOL = 512
POOL_WINDOWS = (2, 4, 8, 16)
POOL_GROUP = 128
N_HEADS = 4
QK_NOPE = 128
QK_ROPE = 64
V_HEAD = 128
D_ATTN = N_HEADS * V_HEAD
Q_LORA = 256
KV_LORA = 128
ROPE_THETA = 10000.0
EPS = 1e-6
Q_SCALE = (QK_NOPE + QK_ROPE) ** -0.5 * math.log2(math.e)

LANES = 128
QK_PAD = 2 * LANES
META_PAD = LANES
HALO = 16
ROW_TILE = 512
MASK_VALUE = -1e30
VMEM_LIMIT_BYTES = 56 * 1024 * 1024

F32 = jnp.float32
BF16 = jnp.bfloat16


def _dot(a, b):
    return jnp.dot(a, b, preferred_element_type=F32)


def _dot_nt(a, b):
    return lax.dot_general(a, b, (((1,), (1,)), ((), ())), preferred_element_type=F32)


def _rms(x, g):
    return x * lax.rsqrt(jnp.mean(x * x, axis=-1, keepdims=True) + EPS) * g


def _silu(x):
    h = 0.5 * x
    return h + h * jnp.tanh(h)


def _rope_lanes(x, cos_t, sin_t):
    lane = lax.broadcasted_iota(jnp.int32, x.shape, 1)
    first_half = (lane % QK_ROPE) < (QK_ROPE // 2)
    partner = jnp.where(first_half,
                        pltpu.roll(x, LANES - QK_ROPE // 2, axis=1),
                        pltpu.roll(x, QK_ROPE // 2, axis=1))
    return x * cos_t + partner * sin_t


def _latent_kv(ckvr, kv_g_ref, w_kb_ref, w_vbt_ref, cos_t, sin_t):
    ckv_n = _rms(ckvr[:, :KV_LORA], kv_g_ref[...]).astype(BF16)
    k_nope = _dot(ckv_n, w_kb_ref[...])
    v_t = _dot_nt(w_vbt_ref[...], ckv_n)
    k_rope = _rope_lanes(ckvr[:, KV_LORA:], cos_t, sin_t)
    return k_nope, v_t, k_rope


def _window_sum(xg, w):
    y = xg
    span = 1
    while span < w:
        y = y + pltpu.roll(y, span, axis=0)
        span *= 2
    return y


def _fused_kernel(x_ref, meta_ref, kcos_ref, ksin_ref, mcos_ref, msin_ref, qcos_ref, qsin_ref,
                  norm_g_ref, w_pi_ref, w_pg_ref, w_cq_ref, w_ckv_ref, w_ag_ref,
                  q_g_ref, w_qbt_ref, kv_g_ref, w_kb_ref, w_vbt_ref, pool_w_ref, pool_scale_ref,
                  w_out_ref, final_g_ref,
                  out_ref,
                  k_scr, vt_scr, kmeta_scr, vtmeta_scr, pool_scr, halo_scr,
                  qt_scr, s_scr, m_scr, l_scr, acc_scr):
    b = pl.program_id(0)
    i = pl.program_id(1)
    tm = ROW_TILE

    @pl.when((b == 0) & (i == 0))
    def _meta_prologue():
        xn = _rms(meta_ref[...], norm_g_ref[...]).astype(BF16)
        halo_scr[...] = _dot(xn, w_pi_ref[...])
        k_nope, v_t, k_rope = _latent_kv(_dot(xn, w_ckv_ref[...]), kv_g_ref, w_kb_ref, w_vbt_ref,
                                         mcos_ref[...], msin_ref[...])
        kmeta_scr[...] = jnp.zeros_like(kmeta_scr)
        vtmeta_scr[...] = jnp.zeros_like(vtmeta_scr)
        for h in range(N_HEADS):
            kmeta_scr[h, 0:N_META, 0:LANES] = k_nope[:, h * QK_NOPE:(h + 1) * QK_NOPE].astype(BF16)
            kmeta_scr[h, 0:N_META, LANES:QK_PAD] = k_rope.astype(BF16)
            vtmeta_scr[h, :, 0:N_META] = v_t[h * V_HEAD:(h + 1) * V_HEAD, :].astype(BF16)

    @pl.when(i == 0)
    def _seed_halo():
        pool_scr[0:HALO, :] = halo_scr[...]

    x = x_ref[...]
    xn = _rms(x, norm_g_ref[...]).astype(BF16)

    pool_scr[HALO:HALO + tm, :] = _dot(xn, w_pi_ref[...])
    cq = _dot(xn, w_cq_ref[...])
    ckvr = _dot(xn, w_ckv_ref[...])
    pool_gate = _dot(xn, w_pg_ref[...])

    cq_n = _rms(cq, q_g_ref[...]).astype(BF16)
    q_t = _dot_nt(w_qbt_ref[...], cq_n) * Q_SCALE
    k_nope, v_t, k_rope = _latent_kv(ckvr, kv_g_ref, w_kb_ref, w_vbt_ref, kcos_ref[...], ksin_ref[...])
    attn_gate = _dot(xn, w_ag_ref[...])

    pooled = []
    for g, w in enumerate(POOL_WINDOWS):
        xg = pool_scr[:, g * POOL_GROUP:(g + 1) * POOL_GROUP]
        win = _window_sum(xg, w)
        pooled.append((win[HALO:, :] * (1.0 / w) - xg[HALO:, :]).astype(BF16))
    pool_scr[0:HALO, :] = pool_scr[tm:tm + HALO, :]

    row0 = pl.multiple_of(i * tm, tm)
    tile_rows = pl.ds(row0, tm)
    k_rope_bf = k_rope.astype(BF16)
    for h in range(N_HEADS):
        k_scr[h, tile_rows, 0:LANES] = k_nope[:, h * QK_NOPE:(h + 1) * QK_NOPE].astype(BF16)
        k_scr[h, tile_rows, LANES:QK_PAD] = k_rope_bf
        vt_scr[h, :, tile_rows] = v_t[h * V_HEAD:(h + 1) * V_HEAD, :].astype(BF16)

    cos_q = qcos_ref[...]
    sin_q = qsin_ref[...]
    half = QK_ROPE // 2
    for h in range(N_HEADS):
        base = h * (QK_NOPE + QK_ROPE)
        x1 = q_t[base + QK_NOPE:base + QK_NOPE + half, :]
        x2 = q_t[base + QK_NOPE + half:base + QK_NOPE + QK_ROPE, :]
        qt_scr[h, 0:QK_NOPE, :] = q_t[base:base + QK_NOPE, :].astype(BF16)
        qt_scr[h, QK_NOPE:QK_NOPE + half, :] = (x1 * cos_q - x2 * sin_q).astype(BF16)
        qt_scr[h, QK_NOPE + half:QK_NOPE + QK_ROPE, :] = (x1 * sin_q + x2 * cos_q).astype(BF16)
        qt_scr[h, QK_NOPE + QK_ROPE:QK_PAD, :] = jnp.zeros((QK_PAD - QK_NOPE - QK_ROPE, tm), BF16)

    mixed = [_dot(pooled[g], pool_w_ref[g]) for g in range(len(POOL_WINDOWS))]
    pool_out = (_silu(pool_gate) * (jnp.concatenate(mixed, axis=1) * pool_scale_ref[...])).astype(BF16)
    gate = _silu(attn_gate)

    def _scores(h, rows):
        return _dot(k_scr[h, rows, :], qt_scr[h])

    def _update(h, blocks):
        m_prev = m_scr[h]
        m_new = m_prev
        for s, _ in blocks:
            m_new = jnp.maximum(m_new, jnp.max(s, axis=0, keepdims=True))
        alpha = jnp.exp2(m_prev - m_new)
        l_new = alpha * l_scr[h]
        acc_new = alpha * acc_scr[h]
        for s, vt_blk in blocks:
            p = jnp.exp2(s - m_new)
            l_new = l_new + jnp.sum(p, axis=0, keepdims=True)
            acc_new = acc_new + _dot(vt_blk, p.astype(BF16))
        m_scr[h] = m_new
        l_scr[h] = l_new
        acc_scr[h] = acc_new

    for h in range(N_HEADS):
        m_scr[h] = jnp.full((1, tm), MASK_VALUE, F32)
        l_scr[h] = jnp.zeros((1, tm), F32)
        acc_scr[h] = jnp.zeros((V_HEAD, tm), F32)
    s_scr[...] = _scores(0, pl.ds(0, tm))

    def _earlier_tile(j, carry):
        rows = pl.ds(pl.multiple_of(j * tm, tm), tm)
        next_rows = pl.ds(pl.multiple_of((j + 1) * tm, tm), tm)
        s_next = s_scr[...]
        for h in range(N_HEADS):
            s = s_next
            if h + 1 < N_HEADS:
                s_next = _scores(h + 1, rows)
            else:
                s_scr[...] = _scores(0, next_rows)
            _update(h, [(s, vt_scr[h, :, rows])])
        return carry

    lax.fori_loop(0, i, _earlier_tile, 0)

    key_chunk = lax.broadcasted_iota(jnp.int32, (tm, tm), 0) // CHUNK
    query_chunk = lax.broadcasted_iota(jnp.int32, (tm, tm), 1) // CHUNK
    diag_valid = key_chunk <= query_chunk
    meta_valid = lax.broadcasted_iota(jnp.int32, (META_PAD, tm), 0) < N_META
    s_next = jnp.where(diag_valid, s_scr[...], MASK_VALUE)
    for h in range(N_HEADS):
        s = s_next
        s_meta = jnp.where(meta_valid, _dot(kmeta_scr[h], qt_scr[h]), MASK_VALUE)
        if h + 1 < N_HEADS:
            s_next = jnp.where(diag_valid, _scores(h + 1, tile_rows), MASK_VALUE)
        else:
            mix_pool = _dot(pool_out, w_out_ref[0:D_POOL, :])
        _update(h, [(s, vt_scr[h, :, tile_rows]), (s_meta, vtmeta_scr[h])])

    attn = [(acc_scr[h] * (1.0 / l_scr[h])).T for h in range(N_HEADS)]
    attn_out = (gate * jnp.concatenate(attn, axis=1)).astype(BF16)

    mix = mix_pool + _dot(attn_out, w_out_ref[D_POOL:, :])
    out_ref[...] = _rms(x + mix, final_g_ref[...])


def _rope_tables(length):
    half = QK_ROPE // 2
    inv_freq = 1.0 / (ROPE_THETA ** (jnp.arange(half, dtype=F32) / half))
    ang = jnp.arange(length, dtype=F32)[:, None] * inv_freq[None, :]
    return jnp.cos(ang), jnp.sin(ang)


def kernel(x, meta_tokens, norm_g, w_in, q_norm_g, w_q_b, kv_norm_g, w_kv_b, pool_w, pool_scale, w_out, final_norm_g):
    batch, seq, d_model = x.shape
    assert d_model == D_MODEL and seq % ROW_TILE == 0 and meta_tokens.shape == (N_META, D_MODEL)
    assert norm_g.shape[0] == 1, "single-layer block"
    tm = ROW_TILE

    w_in0 = w_in[0]
    o_cq = 2 * D_POOL
    o_ckv = o_cq + Q_LORA
    o_kr = o_ckv + KV_LORA
    o_ag = o_kr + QK_ROPE
    w_pi = w_in0[:, :D_POOL].astype(BF16)
    w_pg = w_in0[:, D_POOL:o_cq].astype(BF16)
    w_cq = w_in0[:, o_cq:o_ckv].astype(BF16)
    w_kr = w_in0[:, o_kr:o_ag]
    w_ckv = jnp.concatenate([w_in0[:, o_ckv:o_kr], w_kr, w_kr], axis=1).astype(BF16)
    w_ag = w_in0[:, o_ag:].astype(BF16)

    w_qbt = w_q_b[0].T.astype(BF16)
    wkv = w_kv_b[0].reshape(KV_LORA, N_HEADS, QK_NOPE + V_HEAD)
    w_kb = wkv[:, :, :QK_NOPE].reshape(KV_LORA, N_HEADS * QK_NOPE).astype(BF16)
    w_vbt = wkv[:, :, QK_NOPE:].reshape(KV_LORA, N_HEADS * V_HEAD).T.astype(BF16)

    cos, sin = _rope_tables(N_META + seq)
    kcos = jnp.tile(cos, (1, LANES // (QK_ROPE // 2)))
    ksin = jnp.tile(jnp.concatenate([-sin, sin], axis=1), (1, LANES // QK_ROPE))
    qcos = cos[N_META:].T
    qsin = sin[N_META:].T

    def whole(a):
        return pl.BlockSpec(a.shape, lambda b, i: (0,) * a.ndim)

    row_spec = pl.BlockSpec((None, tm, D_MODEL), lambda b, i: (b, i, 0))
    ktable_spec = pl.BlockSpec((tm, LANES), lambda b, i: (i, 0))
    qtable_spec = pl.BlockSpec((QK_ROPE // 2, tm), lambda b, i: (0, i))

    operands = [
        (x, row_spec),
        (meta_tokens, None),
        (kcos[N_META:], ktable_spec),
        (ksin[N_META:], ktable_spec),
        (kcos[:N_META], None),
        (ksin[:N_META], None),
        (qcos, qtable_spec),
        (qsin, qtable_spec),
        (norm_g, None),
        (w_pi, None), (w_pg, None), (w_cq, None), (w_ckv, None), (w_ag, None),
        (q_norm_g, None), (w_qbt, None),
        (kv_norm_g, None), (w_kb, None), (w_vbt, None),
        (pool_w[0].astype(BF16), None),
        (pool_scale, None),
        (w_out[0].astype(BF16), None),
        (final_norm_g.reshape(1, D_MODEL), None),
    ]
    args = [a for a, _ in operands]
    in_specs = [whole(a) if s is None else s for a, s in operands]

    return pl.pallas_call(
        _fused_kernel,
        out_shape=jax.ShapeDtypeStruct(x.shape, x.dtype),
        grid=(batch, seq // tm),
        in_specs=in_specs,
        out_specs=row_spec,
        scratch_shapes=[
            pltpu.VMEM((N_HEADS, seq, QK_PAD), BF16),
            pltpu.VMEM((N_HEADS, V_HEAD, seq), BF16),
            pltpu.VMEM((N_HEADS, META_PAD, QK_PAD), BF16),
            pltpu.VMEM((N_HEADS, V_HEAD, META_PAD), BF16),
            pltpu.VMEM((HALO + tm, D_POOL), F32),
            pltpu.VMEM((HALO, D_POOL), F32),
            pltpu.VMEM((N_HEADS, QK_PAD, tm), BF16),
            pltpu.VMEM((tm, tm), F32),
            pltpu.VMEM((N_HEADS, 1, tm), F32),
            pltpu.VMEM((N_HEADS, 1, tm), F32),
            pltpu.VMEM((N_HEADS, V_HEAD, tm), F32),
        ],
        compiler_params=pltpu.CompilerParams(
            dimension_semantics=("arbitrary", "arbitrary"),
            vmem_limit_bytes=VMEM_LIMIT_BYTES,
        ),
        name="hybrid_pool_mla_block",
    )(*args)
```

```python
import math

import jax
import jax.numpy as jnp
from jax import lax
from jax.experimental import pallas as pl
from jax.experimental.pallas import tpu as pltpu

D_MODEL = 1024
N_META = 16
CHUNK = 64
D_POOL = 512
POOL_WINDOWS = (2, 4, 8, 16)
POOL_GROUP = 128
N_HEADS = 4
QK_NOPE = 128
QK_ROPE = 64
V_HEAD = 128
D_ATTN = N_HEADS * V_HEAD
Q_LORA = 256
KV_LORA = 128
ROPE_THETA = 10000.0
EPS = 1e-6
Q_SCALE = (QK_NOPE + QK_ROPE) ** -0.5 * math.log2(math.e)

O_CQ = 2 * D_POOL
O_CKV = O_CQ + Q_LORA
O_KR = O_CKV + KV_LORA
O_AG = O_KR + QK_ROPE
D_IN = O_AG + D_ATTN
S_AG = O_AG + QK_ROPE
D_IN_PADDED = S_AG + D_ATTN

LANES = 128
QK_PAD = 2 * LANES
META_PAD = LANES
HALO = 16
ROW_TILE = 512
MASK_VALUE = -1e30
VMEM_LIMIT_BYTES = 56 * 1024 * 1024

F32 = jnp.float32
BF16 = jnp.bfloat16


def _dot(a, b):
    return jnp.dot(a, b, preferred_element_type=F32)


def _dot_nt(a, b):
    return lax.dot_general(a, b, (((1,), (1,)), ((), ())), preferred_element_type=F32)


def _rms(x, g):
    return x * lax.rsqrt(jnp.mean(x * x, axis=-1, keepdims=True) + EPS) * g


def _silu(x):
    h = 0.5 * x
    return h + h * jnp.tanh(h)


def _rope_lanes(x, cos_t, sin_t):
    lane = lax.broadcasted_iota(jnp.int32, x.shape, 1)
    first_half = (lane % QK_ROPE) < (QK_ROPE // 2)
    partner = jnp.where(first_half,
                        pltpu.roll(x, LANES - QK_ROPE // 2, axis=1),
                        pltpu.roll(x, QK_ROPE // 2, axis=1))
    return x * cos_t + partner * sin_t


def _latent_kv(ckvr, kv_g_ref, w_kb_ref, w_vbt_ref, cos_t, sin_t):
    ckv_n = _rms(ckvr[:, :KV_LORA], kv_g_ref[...]).astype(BF16)
    k_nope = _dot(ckv_n, w_kb_ref[...])
    v_t = _dot_nt(w_vbt_ref[...], ckv_n)
    k_rope = _rope_lanes(ckvr[:, KV_LORA:], cos_t, sin_t)
    return k_nope, v_t, k_rope


def _window_sum(xg, w):
    y = xg
    span = 1
    while span < w:
        y = y + pltpu.roll(y, span, axis=0)
        span *= 2
    return y


def _fused_kernel(x_ref, meta_ref, kcos_ref, ksin_ref, mcos_ref, msin_ref, qcos_ref, qsin_ref,
                  norm_g_ref, w_in_ref, q_g_ref, w_qb_ref, kv_g_ref, w_kvb_ref,
                  pool_w_f32_ref, pool_scale_ref, w_out_f32_ref, final_g_ref,
                  out_ref,
                  w_in_s, w_qbt_ref, w_kb_ref, w_vbt_ref, pool_w_ref, w_out_ref,
                  k_scr, vt_scr, kmeta_scr, vtmeta_scr, pool_scr, halo_scr,
                  qt_scr, s_scr, m_scr, l_scr, acc_scr):
    b = pl.program_id(0)
    i = pl.program_id(1)
    tm = ROW_TILE
    first_step = (b == 0) & (i == 0)

    w_pi_ref = w_in_s.at[:, 0:D_POOL]
    w_pg_ref = w_in_s.at[:, D_POOL:O_CQ]
    w_cq_ref = w_in_s.at[:, O_CQ:O_CKV]
    w_ckv_ref = w_in_s.at[:, O_CKV:S_AG]
    w_ag_ref = w_in_s.at[:, S_AG:S_AG + D_ATTN]

    @pl.when(first_step)
    def _prepare_weights():
        quarter = D_MODEL // 4
        for r in range(4):
            rs = slice(r * quarter, (r + 1) * quarter)
            w_in_s[rs, 0:O_AG] = w_in_ref[rs, 0:O_AG].astype(BF16)
            w_in_s[rs, O_AG:S_AG] = w_in_ref[rs, O_KR:O_AG].astype(BF16)
            w_in_s[rs, S_AG:S_AG + D_ATTN] = w_in_ref[rs, O_AG:O_AG + D_ATTN].astype(BF16)
            w_out_ref[rs, :] = w_out_f32_ref[rs, :].astype(BF16)
        w_qbt_ref[...] = w_qb_ref[...].T.astype(BF16)
        for h in range(N_HEADS):
            base = h * (QK_NOPE + V_HEAD)
            w_kb_ref[:, h * QK_NOPE:(h + 1) * QK_NOPE] = w_kvb_ref[:, base:base + QK_NOPE].astype(BF16)
            w_vbt_ref[h * V_HEAD:(h + 1) * V_HEAD, :] = w_kvb_ref[:, base + QK_NOPE:base + QK_NOPE + V_HEAD].T.astype(BF16)
        pool_w_ref[...] = pool_w_f32_ref[...].astype(BF16)

    @pl.when(first_step)
    def _meta_prologue():
        xn = _rms(meta_ref[...], norm_g_ref[...]).astype(BF16)
        halo_scr[...] = _dot(xn, w_pi_ref[...])
        k_nope, v_t, k_rope = _latent_kv(_dot(xn, w_ckv_ref[...]), kv_g_ref, w_kb_ref, w_vbt_ref,
                                         mcos_ref[...], msin_ref[...])
        kmeta_scr[...] = jnp.zeros_like(kmeta_scr)
        vtmeta_scr[...] = jnp.zeros_like(vtmeta_scr)
        for h in range(N_HEADS):
            kmeta_scr[h, 0:N_META, 0:LANES] = k_nope[:, h * QK_NOPE:(h + 1) * QK_NOPE].astype(BF16)
            kmeta_scr[h, 0:N_META, LANES:QK_PAD] = k_rope.astype(BF16)
            vtmeta_scr[h, :, 0:N_META] = v_t[h * V_HEAD:(h + 1) * V_HEAD, :].astype(BF16)

    @pl.when(i == 0)
    def _seed_halo():
        pool_scr[0:HALO, :] = halo_scr[...]

    x = x_ref[...]
    xn = _rms(x, norm_g_ref[...]).astype(BF16)

    pool_scr[HALO:HALO + tm, :] = _dot(xn, w_pi_ref[...])
    cq = _dot(xn, w_cq_ref[...])
    ckvr = _dot(xn, w_ckv_ref[...])
    pool_gate = _dot(xn, w_pg_ref[...])

    cq_n = _rms(cq, q_g_ref[...]).astype(BF16)
    q_t = _dot_nt(w_qbt_ref[...], cq_n) * Q_SCALE
    k_nope, v_t, k_rope = _latent_kv(ckvr, kv_g_ref, w_kb_ref, w_vbt_ref, kcos_ref[...], ksin_ref[...])
    attn_gate = _dot(xn, w_ag_ref[...])

    pooled = []
    for g, w in enumerate(POOL_WINDOWS):
        xg = pool_scr[:, g * POOL_GROUP:(g + 1) * POOL_GROUP]
        win = _window_sum(xg, w)
        pooled.append((win[HALO:, :] * (1.0 / w) - xg[HALO:, :]).astype(BF16))
    pool_scr[0:HALO, :] = pool_scr[tm:tm + HALO, :]

    row0 = pl.multiple_of(i * tm, tm)
    tile_rows = pl.ds(row0, tm)
    k_rope_bf = k_rope.astype(BF16)
    for h in range(N_HEADS):
        k_scr[h, tile_rows, 0:LANES] = k_nope[:, h * QK_NOPE:(h + 1) * QK_NOPE].astype(BF16)
        k_scr[h, tile_rows, LANES:QK_PAD] = k_rope_bf
        vt_scr[h, :, tile_rows] = v_t[h * V_HEAD:(h + 1) * V_HEAD, :].astype(BF16)

    cos_q = qcos_ref[...]
    sin_q = qsin_ref[...]
    half = QK_ROPE // 2
    for h in range(N_HEADS):
        base = h * (QK_NOPE + QK_ROPE)
        x1 = q_t[base + QK_NOPE:base + QK_NOPE + half, :]
        x2 = q_t[base + QK_NOPE + half:base + QK_NOPE + QK_ROPE, :]
        qt_scr[h, 0:QK_NOPE, :] = q_t[base:base + QK_NOPE, :].astype(BF16)
        qt_scr[h, QK_NOPE:QK_NOPE + half, :] = (x1 * cos_q - x2 * sin_q).astype(BF16)
        qt_scr[h, QK_NOPE + half:QK_NOPE + QK_ROPE, :] = (x1 * sin_q + x2 * cos_q).astype(BF16)
        qt_scr[h, QK_NOPE + QK_ROPE:QK_PAD, :] = jnp.zeros((QK_PAD - QK_NOPE - QK_ROPE, tm), BF16)

    mixed = [_dot(pooled[g], pool_w_ref[g]) for g in range(len(POOL_WINDOWS))]
    pool_out = (_silu(pool_gate) * (jnp.concatenate(mixed, axis=1) * pool_scale_ref[...])).astype(BF16)
    gate = _silu(attn_gate)

    def _scores(h, rows):
        return _dot(k_scr[h, rows, :], qt_scr[h])

    def _update(h, blocks):
        m_prev = m_scr[h]
        m_new = m_prev
        for s, _ in blocks:
            m_new = jnp.maximum(m_new, jnp.max(s, axis=0, keepdims=True))
        alpha = jnp.exp2(m_prev - m_new)
        l_new = alpha * l_scr[h]
        acc_new = alpha * acc_scr[h]
        for s, vt_blk in blocks:
            p = jnp.exp2(s - m_new)
            l_new = l_new + jnp.sum(p, axis=0, keepdims=True)
            acc_new = acc_new + _dot(vt_blk, p.astype(BF16))
        m_scr[h] = m_new
        l_scr[h] = l_new
        acc_scr[h] = acc_new

    for h in range(N_HEADS):
        m_scr[h] = jnp.full((1, tm), MASK_VALUE, F32)
        l_scr[h] = jnp.zeros((1, tm), F32)
        acc_scr[h] = jnp.zeros((V_HEAD, tm), F32)
    s_scr[...] = _scores(0, pl.ds(0, tm))

    def _earlier_tile(j, carry):
        rows = pl.ds(pl.multiple_of(j * tm, tm), tm)
        next_rows = pl.ds(pl.multiple_of((j + 1) * tm, tm), tm)
        s_next = s_scr[...]
        for h in range(N_HEADS):
            s = s_next
            if h + 1 < N_HEADS:
                s_next = _scores(h + 1, rows)
            else:
                s_scr[...] = _scores(0, next_rows)
            _update(h, [(s, vt_scr[h, :, rows])])
        return carry

    lax.fori_loop(0, i, _earlier_tile, 0)

    key_chunk = lax.broadcasted_iota(jnp.int32, (tm, tm), 0) // CHUNK
    query_chunk = lax.broadcasted_iota(jnp.int32, (tm, tm), 1) // CHUNK
    diag_valid = key_chunk <= query_chunk
    meta_valid = lax.broadcasted_iota(jnp.int32, (META_PAD, tm), 0) < N_META
    s_next = jnp.where(diag_valid, s_scr[...], MASK_VALUE)
    for h in range(N_HEADS):
        s = s_next
        s_meta = jnp.where(meta_valid, _dot(kmeta_scr[h], qt_scr[h]), MASK_VALUE)
        if h + 1 < N_HEADS:
            s_next = jnp.where(diag_valid, _scores(h + 1, tile_rows), MASK_VALUE)
        else:
            mix_pool = _dot(pool_out, w_out_ref[0:D_POOL, :])
        _update(h, [(s, vt_scr[h, :, tile_rows]), (s_meta, vtmeta_scr[h])])

    attn = [(acc_scr[h] * (1.0 / l_scr[h])).T for h in range(N_HEADS)]
    attn_out = (gate * jnp.concatenate(attn, axis=1)).astype(BF16)

    mix = mix_pool + _dot(attn_out, w_out_ref[D_POOL:, :])
    out_ref[...] = _rms(x + mix, final_g_ref[...])


def _rope_tables(length):
    half = QK_ROPE // 2
    inv_freq = 1.0 / (ROPE_THETA ** (jnp.arange(half, dtype=F32) / half))
    ang = jnp.arange(length, dtype=F32)[:, None] * inv_freq[None, :]
    return jnp.cos(ang), jnp.sin(ang)


def kernel(x, meta_tokens, norm_g, w_in, q_norm_g, w_q_b, kv_norm_g, w_kv_b, pool_w, pool_scale, w_out, final_norm_g):
    batch, seq, d_model = x.shape
    assert d_model == D_MODEL and seq % ROW_TILE == 0 and meta_tokens.shape == (N_META, D_MODEL)
    assert norm_g.shape[0] == 1 and w_in.shape == (1, D_MODEL, D_IN), "single-layer block"
    tm = ROW_TILE

    cos, sin = _rope_tables(N_META + seq)
    kcos = jnp.tile(cos, (1, LANES // (QK_ROPE // 2)))
    ksin = jnp.tile(jnp.concatenate([-sin, sin], axis=1), (1, LANES // QK_ROPE))
    qcos = cos[N_META:].T
    qsin = sin[N_META:].T

    def whole(a):
        return pl.BlockSpec(a.shape, lambda b, i: (0,) * a.ndim)

    def layer0(a):
        return pl.BlockSpec((None,) + a.shape[1:], lambda b, i: (0,) * a.ndim,
                            pipeline_mode=pl.Buffered(1))

    row_spec = pl.BlockSpec((None, tm, D_MODEL), lambda b, i: (b, i, 0))
    ktable_spec = pl.BlockSpec((tm, LANES), lambda b, i: (i, 0))
    qtable_spec = pl.BlockSpec((QK_ROPE // 2, tm), lambda b, i: (0, i))

    operands = [
        (x, row_spec),
        (meta_tokens, None),
        (kcos[N_META:], ktable_spec),
        (ksin[N_META:], ktable_spec),
        (kcos[:N_META], None),
        (ksin[:N_META], None),
        (qcos, qtable_spec),
        (qsin, qtable_spec),
        (norm_g, None),
        (w_in, layer0(w_in)),
        (q_norm_g, None), (w_q_b, layer0(w_q_b)),
        (kv_norm_g, None), (w_kv_b, layer0(w_kv_b)),
        (pool_w, layer0(pool_w)),
        (pool_scale, None),
        (w_out, layer0(w_out)),
        (final_norm_g.reshape(1, D_MODEL), None),
    ]
    args = [a for a, _ in operands]
    in_specs = [whole(a) if s is None else s for a, s in operands]

    return pl.pallas_call(
        _fused_kernel,
        out_shape=jax.ShapeDtypeStruct(x.shape, x.dtype),
        grid=(batch, seq // tm),
        in_specs=in_specs,
        out_specs=row_spec,
        scratch_shapes=[
            pltpu.VMEM((D_MODEL, D_IN_PADDED), BF16),
            pltpu.VMEM((N_HEADS * (QK_NOPE + QK_ROPE), Q_LORA), BF16),
            pltpu.VMEM((KV_LORA, N_HEADS * QK_NOPE), BF16),
            pltpu.VMEM((N_HEADS * V_HEAD, KV_LORA), BF16),
            pltpu.VMEM((len(POOL_WINDOWS), POOL_GROUP, POOL_GROUP), BF16),
            pltpu.VMEM((D_MODEL, D_MODEL), BF16),
            pltpu.VMEM((N_HEADS, seq, QK_PAD), BF16),
            pltpu.VMEM((N_HEADS, V_HEAD, seq), BF16),
            pltpu.VMEM((N_HEADS, META_PAD, QK_PAD), BF16),
            pltpu.VMEM((N_HEADS, V_HEAD, META_PAD), BF16),
            pltpu.VMEM((HALO + tm, D_POOL), F32),
            pltpu.VMEM((HALO, D_POOL), F32),
            pltpu.VMEM((N_HEADS, QK_PAD, tm), BF16),
            pltpu.VMEM((tm, tm), F32),
            pltpu.VMEM((N_HEADS, 1, tm), F32),
            pltpu.VMEM((N_HEADS, 1, tm), F32),
            pltpu.VMEM((N_HEADS, V_HEAD, tm), F32),
        ],
        compiler_params=pltpu.CompilerParams(
            dimension_semantics=("arbitrary", "arbitrary"),
            vmem_limit_bytes=VMEM_LIMIT_BYTES,
        ),
        name="hybrid_pool_mla_block",
    )(*args)
```

```python
import math

import numpy as np
import jax
import jax.numpy as jnp
from jax import lax
from jax.experimental import pallas as pl
from jax.experimental.pallas import tpu as pltpu

D_MODEL = 1024
N_META = 16
CHUNK = 64
D_POOL = 512
POOL_WINDOWS = (2, 4, 8, 16)
POOL_GROUP = 128
N_HEADS = 4
QK_NOPE = 128
QK_ROPE = 64
V_HEAD = 128
D_ATTN = N_HEADS * V_HEAD
Q_LORA = 256
KV_LORA = 128
ROPE_THETA = 10000.0
EPS = 1e-6
Q_SCALE = (QK_NOPE + QK_ROPE) ** -0.5 * math.log2(math.e)

O_CQ = 2 * D_POOL
O_CKV = O_CQ + Q_LORA
O_KR = O_CKV + KV_LORA
O_AG = O_KR + QK_ROPE
D_IN = O_AG + D_ATTN
S_AG = O_AG + QK_ROPE
D_IN_PADDED = S_AG + D_ATTN

LANES = 128
QK_PAD = 2 * LANES
META_PAD = LANES
HALO = 16
ROW_TILE = 512
MASK_VALUE = -1e30
VMEM_LIMIT_BYTES = 56 * 1024 * 1024

F32 = jnp.float32
BF16 = jnp.bfloat16


def _dot(a, b):
    return jnp.dot(a, b, preferred_element_type=F32)


def _dot_nt(a, b):
    return lax.dot_general(a, b, (((1,), (1,)), ((), ())), preferred_element_type=F32)


def _rms(x, g):
    return x * lax.rsqrt(jnp.mean(x * x, axis=-1, keepdims=True) + EPS) * g


def _silu(x):
    h = 0.5 * x
    return h + h * jnp.tanh(h)


def _rope_lanes(x, cos_t, sin_t):
    lane = lax.broadcasted_iota(jnp.int32, x.shape, 1)
    first_half = (lane % QK_ROPE) < (QK_ROPE // 2)
    partner = jnp.where(first_half,
                        pltpu.roll(x, LANES - QK_ROPE // 2, axis=1),
                        pltpu.roll(x, QK_ROPE // 2, axis=1))
    return x * cos_t + partner * sin_t


def _latent_kv(ckvr, kv_g_ref, w_kb_ref, w_vbt_ref, cos_t, sin_t):
    ckv_n = _rms(ckvr[:, :KV_LORA], kv_g_ref[...]).astype(BF16)
    k_nope = _dot(ckv_n, w_kb_ref[...])
    v_t = _dot_nt(w_vbt_ref[...], ckv_n)
    k_rope = _rope_lanes(ckvr[:, KV_LORA:], cos_t, sin_t)
    return k_nope, v_t, k_rope


def _window_sum(xg, w):
    y = xg
    span = 1
    while span < w:
        y = y + pltpu.roll(y, span, axis=0)
        span *= 2
    return y


def _fused_kernel(x_ref, meta_ref, kcos_ref, ksin_ref, mcos_ref, msin_ref, qcos_ref, qsin_ref,
                  norm_g_ref, w_in_ref, q_g_ref, w_qb_ref, kv_g_ref, w_kvb_ref,
                  pool_w_f32_ref, pool_scale_ref, w_out_f32_ref, final_g_ref,
                  out_ref,
                  w_in_s, w_qbt_ref, w_kb_ref, w_vbt_ref, pool_w_ref, w_out_ref,
                  k_scr, vt_scr, kmeta_scr, vtmeta_scr, pool_scr, halo_scr,
                  qt_scr, s_scr, m_scr, l_scr, acc_scr):
    b = pl.program_id(0)
    i = pl.program_id(1)
    tm = ROW_TILE
    first_step = (b == 0) & (i == 0)

    w_pi_ref = w_in_s.at[:, 0:D_POOL]
    w_pg_ref = w_in_s.at[:, D_POOL:O_CQ]
    w_cq_ref = w_in_s.at[:, O_CQ:O_CKV]
    w_ckv_ref = w_in_s.at[:, O_CKV:S_AG]
    w_ag_ref = w_in_s.at[:, S_AG:S_AG + D_ATTN]

    @pl.when(first_step)
    def _prepare_weights():
        for c in range(O_KR // LANES):
            w_in_s[:, c * LANES:(c + 1) * LANES] = w_in_ref[c * LANES:(c + 1) * LANES, :].T.astype(BF16)
        k_rope_rows = w_in_ref[O_KR:O_AG, :]
        w_in_s[:, O_KR:S_AG] = jnp.concatenate([k_rope_rows, k_rope_rows], axis=0).T.astype(BF16)
        for c in range(D_ATTN // LANES):
            w_in_s[:, S_AG + c * LANES:S_AG + (c + 1) * LANES] = (
                w_in_ref[O_AG + c * LANES:O_AG + (c + 1) * LANES, :].T.astype(BF16))
        quarter = D_MODEL // 4
        for r in range(4):
            rs = slice(r * quarter, (r + 1) * quarter)
            w_out_ref[rs, :] = w_out_f32_ref[rs, :].astype(BF16)
        w_qbt_ref[...] = w_qb_ref[...].T.astype(BF16)
        for h in range(N_HEADS):
            base = h * (QK_NOPE + V_HEAD)
            w_kb_ref[:, h * QK_NOPE:(h + 1) * QK_NOPE] = w_kvb_ref[:, base:base + QK_NOPE].astype(BF16)
            w_vbt_ref[h * V_HEAD:(h + 1) * V_HEAD, :] = w_kvb_ref[:, base + QK_NOPE:base + QK_NOPE + V_HEAD].T.astype(BF16)
        pool_w_ref[...] = pool_w_f32_ref[...].astype(BF16)

    @pl.when(first_step)
    def _meta_prologue():
        xn = _rms(meta_ref[...], norm_g_ref[...]).astype(BF16)
        halo_scr[...] = _dot(xn, w_pi_ref[...])
        k_nope, v_t, k_rope = _latent_kv(_dot(xn, w_ckv_ref[...]), kv_g_ref, w_kb_ref, w_vbt_ref,
                                         mcos_ref[...], msin_ref[...])
        kmeta_scr[...] = jnp.zeros_like(kmeta_scr)
        vtmeta_scr[...] = jnp.zeros_like(vtmeta_scr)
        for h in range(N_HEADS):
            kmeta_scr[h, 0:N_META, 0:LANES] = k_nope[:, h * QK_NOPE:(h + 1) * QK_NOPE].astype(BF16)
            kmeta_scr[h, 0:N_META, LANES:QK_PAD] = k_rope.astype(BF16)
            vtmeta_scr[h, :, 0:N_META] = v_t[h * V_HEAD:(h + 1) * V_HEAD, :].astype(BF16)

    @pl.when(i == 0)
    def _seed_halo():
        pool_scr[0:HALO, :] = halo_scr[...]

    x = x_ref[...]
    xn = _rms(x, norm_g_ref[...]).astype(BF16)

    pool_scr[HALO:HALO + tm, :] = _dot(xn, w_pi_ref[...])
    cq = _dot(xn, w_cq_ref[...])
    ckvr = _dot(xn, w_ckv_ref[...])
    pool_gate = _dot(xn, w_pg_ref[...])

    cq_n = _rms(cq, q_g_ref[...]).astype(BF16)
    q_t = _dot_nt(w_qbt_ref[...], cq_n) * Q_SCALE
    k_nope, v_t, k_rope = _latent_kv(ckvr, kv_g_ref, w_kb_ref, w_vbt_ref, kcos_ref[...], ksin_ref[...])
    attn_gate = _dot(xn, w_ag_ref[...])

    pooled = []
    for g, w in enumerate(POOL_WINDOWS):
        xg = pool_scr[:, g * POOL_GROUP:(g + 1) * POOL_GROUP]
        win = _window_sum(xg, w)
        pooled.append((win[HALO:, :] * (1.0 / w) - xg[HALO:, :]).astype(BF16))
    pool_scr[0:HALO, :] = pool_scr[tm:tm + HALO, :]

    row0 = pl.multiple_of(i * tm, tm)
    tile_rows = pl.ds(row0, tm)
    k_rope_bf = k_rope.astype(BF16)
    for h in range(N_HEADS):
        k_scr[h, tile_rows, 0:LANES] = k_nope[:, h * QK_NOPE:(h + 1) * QK_NOPE].astype(BF16)
        k_scr[h, tile_rows, LANES:QK_PAD] = k_rope_bf
        vt_scr[h, :, tile_rows] = v_t[h * V_HEAD:(h + 1) * V_HEAD, :].astype(BF16)

    cos_q = qcos_ref[...]
    sin_q = qsin_ref[...]
    half = QK_ROPE // 2
    for h in range(N_HEADS):
        base = h * (QK_NOPE + QK_ROPE)
        x1 = q_t[base + QK_NOPE:base + QK_NOPE + half, :]
        x2 = q_t[base + QK_NOPE + half:base + QK_NOPE + QK_ROPE, :]
        qt_scr[h, 0:QK_NOPE, :] = q_t[base:base + QK_NOPE, :].astype(BF16)
        qt_scr[h, QK_NOPE:QK_NOPE + half, :] = (x1 * cos_q - x2 * sin_q).astype(BF16)
        qt_scr[h, QK_NOPE + half:QK_NOPE + QK_ROPE, :] = (x1 * sin_q + x2 * cos_q).astype(BF16)
        qt_scr[h, QK_NOPE + QK_ROPE:QK_PAD, :] = jnp.zeros((QK_PAD - QK_NOPE - QK_ROPE, tm), BF16)

    mixed = [_dot(pooled[g], pool_w_ref[g]) for g in range(len(POOL_WINDOWS))]
    pool_out = (_silu(pool_gate) * (jnp.concatenate(mixed, axis=1) * pool_scale_ref[...])).astype(BF16)
    gate = _silu(attn_gate)

    def _scores(h, rows):
        return _dot(k_scr[h, rows, :], qt_scr[h])

    def _update(h, blocks):
        m_prev = m_scr[h]
        m_new = m_prev
        for s, _ in blocks:
            m_new = jnp.maximum(m_new, jnp.max(s, axis=0, keepdims=True))
        alpha = jnp.exp2(m_prev - m_new)
        l_new = alpha * l_scr[h]
        acc_new = alpha * acc_scr[h]
        for s, vt_blk in blocks:
            p = jnp.exp2(s - m_new)
            l_new = l_new + jnp.sum(p, axis=0, keepdims=True)
            acc_new = acc_new + _dot(vt_blk, p.astype(BF16))
        m_scr[h] = m_new
        l_scr[h] = l_new
        acc_scr[h] = acc_new

    for h in range(N_HEADS):
        m_scr[h] = jnp.full((1, tm), MASK_VALUE, F32)
        l_scr[h] = jnp.zeros((1, tm), F32)
        acc_scr[h] = jnp.zeros((V_HEAD, tm), F32)
    s_scr[...] = _scores(0, pl.ds(0, tm))

    def _earlier_tile(j, carry):
        rows = pl.ds(pl.multiple_of(j * tm, tm), tm)
        next_rows = pl.ds(pl.multiple_of((j + 1) * tm, tm), tm)
        s_next = s_scr[...]
        for h in range(N_HEADS):
            s = s_next
            if h + 1 < N_HEADS:
                s_next = _scores(h + 1, rows)
            else:
                s_scr[...] = _scores(0, next_rows)
            _update(h, [(s, vt_scr[h, :, rows])])
        return carry

    lax.fori_loop(0, i, _earlier_tile, 0)

    key_chunk = lax.broadcasted_iota(jnp.int32, (tm, tm), 0) // CHUNK
    query_chunk = lax.broadcasted_iota(jnp.int32, (tm, tm), 1) // CHUNK
    diag_valid = key_chunk <= query_chunk
    meta_valid = lax.broadcasted_iota(jnp.int32, (META_PAD, tm), 0) < N_META
    s_next = jnp.where(diag_valid, s_scr[...], MASK_VALUE)
    for h in range(N_HEADS):
        s = s_next
        s_meta = jnp.where(meta_valid, _dot(kmeta_scr[h], qt_scr[h]), MASK_VALUE)
        if h + 1 < N_HEADS:
            s_next = jnp.where(diag_valid, _scores(h + 1, tile_rows), MASK_VALUE)
        else:
            mix_pool = _dot(pool_out, w_out_ref[0:D_POOL, :])
        _update(h, [(s, vt_scr[h, :, tile_rows]), (s_meta, vtmeta_scr[h])])

    attn = [(acc_scr[h] * (1.0 / l_scr[h])).T for h in range(N_HEADS)]
    attn_out = (gate * jnp.concatenate(attn, axis=1)).astype(BF16)

    mix = mix_pool + _dot(attn_out, w_out_ref[D_POOL:, :])
    out_ref[...] = _rms(x + mix, final_g_ref[...])


def _rope_tables(length):
    half = QK_ROPE // 2
    inv_freq = 1.0 / (ROPE_THETA ** (np.arange(half, dtype=np.float64) / half))
    ang = np.arange(length, dtype=np.float64)[:, None] * inv_freq[None, :]
    return np.cos(ang), np.sin(ang)


def kernel(x, meta_tokens, norm_g, w_in, q_norm_g, w_q_b, kv_norm_g, w_kv_b, pool_w, pool_scale, w_out, final_norm_g):
    batch, seq, d_model = x.shape
    assert d_model == D_MODEL and seq % ROW_TILE == 0 and meta_tokens.shape == (N_META, D_MODEL)
    assert norm_g.shape[0] == 1 and w_in.shape == (1, D_MODEL, D_IN), "single-layer block"
    tm = ROW_TILE

    cos, sin = _rope_tables(N_META + seq)
    kcos = np.tile(cos, (1, LANES // (QK_ROPE // 2))).astype(np.float32)
    ksin = np.tile(np.concatenate([-sin, sin], axis=1), (1, LANES // QK_ROPE)).astype(np.float32)
    qcos = np.ascontiguousarray(cos[N_META:].T).astype(np.float32)
    qsin = np.ascontiguousarray(sin[N_META:].T).astype(np.float32)

    def whole(a):
        return pl.BlockSpec(a.shape, lambda b, i: (0,) * a.ndim)

    def layer0(a):
        return pl.BlockSpec((None,) + a.shape[1:], lambda b, i: (0,) * a.ndim,
                            pipeline_mode=pl.Buffered(1))

    row_spec = pl.BlockSpec((None, tm, D_MODEL), lambda b, i: (b, i, 0))
    ktable_spec = pl.BlockSpec((tm, LANES), lambda b, i: (i, 0))
    qtable_spec = pl.BlockSpec((QK_ROPE // 2, tm), lambda b, i: (0, i))

    w_in_t = jnp.transpose(w_in[0])

    operands = [
        (x, row_spec),
        (meta_tokens, None),
        (kcos[N_META:], ktable_spec),
        (ksin[N_META:], ktable_spec),
        (kcos[:N_META], None),
        (ksin[:N_META], None),
        (qcos, qtable_spec),
        (qsin, qtable_spec),
        (norm_g, None),
        (w_in_t, pl.BlockSpec(w_in_t.shape, lambda b, i: (0, 0), pipeline_mode=pl.Buffered(1))),
        (q_norm_g, None), (w_q_b, layer0(w_q_b)),
        (kv_norm_g, None), (w_kv_b, layer0(w_kv_b)),
        (pool_w, layer0(pool_w)),
        (pool_scale, None),
        (w_out, layer0(w_out)),
        (final_norm_g.reshape(1, D_MODEL), None),
    ]
    args = [a for a, _ in operands]
    in_specs = [whole(a) if s is None else s for a, s in operands]

    return pl.pallas_call(
        _fused_kernel,
        out_shape=jax.ShapeDtypeStruct(x.shape, x.dtype),
        grid=(batch, seq // tm),
        in_specs=in_specs,
        out_specs=row_spec,
        scratch_shapes=[
            pltpu.VMEM((D_MODEL, D_IN_PADDED), BF16),
            pltpu.VMEM((N_HEADS * (QK_NOPE + QK_ROPE), Q_LORA), BF16),
            pltpu.VMEM((KV_LORA, N_HEADS * QK_NOPE), BF16),
            pltpu.VMEM((N_HEADS * V_HEAD, KV_LORA), BF16),
            pltpu.VMEM((len(POOL_WINDOWS), POOL_GROUP, POOL_GROUP), BF16),
            pltpu.VMEM((D_MODEL, D_MODEL), BF16),
            pltpu.VMEM((N_HEADS, seq, QK_PAD), BF16),
            pltpu.VMEM((N_HEADS, V_HEAD, seq), BF16),
            pltpu.VMEM((N_HEADS, META_PAD, QK_PAD), BF16),
            pltpu.VMEM((N_HEADS, V_HEAD, META_PAD), BF16),
            pltpu.VMEM((HALO + tm, D_POOL), F32),
            pltpu.VMEM((HALO, D_POOL), F32),
            pltpu.VMEM((N_HEADS, QK_PAD, tm), BF16),
            pltpu.VMEM((tm, tm), F32),
            pltpu.VMEM((N_HEADS, 1, tm), F32),
            pltpu.VMEM((N_HEADS, 1, tm), F32),
            pltpu.VMEM((N_HEADS, V_HEAD, tm), F32),
        ],
        compiler_params=pltpu.CompilerParams(
            dimension_semantics=("arbitrary", "arbitrary"),
            vmem_limit_bytes=VMEM_LIMIT_BYTES,
        ),
        name="hybrid_pool_mla_block",
    )(*args)
```

```python
import functools
import math

import numpy as np
import jax
import jax.numpy as jnp
from jax import lax
from jax.experimental import pallas as pl
from jax.experimental.pallas import tpu as pltpu

D_MODEL = 1024
N_META = 16
CHUNK = 64
D_POOL = 512
POOL_WINDOWS = (2, 4, 8, 16)
POOL_GROUP = 128
N_HEADS = 4
QK_NOPE = 128
QK_ROPE = 64
V_HEAD = 128
D_ATTN = N_HEADS * V_HEAD
Q_LORA = 256
KV_LORA = 128
ROPE_THETA = 10000.0
EPS = 1e-6
Q_SCALE = (QK_NOPE + QK_ROPE) ** -0.5 * math.log2(math.e)

O_CQ = 2 * D_POOL
O_CKV = O_CQ + Q_LORA
O_KR = O_CKV + KV_LORA
O_AG = O_KR + QK_ROPE
D_IN = O_AG + D_ATTN
S_AG = O_AG + QK_ROPE
D_IN_PADDED = S_AG + D_ATTN

LANES = 128
QK_PAD = 2 * LANES
META_PAD = LANES
HALO = 16
ROW_TILE = 512
MASK_VALUE = -1e30
VMEM_LIMIT_BYTES = 60 * 1024 * 1024

F32 = jnp.float32
BF16 = jnp.bfloat16


def _dot(a, b):
    return jnp.dot(a, b, preferred_element_type=F32)


def _dot_nt(a, b):
    return lax.dot_general(a, b, (((1,), (1,)), ((), ())), preferred_element_type=F32)


def _rms(x, g):
    return x * lax.rsqrt(jnp.mean(x * x, axis=-1, keepdims=True) + EPS) * g


def _silu(x):
    h = 0.5 * x
    return h + h * jnp.tanh(h)


def _rope_lanes(x, cos_t, sin_t):
    lane = lax.broadcasted_iota(jnp.int32, x.shape, 1)
    first_half = (lane % QK_ROPE) < (QK_ROPE // 2)
    partner = jnp.where(first_half,
                        pltpu.roll(x, LANES - QK_ROPE // 2, axis=1),
                        pltpu.roll(x, QK_ROPE // 2, axis=1))
    return x * cos_t + partner * sin_t


def _latent_kv(ckvr, kv_g_ref, w_kb_ref, w_vbt_ref, cos_t, sin_t):
    ckv_n = _rms(ckvr[:, :KV_LORA], kv_g_ref[...]).astype(BF16)
    k_nope = _dot(ckv_n, w_kb_ref[...])
    v_t = _dot_nt(w_vbt_ref[...], ckv_n)
    k_rope = _rope_lanes(ckvr[:, KV_LORA:], cos_t, sin_t)
    return k_nope, v_t, k_rope


def _window_sum(xg, w):
    y = xg
    span = 1
    while span < w:
        y = y + pltpu.roll(y, span, axis=0)
        span *= 2
    return y


def _fused_kernel(tiles_per_seq,
                  x_ref, x_next_ref, meta_ref, kcos_ref, ksin_ref, mcos_ref, msin_ref, qcos_ref, qsin_ref,
                  norm_g_ref, w_in_ref, q_g_ref, w_qb_ref, kv_g_ref, w_kvb_ref,
                  pool_w_f32_ref, pool_scale_ref, w_out_f32_ref, final_g_ref,
                  out_ref,
                  w_in_s, w_qbt_ref, w_kb_ref, w_vbt_ref, pool_w_ref, w_out_ref,
                  k_scr, vt_scr, kmeta_scr, vtmeta_scr, pool_scr, halo_scr,
                  qt_scr, s_scr, m_scr, l_scr, acc_scr, xn_scr, y_scr):
    step = pl.program_id(0)
    last_step = pl.num_programs(0) - 1
    i = lax.rem(step, tiles_per_seq)
    tm = ROW_TILE

    w_pi_ref = w_in_s.at[:, 0:D_POOL]
    w_pg_ref = w_in_s.at[:, D_POOL:O_CQ]
    w_cq_ref = w_in_s.at[:, O_CQ:O_CKV]
    w_ckv_ref = w_in_s.at[:, O_CKV:S_AG]
    w_ag_ref = w_in_s.at[:, S_AG:S_AG + D_ATTN]

    @pl.when(step == 0)
    def _prepare_weights():
        for c in range(O_KR // LANES):
            w_in_s[:, c * LANES:(c + 1) * LANES] = w_in_ref[c * LANES:(c + 1) * LANES, :].T.astype(BF16)
        k_rope_rows = w_in_ref[O_KR:O_AG, :]
        w_in_s[:, O_KR:S_AG] = jnp.concatenate([k_rope_rows, k_rope_rows], axis=0).T.astype(BF16)
        for c in range(D_ATTN // LANES):
            w_in_s[:, S_AG + c * LANES:S_AG + (c + 1) * LANES] = (
                w_in_ref[O_AG + c * LANES:O_AG + (c + 1) * LANES, :].T.astype(BF16))
        quarter = D_MODEL // 4
        for r in range(4):
            rs = slice(r * quarter, (r + 1) * quarter)
            w_out_ref[rs, :] = w_out_f32_ref[rs, :].astype(BF16)
        w_qbt_ref[...] = w_qb_ref[...].T.astype(BF16)
        for h in range(N_HEADS):
            base = h * (QK_NOPE + V_HEAD)
            w_kb_ref[:, h * QK_NOPE:(h + 1) * QK_NOPE] = w_kvb_ref[:, base:base + QK_NOPE].astype(BF16)
            w_vbt_ref[h * V_HEAD:(h + 1) * V_HEAD, :] = w_kvb_ref[:, base + QK_NOPE:base + QK_NOPE + V_HEAD].T.astype(BF16)
        pool_w_ref[...] = pool_w_f32_ref[...].astype(BF16)

    @pl.when(step == 0)
    def _first_step_state():
        xn = _rms(meta_ref[...], norm_g_ref[...]).astype(BF16)
        halo_scr[...] = _dot(xn, w_pi_ref[...])
        k_nope, v_t, k_rope = _latent_kv(_dot(xn, w_ckv_ref[...]), kv_g_ref, w_kb_ref, w_vbt_ref,
                                         mcos_ref[...], msin_ref[...])
        kmeta_scr[...] = jnp.zeros_like(kmeta_scr)
        vtmeta_scr[...] = jnp.zeros_like(vtmeta_scr)
        for h in range(N_HEADS):
            kmeta_scr[h, 0:N_META, 0:LANES] = k_nope[:, h * QK_NOPE:(h + 1) * QK_NOPE].astype(BF16)
            kmeta_scr[h, 0:N_META, LANES:QK_PAD] = k_rope.astype(BF16)
            vtmeta_scr[h, :, 0:N_META] = v_t[h * V_HEAD:(h + 1) * V_HEAD, :].astype(BF16)
        xn_scr[...] = _rms(x_ref[...], norm_g_ref[...]).astype(BF16)
        y_scr[...] = jnp.zeros_like(y_scr)

    @pl.when(step == last_step)
    def _drain():
        out_ref[...] = _rms(y_scr[...], final_g_ref[...])

    @pl.when(step < last_step)
    def _tile():
        @pl.when(i == 0)
        def _seed_halo():
            pool_scr[0:HALO, :] = halo_scr[...]

        xn = xn_scr[...]

        pool_scr[HALO:HALO + tm, :] = _dot(xn, w_pi_ref[...])
        cq = _dot(xn, w_cq_ref[...])
        ckvr = _dot(xn, w_ckv_ref[...])
        pool_gate = _dot(xn, w_pg_ref[...])

        cq_n = _rms(cq, q_g_ref[...]).astype(BF16)
        q_t = _dot_nt(w_qbt_ref[...], cq_n) * Q_SCALE
        k_nope, v_t, k_rope = _latent_kv(ckvr, kv_g_ref, w_kb_ref, w_vbt_ref, kcos_ref[...], ksin_ref[...])
        attn_gate = _dot(xn, w_ag_ref[...])

        out_ref[...] = _rms(y_scr[...], final_g_ref[...])

        pooled = []
        for g, w in enumerate(POOL_WINDOWS):
            xg = pool_scr[:, g * POOL_GROUP:(g + 1) * POOL_GROUP]
            win = _window_sum(xg, w)
            pooled.append((win[HALO:, :] * (1.0 / w) - xg[HALO:, :]).astype(BF16))
        pool_scr[0:HALO, :] = pool_scr[tm:tm + HALO, :]

        row0 = pl.multiple_of(i * tm, tm)
        tile_rows = pl.ds(row0, tm)
        k_rope_bf = k_rope.astype(BF16)
        for h in range(N_HEADS):
            k_scr[h, tile_rows, 0:LANES] = k_nope[:, h * QK_NOPE:(h + 1) * QK_NOPE].astype(BF16)
            k_scr[h, tile_rows, LANES:QK_PAD] = k_rope_bf
            vt_scr[h, :, tile_rows] = v_t[h * V_HEAD:(h + 1) * V_HEAD, :].astype(BF16)

        cos_q = qcos_ref[...]
        sin_q = qsin_ref[...]
        half = QK_ROPE // 2
        for h in range(N_HEADS):
            base = h * (QK_NOPE + QK_ROPE)
            x1 = q_t[base + QK_NOPE:base + QK_NOPE + half, :]
            x2 = q_t[base + QK_NOPE + half:base + QK_NOPE + QK_ROPE, :]
            qt_scr[h, 0:QK_NOPE, :] = q_t[base:base + QK_NOPE, :].astype(BF16)
            qt_scr[h, QK_NOPE:QK_NOPE + half, :] = (x1 * cos_q - x2 * sin_q).astype(BF16)
            qt_scr[h, QK_NOPE + half:QK_NOPE + QK_ROPE, :] = (x1 * sin_q + x2 * cos_q).astype(BF16)
            qt_scr[h, QK_NOPE + QK_ROPE:QK_PAD, :] = jnp.zeros((QK_PAD - QK_NOPE - QK_ROPE, tm), BF16)

        mixed = [_dot(pooled[g], pool_w_ref[g]) for g in range(len(POOL_WINDOWS))]
        pool_out = (_silu(pool_gate) * (jnp.concatenate(mixed, axis=1) * pool_scale_ref[...])).astype(BF16)
        gate = _silu(attn_gate)

        def _scores(h, rows):
            return _dot(k_scr[h, rows, :], qt_scr[h])

        def _update(h, blocks):
            m_prev = m_scr[h]
            m_new = m_prev
            for s, _ in blocks:
                m_new = jnp.maximum(m_new, jnp.max(s, axis=0, keepdims=True))
            alpha = jnp.exp2(m_prev - m_new)
            l_new = alpha * l_scr[h]
            acc_new = alpha * acc_scr[h]
            for s, vt_blk in blocks:
                p = jnp.exp2(s - m_new)
                l_new = l_new + jnp.sum(p, axis=0, keepdims=True)
                acc_new = acc_new + _dot(vt_blk, p.astype(BF16))
            m_scr[h] = m_new
            l_scr[h] = l_new
            acc_scr[h] = acc_new

        for h in range(N_HEADS):
            m_scr[h] = jnp.full((1, tm), MASK_VALUE, F32)
            l_scr[h] = jnp.zeros((1, tm), F32)
            acc_scr[h] = jnp.zeros((V_HEAD, tm), F32)
        s_scr[...] = _scores(0, pl.ds(0, tm))

        def _earlier_tile(j, carry):
            rows = pl.ds(pl.multiple_of(j * tm, tm), tm)
            next_rows = pl.ds(pl.multiple_of((j + 1) * tm, tm), tm)
            s_next = s_scr[...]
            for h in range(N_HEADS):
                s = s_next
                if h + 1 < N_HEADS:
                    s_next = _scores(h + 1, rows)
                else:
                    s_scr[...] = _scores(0, next_rows)
                _update(h, [(s, vt_scr[h, :, rows])])
            return carry

        lax.fori_loop(0, i, _earlier_tile, 0)

        key_chunk = lax.broadcasted_iota(jnp.int32, (tm, tm), 0) // CHUNK
        query_chunk = lax.broadcasted_iota(jnp.int32, (tm, tm), 1) // CHUNK
        diag_valid = key_chunk <= query_chunk
        meta_valid = lax.broadcasted_iota(jnp.int32, (META_PAD, tm), 0) < N_META
        s_next = jnp.where(diag_valid, s_scr[...], MASK_VALUE)
        for h in range(N_HEADS):
            s = s_next
            s_meta = jnp.where(meta_valid, _dot(kmeta_scr[h], qt_scr[h]), MASK_VALUE)
            if h + 1 < N_HEADS:
                s_next = jnp.where(diag_valid, _scores(h + 1, tile_rows), MASK_VALUE)
            else:
                mix_pool = _dot(pool_out, w_out_ref[0:D_POOL, :])
            _update(h, [(s, vt_scr[h, :, tile_rows]), (s_meta, vtmeta_scr[h])])

        attn = [(acc_scr[h] * (1.0 / l_scr[h])).T for h in range(N_HEADS)]
        attn_out = (gate * jnp.concatenate(attn, axis=1)).astype(BF16)

        xn_scr[...] = _rms(x_next_ref[...], norm_g_ref[...]).astype(BF16)
        mix = mix_pool + _dot(attn_out, w_out_ref[D_POOL:, :])
        y_scr[...] = x_ref[...] + mix


def _rope_tables(length):
    half = QK_ROPE // 2
    inv_freq = 1.0 / (ROPE_THETA ** (np.arange(half, dtype=np.float64) / half))
    ang = np.arange(length, dtype=np.float64)[:, None] * inv_freq[None, :]
    return np.cos(ang), np.sin(ang)


def kernel(x, meta_tokens, norm_g, w_in, q_norm_g, w_q_b, kv_norm_g, w_kv_b, pool_w, pool_scale, w_out, final_norm_g):
    batch, seq, d_model = x.shape
    assert d_model == D_MODEL and seq % ROW_TILE == 0 and meta_tokens.shape == (N_META, D_MODEL)
    assert norm_g.shape[0] == 1 and w_in.shape == (1, D_MODEL, D_IN), "single-layer block"
    tm = ROW_TILE
    tiles_per_seq = seq // tm
    n_tiles = batch * tiles_per_seq

    cos, sin = _rope_tables(N_META + seq)
    kcos = np.tile(cos, (1, LANES // (QK_ROPE // 2))).astype(np.float32)
    ksin = np.tile(np.concatenate([-sin, sin], axis=1), (1, LANES // QK_ROPE)).astype(np.float32)
    qcos = np.ascontiguousarray(cos[N_META:].T).astype(np.float32)
    qsin = np.ascontiguousarray(sin[N_META:].T).astype(np.float32)

    def tile_of(s):
        t = jnp.clip(s, 0, n_tiles - 1)
        return t // tiles_per_seq, t % tiles_per_seq

    def rows_at(offset):
        def index_map(s):
            b, i = tile_of(s + offset)
            return b, i, 0
        return pl.BlockSpec((None, tm, D_MODEL), index_map)

    def whole(a):
        return pl.BlockSpec(a.shape, lambda s: (0,) * a.ndim)

    def fetched_once(a, squeeze_layer):
        shape = ((None,) + a.shape[1:]) if squeeze_layer else a.shape
        return pl.BlockSpec(shape, lambda s: (0,) * a.ndim, pipeline_mode=pl.Buffered(1))

    ktable_spec = pl.BlockSpec((tm, LANES), lambda s: (tile_of(s)[1], 0))
    qtable_spec = pl.BlockSpec((QK_ROPE // 2, tm), lambda s: (0, tile_of(s)[1]))

    w_in_t = jnp.transpose(w_in[0])

    operands = [
        (x, rows_at(0)),
        (x, rows_at(1)),
        (meta_tokens, None),
        (kcos[N_META:], ktable_spec),
        (ksin[N_META:], ktable_spec),
        (kcos[:N_META], None),
        (ksin[:N_META], None),
        (qcos, qtable_spec),
        (qsin, qtable_spec),
        (norm_g, None),
        (w_in_t, fetched_once(w_in_t, False)),
        (q_norm_g, None), (w_q_b, fetched_once(w_q_b, True)),
        (kv_norm_g, None), (w_kv_b, fetched_once(w_kv_b, True)),
        (pool_w, fetched_once(pool_w, True)),
        (pool_scale, None),
        (w_out, fetched_once(w_out, True)),
        (final_norm_g.reshape(1, D_MODEL), None),
    ]
    args = [a for a, _ in operands]
    in_specs = [whole(a) if s is None else s for a, s in operands]

    return pl.pallas_call(
        functools.partial(_fused_kernel, tiles_per_seq),
        out_shape=jax.ShapeDtypeStruct(x.shape, x.dtype),
        grid=(n_tiles + 1,),
        in_specs=in_specs,
        out_specs=rows_at(-1),
        scratch_shapes=[
            pltpu.VMEM((D_MODEL, D_IN_PADDED), BF16),
            pltpu.VMEM((N_HEADS * (QK_NOPE + QK_ROPE), Q_LORA), BF16),
            pltpu.VMEM((KV_LORA, N_HEADS * QK_NOPE), BF16),
            pltpu.VMEM((N_HEADS * V_HEAD, KV_LORA), BF16),
            pltpu.VMEM((len(POOL_WINDOWS), POOL_GROUP, POOL_GROUP), BF16),
            pltpu.VMEM((D_MODEL, D_MODEL), BF16),
            pltpu.VMEM((N_HEADS, seq, QK_PAD), BF16),
            pltpu.VMEM((N_HEADS, V_HEAD, seq), BF16),
            pltpu.VMEM((N_HEADS, META_PAD, QK_PAD), BF16),
            pltpu.VMEM((N_HEADS, V_HEAD, META_PAD), BF16),
            pltpu.VMEM((HALO + tm, D_POOL), F32),
            pltpu.VMEM((HALO, D_POOL), F32),
            pltpu.VMEM((N_HEADS, QK_PAD, tm), BF16),
            pltpu.VMEM((tm, tm), F32),
            pltpu.VMEM((N_HEADS, 1, tm), F32),
            pltpu.VMEM((N_HEADS, 1, tm), F32),
            pltpu.VMEM((N_HEADS, V_HEAD, tm), F32),
            pltpu.VMEM((tm, D_MODEL), BF16),
            pltpu.VMEM((tm, D_MODEL), F32),
        ],
        compiler_params=pltpu.CompilerParams(
            dimension_semantics=("arbitrary",),
            vmem_limit_bytes=VMEM_LIMIT_BYTES,
        ),
        name="hybrid_pool_mla_block",
    )(*args)
```

```python
import math

import numpy as np
import jax
import jax.numpy as jnp
from jax import lax
from jax.experimental import pallas as pl
from jax.experimental.pallas import tpu as pltpu

D_MODEL = 1024
N_META = 16
CHUNK = 64
D_POOL = 512
POOL_WINDOWS = (2, 4, 8, 16)
POOL_GROUP = 128
N_HEADS = 4
QK_NOPE = 128
QK_ROPE = 64
V_HEAD = 128
D_ATTN = N_HEADS * V_HEAD
Q_LORA = 256
KV_LORA = 128
ROPE_THETA = 10000.0
EPS = 1e-6
Q_SCALE = (QK_NOPE + QK_ROPE) ** -0.5 * math.log2(math.e)

O_CQ = 2 * D_POOL
O_CKV = O_CQ + Q_LORA
O_KR = O_CKV + KV_LORA
O_AG = O_KR + QK_ROPE
D_IN = O_AG + D_ATTN
S_AG = O_AG + QK_ROPE
D_IN_PADDED = S_AG + D_ATTN

LANES = 128
BF16_ROWS = 16
QK_PAD = 2 * LANES
V_AUG = V_HEAD + BF16_ROWS
HALO = 16
ROW_TILE = 512
MASK_VALUE = -1e30
VMEM_LIMIT_BYTES = 56 * 1024 * 1024

F32 = jnp.float32
BF16 = jnp.bfloat16


def _dot(a, b):
    return jnp.dot(a, b, preferred_element_type=F32)


def _dot_nt(a, b):
    return lax.dot_general(a, b, (((1,), (1,)), ((), ())), preferred_element_type=F32)


def _rms(x, g):
    return x * lax.rsqrt(jnp.mean(x * x, axis=-1, keepdims=True) + EPS) * g


def _silu(x):
    h = 0.5 * x
    return h + h * jnp.tanh(h)


def _rope_lanes(x, cos_t, sin_t):
    lane = lax.broadcasted_iota(jnp.int32, x.shape, 1)
    first_half = (lane % QK_ROPE) < (QK_ROPE // 2)
    partner = jnp.where(first_half,
                        pltpu.roll(x, LANES - QK_ROPE // 2, axis=1),
                        pltpu.roll(x, QK_ROPE // 2, axis=1))
    return x * cos_t + partner * sin_t


def _latent_kv(ckvr, kv_g_ref, w_kb_ref, w_vbt_ref, cos_t, sin_t):
    ckv_n = _rms(ckvr[:, :KV_LORA], kv_g_ref[...]).astype(BF16)
    k_nope = _dot(ckv_n, w_kb_ref[...])
    v_t = _dot_nt(w_vbt_ref[...], ckv_n)
    k_rope = _rope_lanes(ckvr[:, KV_LORA:], cos_t, sin_t)
    return k_nope, v_t, k_rope


def _window_sum(xg, w):
    y = xg
    span = 1
    while span < w:
        y = y + pltpu.roll(y, span, axis=0)
        span *= 2
    return y


def _denominator_rows(n):
    return (lax.broadcasted_iota(jnp.int32, (BF16_ROWS, n), 0) == 0).astype(BF16)


def _fused_kernel(x_ref, meta_ref, kcos_ref, ksin_ref, mcos_ref, msin_ref, qcos_ref, qsin_ref,
                  norm_g_ref, w_in_ref, q_g_ref, w_qb_ref, kv_g_ref, w_kvb_ref,
                  pool_w_f32_ref, pool_scale_ref, w_out_f32_ref, final_g_ref,
                  out_ref,
                  w_in_s, w_qbt_ref, w_kb_ref, w_vbt_ref, pool_w_ref, w_out_ref,
                  k_scr, vt_scr, kmeta_scr, vtmeta_scr, pool_scr, halo_scr,
                  qt_scr, s_scr, m_scr, acc_scr):
    b = pl.program_id(0)
    i = pl.program_id(1)
    tm = ROW_TILE
    first_step = (b == 0) & (i == 0)

    w_pi_ref = w_in_s.at[:, 0:D_POOL]
    w_pg_ref = w_in_s.at[:, D_POOL:O_CQ]
    w_cq_ref = w_in_s.at[:, O_CQ:O_CKV]
    w_ckv_ref = w_in_s.at[:, O_CKV:S_AG]
    w_ag_ref = w_in_s.at[:, S_AG:S_AG + D_ATTN]

    @pl.when(first_step)
    def _prepare_weights():
        for c in range(O_KR // LANES):
            w_in_s[:, c * LANES:(c + 1) * LANES] = w_in_ref[c * LANES:(c + 1) * LANES, :].T.astype(BF16)
        k_rope_rows = w_in_ref[O_KR:O_AG, :]
        w_in_s[:, O_KR:S_AG] = jnp.concatenate([k_rope_rows, k_rope_rows], axis=0).T.astype(BF16)
        for c in range(D_ATTN // LANES):
            w_in_s[:, S_AG + c * LANES:S_AG + (c + 1) * LANES] = (
                w_in_ref[O_AG + c * LANES:O_AG + (c + 1) * LANES, :].T.astype(BF16))
        quarter = D_MODEL // 4
        for r in range(4):
            rs = slice(r * quarter, (r + 1) * quarter)
            w_out_ref[rs, :] = w_out_f32_ref[rs, :].astype(BF16)
        w_qbt_ref[...] = w_qb_ref[...].T.astype(BF16)
        for h in range(N_HEADS):
            base = h * (QK_NOPE + V_HEAD)
            w_kb_ref[:, h * QK_NOPE:(h + 1) * QK_NOPE] = w_kvb_ref[:, base:base + QK_NOPE].astype(BF16)
            w_vbt_ref[h * V_HEAD:(h + 1) * V_HEAD, :] = w_kvb_ref[:, base + QK_NOPE:base + QK_NOPE + V_HEAD].T.astype(BF16)
        pool_w_ref[...] = pool_w_f32_ref[...].astype(BF16)

    @pl.when(first_step)
    def _meta_prologue():
        xn = _rms(meta_ref[...], norm_g_ref[...]).astype(BF16)
        halo_scr[...] = _dot(xn, w_pi_ref[...])
        k_nope, v_t, k_rope = _latent_kv(_dot(xn, w_ckv_ref[...]), kv_g_ref, w_kb_ref, w_vbt_ref,
                                         mcos_ref[...], msin_ref[...])
        for h in range(N_HEADS):
            kmeta_scr[h, :, 0:LANES] = k_nope[:, h * QK_NOPE:(h + 1) * QK_NOPE].astype(BF16)
            kmeta_scr[h, :, LANES:QK_PAD] = k_rope.astype(BF16)
            vtmeta_scr[h, 0:V_HEAD, :] = v_t[h * V_HEAD:(h + 1) * V_HEAD, :].astype(BF16)
            vtmeta_scr[h, V_HEAD:V_AUG, :] = _denominator_rows(N_META)

    @pl.when(i == 0)
    def _seed_halo():
        pool_scr[0:HALO, :] = halo_scr[...]

    x = x_ref[...]
    xn = _rms(x, norm_g_ref[...]).astype(BF16)

    pool_scr[HALO:HALO + tm, :] = _dot(xn, w_pi_ref[...])
    cq = _dot(xn, w_cq_ref[...])
    ckvr = _dot(xn, w_ckv_ref[...])
    pool_gate = _dot(xn, w_pg_ref[...])

    cq_n = _rms(cq, q_g_ref[...]).astype(BF16)
    q_t = _dot_nt(w_qbt_ref[...], cq_n) * Q_SCALE
    k_nope, v_t, k_rope = _latent_kv(ckvr, kv_g_ref, w_kb_ref, w_vbt_ref, kcos_ref[...], ksin_ref[...])
    attn_gate = _dot(xn, w_ag_ref[...])

    pooled = []
    for g, w in enumerate(POOL_WINDOWS):
        xg = pool_scr[:, g * POOL_GROUP:(g + 1) * POOL_GROUP]
        win = _window_sum(xg, w)
        pooled.append((win[HALO:, :] * (1.0 / w) - xg[HALO:, :]).astype(BF16))
    pool_scr[0:HALO, :] = pool_scr[tm:tm + HALO, :]

    row0 = pl.multiple_of(i * tm, tm)
    tile_rows = pl.ds(row0, tm)
    k_rope_bf = k_rope.astype(BF16)
    denominator_rows = _denominator_rows(tm)
    for h in range(N_HEADS):
        k_scr[h, tile_rows, 0:LANES] = k_nope[:, h * QK_NOPE:(h + 1) * QK_NOPE].astype(BF16)
        k_scr[h, tile_rows, LANES:QK_PAD] = k_rope_bf
        vt_scr[h, 0:V_HEAD, tile_rows] = v_t[h * V_HEAD:(h + 1) * V_HEAD, :].astype(BF16)
        vt_scr[h, V_HEAD:V_AUG, tile_rows] = denominator_rows

    cos_q = qcos_ref[...]
    sin_q = qsin_ref[...]
    half = QK_ROPE // 2
    for h in range(N_HEADS):
        base = h * (QK_NOPE + QK_ROPE)
        x1 = q_t[base + QK_NOPE:base + QK_NOPE + half, :]
        x2 = q_t[base + QK_NOPE + half:base + QK_NOPE + QK_ROPE, :]
        qt_scr[h, 0:QK_NOPE, :] = q_t[base:base + QK_NOPE, :].astype(BF16)
        qt_scr[h, QK_NOPE:QK_NOPE + half, :] = (x1 * cos_q - x2 * sin_q).astype(BF16)
        qt_scr[h, QK_NOPE + half:QK_NOPE + QK_ROPE, :] = (x1 * sin_q + x2 * cos_q).astype(BF16)
        qt_scr[h, QK_NOPE + QK_ROPE:QK_PAD, :] = jnp.zeros((QK_PAD - QK_NOPE - QK_ROPE, tm), BF16)

    mixed = [_dot(pooled[g], pool_w_ref[g]) for g in range(len(POOL_WINDOWS))]
    pool_out = (_silu(pool_gate) * (jnp.concatenate(mixed, axis=1) * pool_scale_ref[...])).astype(BF16)
    gate = _silu(attn_gate)

    def _scores(h, rows):
        return _dot(k_scr[h, rows, :], qt_scr[h])

    def _update(h, blocks):
        m_prev = m_scr[h]
        m_new = m_prev
        for s, _ in blocks:
            m_new = jnp.maximum(m_new, jnp.max(s, axis=0, keepdims=True))
        acc_new = jnp.exp2(m_prev - m_new) * acc_scr[h]
        for s, vt_blk in blocks:
            acc_new = acc_new + _dot(vt_blk, jnp.exp2(s - m_new).astype(BF16))
        m_scr[h] = m_new
        acc_scr[h] = acc_new

    for h in range(N_HEADS):
        m_scr[h] = jnp.full((1, tm), MASK_VALUE, F32)
        acc_scr[h] = jnp.zeros((V_AUG, tm), F32)
    s_scr[...] = _scores(0, pl.ds(0, tm))

    def _earlier_tile(j, carry):
        rows = pl.ds(pl.multiple_of(j * tm, tm), tm)
        next_rows = pl.ds(pl.multiple_of((j + 1) * tm, tm), tm)
        s_next = s_scr[...]
        for h in range(N_HEADS):
            s = s_next
            if h + 1 < N_HEADS:
                s_next = _scores(h + 1, rows)
            else:
                s_scr[...] = _scores(0, next_rows)
            _update(h, [(s, vt_scr[h, :, rows])])
        return carry

    lax.fori_loop(0, i, _earlier_tile, 0)

    key_chunk = lax.broadcasted_iota(jnp.int32, (tm, tm), 0) // CHUNK
    query_chunk = lax.broadcasted_iota(jnp.int32, (tm, tm), 1) // CHUNK
    diag_valid = key_chunk <= query_chunk
    s_next = jnp.where(diag_valid, s_scr[...], MASK_VALUE)
    for h in range(N_HEADS):
        s = s_next
        s_meta = _dot(kmeta_scr[h], qt_scr[h])
        if h + 1 < N_HEADS:
            s_next = jnp.where(diag_valid, _scores(h + 1, tile_rows), MASK_VALUE)
        else:
            mix_pool = _dot(pool_out, w_out_ref[0:D_POOL, :])
        _update(h, [(s, vt_scr[h, :, tile_rows]), (s_meta, vtmeta_scr[h])])

    attn = []
    for h in range(N_HEADS):
        acc = acc_scr[h]
        attn.append((acc[0:V_HEAD, :] * (1.0 / acc[V_HEAD:V_HEAD + 1, :])).T)
    attn_out = (gate * jnp.concatenate(attn, axis=1)).astype(BF16)

    mix = mix_pool + _dot(attn_out, w_out_ref[D_POOL:, :])
    out_ref[...] = _rms(x + mix, final_g_ref[...])


def _rope_tables(length):
    half = QK_ROPE // 2
    inv_freq = 1.0 / (ROPE_THETA ** (np.arange(half, dtype=np.float64) / half))
    ang = np.arange(length, dtype=np.float64)[:, None] * inv_freq[None, :]
    return np.cos(ang), np.sin(ang)


def kernel(x, meta_tokens, norm_g, w_in, q_norm_g, w_q_b, kv_norm_g, w_kv_b, pool_w, pool_scale, w_out, final_norm_g):
    batch, seq, d_model = x.shape
    assert d_model == D_MODEL and seq % ROW_TILE == 0 and meta_tokens.shape == (N_META, D_MODEL)
    assert norm_g.shape[0] == 1 and w_in.shape == (1, D_MODEL, D_IN), "single-layer block"
    tm = ROW_TILE

    cos, sin = _rope_tables(N_META + seq)
    kcos = np.tile(cos, (1, LANES // (QK_ROPE // 2))).astype(np.float32)
    ksin = np.tile(np.concatenate([-sin, sin], axis=1), (1, LANES // QK_ROPE)).astype(np.float32)
    qcos = np.ascontiguousarray(cos[N_META:].T).astype(np.float32)
    qsin = np.ascontiguousarray(sin[N_META:].T).astype(np.float32)

    def whole(a):
        return pl.BlockSpec(a.shape, lambda b, i: (0,) * a.ndim)

    def fetched_once(a, squeeze_layer):
        shape = ((None,) + a.shape[1:]) if squeeze_layer else a.shape
        return pl.BlockSpec(shape, lambda b, i: (0,) * a.ndim, pipeline_mode=pl.Buffered(1))

    row_spec = pl.BlockSpec((None, tm, D_MODEL), lambda b, i: (b, i, 0))
    ktable_spec = pl.BlockSpec((tm, LANES), lambda b, i: (i, 0))
    qtable_spec = pl.BlockSpec((QK_ROPE // 2, tm), lambda b, i: (0, i))

    w_in_t = jnp.transpose(w_in[0])

    operands = [
        (x, row_spec),
        (meta_tokens, None),
        (kcos[N_META:], ktable_spec),
        (ksin[N_META:], ktable_spec),
        (kcos[:N_META], None),
        (ksin[:N_META], None),
        (qcos, qtable_spec),
        (qsin, qtable_spec),
        (norm_g, None),
        (w_in_t, fetched_once(w_in_t, False)),
        (q_norm_g, None), (w_q_b, fetched_once(w_q_b, True)),
        (kv_norm_g, None), (w_kv_b, fetched_once(w_kv_b, True)),
        (pool_w, fetched_once(pool_w, True)),
        (pool_scale, None),
        (w_out, fetched_once(w_out, True)),
        (final_norm_g.reshape(1, D_MODEL), None),
    ]
    args = [a for a, _ in operands]
    in_specs = [whole(a) if s is None else s for a, s in operands]

    return pl.pallas_call(
        _fused_kernel,
        out_shape=jax.ShapeDtypeStruct(x.shape, x.dtype),
        grid=(batch, seq // tm),
        in_specs=in_specs,
        out_specs=row_spec,
        scratch_shapes=[
            pltpu.VMEM((D_MODEL, D_IN_PADDED), BF16),
            pltpu.VMEM((N_HEADS * (QK_NOPE + QK_ROPE), Q_LORA), BF16),
            pltpu.VMEM((KV_LORA, N_HEADS * QK_NOPE), BF16),
            pltpu.VMEM((N_HEADS * V_HEAD, KV_LORA), BF16),
            pltpu.VMEM((len(POOL_WINDOWS), POOL_GROUP, POOL_GROUP), BF16),
            pltpu.VMEM((D_MODEL, D_MODEL), BF16),
            pltpu.VMEM((N_HEADS, seq, QK_PAD), BF16),
            pltpu.VMEM((N_HEADS, V_AUG, seq), BF16),
            pltpu.VMEM((N_HEADS, N_META, QK_PAD), BF16),
            pltpu.VMEM((N_HEADS, V_AUG, N_META), BF16),
            pltpu.VMEM((HALO + tm, D_POOL), F32),
            pltpu.VMEM((HALO, D_POOL), F32),
            pltpu.VMEM((N_HEADS, QK_PAD, tm), BF16),
            pltpu.VMEM((tm, tm), F32),
            pltpu.VMEM((N_HEADS, 1, tm), F32),
            pltpu.VMEM((N_HEADS, V_AUG, tm), F32),
        ],
        compiler_params=pltpu.CompilerParams(
            dimension_semantics=("arbitrary", "arbitrary"),
            vmem_limit_bytes=VMEM_LIMIT_BYTES,
        ),
        name="hybrid_pool_mla_block",
    )(*args)
```

```python
import math

import numpy as np
import jax
import jax.numpy as jnp
from jax import lax
from jax.experimental import pallas as pl
from jax.experimental.pallas import tpu as pltpu

D_MODEL = 1024
N_META = 16
CHUNK = 64
D_POOL = 512
POOL_WINDOWS = (2, 4, 8, 16)
POOL_GROUP = 128
N_HEADS = 4
QK_NOPE = 128
QK_ROPE = 64
V_HEAD = 128
D_ATTN = N_HEADS * V_HEAD
Q_LORA = 256
KV_LORA = 128
ROPE_THETA = 10000.0
EPS = 1e-6
Q_SCALE = (QK_NOPE + QK_ROPE) ** -0.5 * math.log2(math.e)

O_CQ = 2 * D_POOL
O_CKV = O_CQ + Q_LORA
O_KR = O_CKV + KV_LORA
O_AG = O_KR + QK_ROPE
D_IN = O_AG + D_ATTN
S_AG = O_AG + QK_ROPE
D_IN_PADDED = S_AG + D_ATTN

LANES = 128
BF16_ROWS = 16
QK_PAD = 2 * LANES
V_AUG = V_HEAD + BF16_ROWS
HALO = 16
ROW_TILE = 512
LOOKAHEAD = 1
MASK_VALUE = -1e30
VMEM_LIMIT_BYTES = 56 * 1024 * 1024

F32 = jnp.float32
BF16 = jnp.bfloat16


def _dot(a, b):
    return jnp.dot(a, b, preferred_element_type=F32)


def _dot_nt(a, b):
    return lax.dot_general(a, b, (((1,), (1,)), ((), ())), preferred_element_type=F32)


def _rms(x, g):
    return x * lax.rsqrt(jnp.mean(x * x, axis=-1, keepdims=True) + EPS) * g


def _silu(x):
    h = 0.5 * x
    return h + h * jnp.tanh(h)


def _rope_lanes(x, cos_t, sin_t):
    lane = lax.broadcasted_iota(jnp.int32, x.shape, 1)
    first_half = (lane % QK_ROPE) < (QK_ROPE // 2)
    partner = jnp.where(first_half,
                        pltpu.roll(x, LANES - QK_ROPE // 2, axis=1),
                        pltpu.roll(x, QK_ROPE // 2, axis=1))
    return x * cos_t + partner * sin_t


def _latent_kv(ckvr, kv_g_ref, w_kb_ref, w_vbt_ref, cos_t, sin_t):
    ckv_n = _rms(ckvr[:, :KV_LORA], kv_g_ref[...]).astype(BF16)
    k_nope = _dot(ckv_n, w_kb_ref[...])
    v_t = _dot_nt(w_vbt_ref[...], ckv_n)
    k_rope = _rope_lanes(ckvr[:, KV_LORA:], cos_t, sin_t)
    return k_nope, v_t, k_rope


def _window_sum(xg, w):
    y = xg
    span = 1
    while span < w:
        y = y + pltpu.roll(y, span, axis=0)
        span *= 2
    return y


def _denominator_rows(n):
    return (lax.broadcasted_iota(jnp.int32, (BF16_ROWS, n), 0) == 0).astype(BF16)


def _fused_kernel(x_ref, meta_ref, kcos_ref, ksin_ref, mcos_ref, msin_ref, qcos_ref, qsin_ref,
                  norm_g_ref, w_in_ref, q_g_ref, w_qb_ref, kv_g_ref, w_kvb_ref,
                  pool_w_f32_ref, pool_scale_ref, w_out_f32_ref, final_g_ref,
                  out_ref,
                  w_in_s, w_qbt_ref, w_kb_ref, w_vbt_ref, pool_w_ref, w_out_ref,
                  k_scr, vt_scr, kmeta_scr, vtmeta_scr, pool_scr, halo_scr,
                  qt_scr, s_scr, m_scr, acc_scr):
    b = pl.program_id(0)
    i = pl.program_id(1)
    tm = ROW_TILE
    first_step = (b == 0) & (i == 0)

    w_pi_ref = w_in_s.at[:, 0:D_POOL]
    w_pg_ref = w_in_s.at[:, D_POOL:O_CQ]
    w_cq_ref = w_in_s.at[:, O_CQ:O_CKV]
    w_ckv_ref = w_in_s.at[:, O_CKV:S_AG]
    w_ag_ref = w_in_s.at[:, S_AG:S_AG + D_ATTN]

    @pl.when(first_step)
    def _prepare_weights():
        for c in range(O_KR // LANES):
            w_in_s[:, c * LANES:(c + 1) * LANES] = w_in_ref[c * LANES:(c + 1) * LANES, :].T.astype(BF16)
        k_rope_rows = w_in_ref[O_KR:O_AG, :]
        w_in_s[:, O_KR:S_AG] = jnp.concatenate([k_rope_rows, k_rope_rows], axis=0).T.astype(BF16)
        for c in range(D_ATTN // LANES):
            w_in_s[:, S_AG + c * LANES:S_AG + (c + 1) * LANES] = (
                w_in_ref[O_AG + c * LANES:O_AG + (c + 1) * LANES, :].T.astype(BF16))
        quarter = D_MODEL // 4
        for r in range(4):
            rs = slice(r * quarter, (r + 1) * quarter)
            w_out_ref[rs, :] = w_out_f32_ref[rs, :].astype(BF16)
        w_qbt_ref[...] = w_qb_ref[...].T.astype(BF16)
        for h in range(N_HEADS):
            base = h * (QK_NOPE + V_HEAD)
            w_kb_ref[:, h * QK_NOPE:(h + 1) * QK_NOPE] = w_kvb_ref[:, base:base + QK_NOPE].astype(BF16)
            w_vbt_ref[h * V_HEAD:(h + 1) * V_HEAD, :] = w_kvb_ref[:, base + QK_NOPE:base + QK_NOPE + V_HEAD].T.astype(BF16)
        pool_w_ref[...] = pool_w_f32_ref[...].astype(BF16)

    @pl.when(first_step)
    def _meta_prologue():
        xn = _rms(meta_ref[...], norm_g_ref[...]).astype(BF16)
        halo_scr[...] = _dot(xn, w_pi_ref[...])
        k_nope, v_t, k_rope = _latent_kv(_dot(xn, w_ckv_ref[...]), kv_g_ref, w_kb_ref, w_vbt_ref,
                                         mcos_ref[...], msin_ref[...])
        for h in range(N_HEADS):
            kmeta_scr[h, :, 0:LANES] = k_nope[:, h * QK_NOPE:(h + 1) * QK_NOPE].astype(BF16)
            kmeta_scr[h, :, LANES:QK_PAD] = k_rope.astype(BF16)
            vtmeta_scr[h, 0:V_HEAD, :] = v_t[h * V_HEAD:(h + 1) * V_HEAD, :].astype(BF16)
            vtmeta_scr[h, V_HEAD:V_AUG, :] = _denominator_rows(N_META)

    @pl.when(i == 0)
    def _seed_halo():
        pool_scr[0:HALO, :] = halo_scr[...]

    hm = tm // 2
    halves = (slice(0, hm), slice(hm, tm))
    xn_halves = [_rms(x_ref[r, :], norm_g_ref[...]).astype(BF16) for r in halves]

    for r, xn_half in zip(halves, xn_halves):
        pool_scr[HALO + r.start:HALO + r.stop, :] = _dot(xn_half, w_pi_ref[...])
    xn = jnp.concatenate(xn_halves, axis=0)
    cq = _dot(xn, w_cq_ref[...])
    ckvr = _dot(xn, w_ckv_ref[...])
    pool_gate = _dot(xn, w_pg_ref[...])

    cq_n = _rms(cq, q_g_ref[...]).astype(BF16)
    q_t = _dot_nt(w_qbt_ref[...], cq_n) * Q_SCALE
    k_nope, v_t, k_rope = _latent_kv(ckvr, kv_g_ref, w_kb_ref, w_vbt_ref, kcos_ref[...], ksin_ref[...])
    attn_gate = _dot(xn, w_ag_ref[...])

    pooled = []
    for g, w in enumerate(POOL_WINDOWS):
        xg = pool_scr[:, g * POOL_GROUP:(g + 1) * POOL_GROUP]
        win = _window_sum(xg, w)
        pooled.append((win[HALO:, :] * (1.0 / w) - xg[HALO:, :]).astype(BF16))
    pool_scr[0:HALO, :] = pool_scr[tm:tm + HALO, :]

    row0 = pl.multiple_of(i * tm, tm)
    tile_rows = pl.ds(row0, tm)
    k_rope_bf = k_rope.astype(BF16)
    denominator_rows = _denominator_rows(tm)
    for h in range(N_HEADS):
        k_scr[h, tile_rows, 0:LANES] = k_nope[:, h * QK_NOPE:(h + 1) * QK_NOPE].astype(BF16)
        k_scr[h, tile_rows, LANES:QK_PAD] = k_rope_bf
        vt_scr[h, 0:V_HEAD, tile_rows] = v_t[h * V_HEAD:(h + 1) * V_HEAD, :].astype(BF16)
        vt_scr[h, V_HEAD:V_AUG, tile_rows] = denominator_rows

    cos_q = qcos_ref[...]
    sin_q = qsin_ref[...]
    half = QK_ROPE // 2
    for h in range(N_HEADS):
        base = h * (QK_NOPE + QK_ROPE)
        x1 = q_t[base + QK_NOPE:base + QK_NOPE + half, :]
        x2 = q_t[base + QK_NOPE + half:base + QK_NOPE + QK_ROPE, :]
        qt_scr[h, 0:QK_NOPE, :] = q_t[base:base + QK_NOPE, :].astype(BF16)
        qt_scr[h, QK_NOPE:QK_NOPE + half, :] = (x1 * cos_q - x2 * sin_q).astype(BF16)
        qt_scr[h, QK_NOPE + half:QK_NOPE + QK_ROPE, :] = (x1 * sin_q + x2 * cos_q).astype(BF16)
        qt_scr[h, QK_NOPE + QK_ROPE:QK_PAD, :] = jnp.zeros((QK_PAD - QK_NOPE - QK_ROPE, tm), BF16)

    mixed = [_dot(pooled[g], pool_w_ref[g]) for g in range(len(POOL_WINDOWS))]
    pool_out = (_silu(pool_gate) * (jnp.concatenate(mixed, axis=1) * pool_scale_ref[...])).astype(BF16)
    gate = _silu(attn_gate)

    def _scores(h, rows):
        return _dot(k_scr[h, rows, :], qt_scr[h])

    def _update(h, blocks):
        m_prev = m_scr[h]
        m_new = m_prev
        for s, _ in blocks:
            m_new = jnp.maximum(m_new, jnp.max(s, axis=0, keepdims=True))
        acc_new = jnp.exp2(m_prev - m_new) * acc_scr[h]
        for s, vt_blk in blocks:
            acc_new = acc_new + _dot(vt_blk, jnp.exp2(s - m_new).astype(BF16))
        m_scr[h] = m_new
        acc_scr[h] = acc_new

    for h in range(N_HEADS):
        m_scr[h] = jnp.full((1, tm), MASK_VALUE, F32)
        acc_scr[h] = jnp.zeros((V_AUG, tm), F32)
    for h in range(LOOKAHEAD):
        s_scr[h] = _scores(h, pl.ds(0, tm))

    def _earlier_tile(j, carry):
        rows = pl.ds(pl.multiple_of(j * tm, tm), tm)
        next_rows = pl.ds(pl.multiple_of((j + 1) * tm, tm), tm)
        ahead = [s_scr[h] for h in range(LOOKAHEAD)]
        for h in range(N_HEADS):
            s = ahead.pop(0)
            if h + LOOKAHEAD < N_HEADS:
                ahead.append(_scores(h + LOOKAHEAD, rows))
            else:
                s_scr[h + LOOKAHEAD - N_HEADS] = _scores(h + LOOKAHEAD - N_HEADS, next_rows)
            _update(h, [(s, vt_scr[h, :, rows])])
        return carry

    lax.fori_loop(0, i, _earlier_tile, 0)

    key_chunk = lax.broadcasted_iota(jnp.int32, (tm, tm), 0) // CHUNK
    query_chunk = lax.broadcasted_iota(jnp.int32, (tm, tm), 1) // CHUNK
    diag_valid = key_chunk <= query_chunk
    ahead = [jnp.where(diag_valid, s_scr[h], MASK_VALUE) for h in range(LOOKAHEAD)]
    for h in range(N_HEADS):
        s = ahead.pop(0)
        s_meta = _dot(kmeta_scr[h], qt_scr[h])
        if h + LOOKAHEAD < N_HEADS:
            ahead.append(jnp.where(diag_valid, _scores(h + LOOKAHEAD, tile_rows), MASK_VALUE))
        elif h + LOOKAHEAD == N_HEADS:
            mix_pool = _dot(pool_out, w_out_ref[0:D_POOL, :])
        _update(h, [(s, vt_scr[h, :, tile_rows]), (s_meta, vtmeta_scr[h])])

    attn = []
    for h in range(N_HEADS):
        acc = acc_scr[h]
        attn.append((acc[0:V_HEAD, :] * (1.0 / acc[V_HEAD:V_HEAD + 1, :])).T)
    attn_out = (gate * jnp.concatenate(attn, axis=1)).astype(BF16)

    for r in halves:
        mix = mix_pool[r, :] + _dot(attn_out[r, :], w_out_ref[D_POOL:, :])
        out_ref[r, :] = _rms(x_ref[r, :] + mix, final_g_ref[...])


def _rope_tables(length):
    half = QK_ROPE // 2
    inv_freq = 1.0 / (ROPE_THETA ** (np.arange(half, dtype=np.float64) / half))
    ang = np.arange(length, dtype=np.float64)[:, None] * inv_freq[None, :]
    return np.cos(ang), np.sin(ang)


def kernel(x, meta_tokens, norm_g, w_in, q_norm_g, w_q_b, kv_norm_g, w_kv_b, pool_w, pool_scale, w_out, final_norm_g):
    batch, seq, d_model = x.shape
    assert d_model == D_MODEL and seq % ROW_TILE == 0 and meta_tokens.shape == (N_META, D_MODEL)
    assert norm_g.shape[0] == 1 and w_in.shape == (1, D_MODEL, D_IN), "single-layer block"
    tm = ROW_TILE

    cos, sin = _rope_tables(N_META + seq)
    kcos = np.tile(cos, (1, LANES // (QK_ROPE // 2))).astype(np.float32)
    ksin = np.tile(np.concatenate([-sin, sin], axis=1), (1, LANES // QK_ROPE)).astype(np.float32)
    qcos = np.ascontiguousarray(cos[N_META:].T).astype(np.float32)
    qsin = np.ascontiguousarray(sin[N_META:].T).astype(np.float32)

    def whole(a):
        return pl.BlockSpec(a.shape, lambda b, i: (0,) * a.ndim)

    def fetched_once(a, squeeze_layer):
        shape = ((None,) + a.shape[1:]) if squeeze_layer else a.shape
        return pl.BlockSpec(shape, lambda b, i: (0,) * a.ndim, pipeline_mode=pl.Buffered(1))

    row_spec = pl.BlockSpec((None, tm, D_MODEL), lambda b, i: (b, i, 0))
    ktable_spec = pl.BlockSpec((tm, LANES), lambda b, i: (i, 0))
    qtable_spec = pl.BlockSpec((QK_ROPE // 2, tm), lambda b, i: (0, i))

    w_in_t = jnp.transpose(w_in[0])

    operands = [
        (x, row_spec),
        (meta_tokens, None),
        (kcos[N_META:], ktable_spec),
        (ksin[N_META:], ktable_spec),
        (kcos[:N_META], None),
        (ksin[:N_META], None),
        (qcos, qtable_spec),
        (qsin, qtable_spec),
        (norm_g, None),
        (w_in_t, fetched_once(w_in_t, False)),
        (q_norm_g, None), (w_q_b, fetched_once(w_q_b, True)),
        (kv_norm_g, None), (w_kv_b, fetched_once(w_kv_b, True)),
        (pool_w, fetched_once(pool_w, True)),
        (pool_scale, None),
        (w_out, fetched_once(w_out, True)),
        (final_norm_g.reshape(1, D_MODEL), None),
    ]
    args = [a for a, _ in operands]
    in_specs = [whole(a) if s is None else s for a, s in operands]

    return pl.pallas_call(
        _fused_kernel,
        out_shape=jax.ShapeDtypeStruct(x.shape, x.dtype),
        grid=(batch, seq // tm),
        in_specs=in_specs,
        out_specs=row_spec,
        scratch_shapes=[
            pltpu.VMEM((D_MODEL, D_IN_PADDED), BF16),
            pltpu.VMEM((N_HEADS * (QK_NOPE + QK_ROPE), Q_LORA), BF16),
            pltpu.VMEM((KV_LORA, N_HEADS * QK_NOPE), BF16),
            pltpu.VMEM((N_HEADS * V_HEAD, KV_LORA), BF16),
            pltpu.VMEM((len(POOL_WINDOWS), POOL_GROUP, POOL_GROUP), BF16),
            pltpu.VMEM((D_MODEL, D_MODEL), BF16),
            pltpu.VMEM((N_HEADS, seq, QK_PAD), BF16),
            pltpu.VMEM((N_HEADS, V_AUG, seq), BF16),
            pltpu.VMEM((N_HEADS, N_META, QK_PAD), BF16),
            pltpu.VMEM((N_HEADS, V_AUG, N_META), BF16),
            pltpu.VMEM((HALO + tm, D_POOL), F32),
            pltpu.VMEM((HALO, D_POOL), F32),
            pltpu.VMEM((N_HEADS, QK_PAD, tm), BF16),
            pltpu.VMEM((LOOKAHEAD, tm, tm), F32),
            pltpu.VMEM((N_HEADS, 1, tm), F32),
            pltpu.VMEM((N_HEADS, V_AUG, tm), F32),
        ],
        compiler_params=pltpu.CompilerParams(
            dimension_semantics=("arbitrary", "arbitrary"),
            vmem_limit_bytes=VMEM_LIMIT_BYTES,
        ),
        name="hybrid_pool_mla_block",
    )(*args)
```

```python
import math

import numpy as np
import jax
import jax.numpy as jnp
from jax import lax
from jax.experimental import pallas as pl
from jax.experimental.pallas import tpu as pltpu

D_MODEL = 1024
N_META = 16
CHUNK = 64
D_POOL = 512
POOL_WINDOWS = (2, 4, 8, 16)
POOL_GROUP = 128
N_HEADS = 4
QK_NOPE = 128
QK_ROPE = 64
V_HEAD = 128
D_ATTN = N_HEADS * V_HEAD
Q_LORA = 256
KV_LORA = 128
ROPE_THETA = 10000.0
EPS = 1e-6
Q_SCALE = (QK_NOPE + QK_ROPE) ** -0.5 * math.log2(math.e)

O_CQ = 2 * D_POOL
O_CKV = O_CQ + Q_LORA
O_KR = O_CKV + KV_LORA
O_AG = O_KR + QK_ROPE
D_IN = O_AG + D_ATTN
S_AG = O_AG + QK_ROPE
D_IN_PADDED = S_AG + D_ATTN

LANES = 128
BF16_ROWS = 16
QK_PAD = 2 * LANES
V_AUG = V_HEAD + BF16_ROWS
HALO = 16
ROW_TILE = 512
EXP_HEADROOM = 64.0
MASK_VALUE = -1e30
VMEM_LIMIT_BYTES = 56 * 1024 * 1024

F32 = jnp.float32
BF16 = jnp.bfloat16


def _dot(a, b):
    return jnp.dot(a, b, preferred_element_type=F32)


def _dot_nt(a, b):
    return lax.dot_general(a, b, (((1,), (1,)), ((), ())), preferred_element_type=F32)


def _rms(x, g):
    return x * lax.rsqrt(jnp.mean(x * x, axis=-1, keepdims=True) + EPS) * g


def _silu(x):
    h = 0.5 * x
    return h + h * jnp.tanh(h)


def _rope_lanes(x, cos_t, sin_t):
    lane = lax.broadcasted_iota(jnp.int32, x.shape, 1)
    first_half = (lane % QK_ROPE) < (QK_ROPE // 2)
    partner = jnp.where(first_half,
                        pltpu.roll(x, LANES - QK_ROPE // 2, axis=1),
                        pltpu.roll(x, QK_ROPE // 2, axis=1))
    return x * cos_t + partner * sin_t


def _latent_kv(ckvr, kv_g_ref, w_kb_ref, w_vbt_ref, cos_t, sin_t):
    ckv_n = _rms(ckvr[:, :KV_LORA], kv_g_ref[...]).astype(BF16)
    k_nope = _dot(ckv_n, w_kb_ref[...])
    v_t = _dot_nt(w_vbt_ref[...], ckv_n)
    k_rope = _rope_lanes(ckvr[:, KV_LORA:], cos_t, sin_t)
    return k_nope, v_t, k_rope


def _window_sum(xg, w):
    y = xg
    span = 1
    while span < w:
        y = y + pltpu.roll(y, span, axis=0)
        span *= 2
    return y


def _denominator_rows(n):
    return (lax.broadcasted_iota(jnp.int32, (BF16_ROWS, n), 0) == 0).astype(BF16)


def _fused_kernel(x_ref, meta_ref, kcos_ref, ksin_ref, mcos_ref, msin_ref, qcos_ref, qsin_ref,
                  norm_g_ref, w_in_ref, q_g_ref, w_qb_ref, kv_g_ref, w_kvb_ref,
                  pool_w_f32_ref, pool_scale_ref, w_out_f32_ref, final_g_ref,
                  out_ref,
                  w_in_s, w_qbt_ref, w_kb_ref, w_vbt_ref, pool_w_ref, w_out_ref,
                  k_scr, vt_scr, kmeta_scr, vtmeta_scr, pool_scr, halo_scr,
                  qt_scr, s_scr, m_scr, acc_scr, rise_scr):
    b = pl.program_id(0)
    i = pl.program_id(1)
    tm = ROW_TILE
    first_step = (b == 0) & (i == 0)

    w_pi_ref = w_in_s.at[:, 0:D_POOL]
    w_pg_ref = w_in_s.at[:, D_POOL:O_CQ]
    w_cq_ref = w_in_s.at[:, O_CQ:O_CKV]
    w_ckv_ref = w_in_s.at[:, O_CKV:S_AG]
    w_ag_ref = w_in_s.at[:, S_AG:S_AG + D_ATTN]

    @pl.when(first_step)
    def _prepare_weights():
        for c in range(O_KR // LANES):
            w_in_s[:, c * LANES:(c + 1) * LANES] = w_in_ref[c * LANES:(c + 1) * LANES, :].T.astype(BF16)
        k_rope_rows = w_in_ref[O_KR:O_AG, :]
        w_in_s[:, O_KR:S_AG] = jnp.concatenate([k_rope_rows, k_rope_rows], axis=0).T.astype(BF16)
        for c in range(D_ATTN // LANES):
            w_in_s[:, S_AG + c * LANES:S_AG + (c + 1) * LANES] = (
                w_in_ref[O_AG + c * LANES:O_AG + (c + 1) * LANES, :].T.astype(BF16))
        quarter = D_MODEL // 4
        for r in range(4):
            rs = slice(r * quarter, (r + 1) * quarter)
            w_out_ref[rs, :] = w_out_f32_ref[rs, :].astype(BF16)
        w_qbt_ref[...] = w_qb_ref[...].T.astype(BF16)
        for h in range(N_HEADS):
            base = h * (QK_NOPE + V_HEAD)
            w_kb_ref[:, h * QK_NOPE:(h + 1) * QK_NOPE] = w_kvb_ref[:, base:base + QK_NOPE].astype(BF16)
            w_vbt_ref[h * V_HEAD:(h + 1) * V_HEAD, :] = w_kvb_ref[:, base + QK_NOPE:base + QK_NOPE + V_HEAD].T.astype(BF16)
        pool_w_ref[...] = pool_w_f32_ref[...].astype(BF16)

    @pl.when(first_step)
    def _meta_prologue():
        xn = _rms(meta_ref[...], norm_g_ref[...]).astype(BF16)
        halo_scr[...] = _dot(xn, w_pi_ref[...])
        k_nope, v_t, k_rope = _latent_kv(_dot(xn, w_ckv_ref[...]), kv_g_ref, w_kb_ref, w_vbt_ref,
                                         mcos_ref[...], msin_ref[...])
        for h in range(N_HEADS):
            kmeta_scr[h, :, 0:LANES] = k_nope[:, h * QK_NOPE:(h + 1) * QK_NOPE].astype(BF16)
            kmeta_scr[h, :, LANES:QK_PAD] = k_rope.astype(BF16)
            vtmeta_scr[h, 0:V_HEAD, :] = v_t[h * V_HEAD:(h + 1) * V_HEAD, :].astype(BF16)
            vtmeta_scr[h, V_HEAD:V_AUG, :] = _denominator_rows(N_META)

    @pl.when(i == 0)
    def _seed_halo():
        pool_scr[0:HALO, :] = halo_scr[...]

    hm = tm // 2
    halves = (slice(0, hm), slice(hm, tm))
    xn_halves = [_rms(x_ref[r, :], norm_g_ref[...]).astype(BF16) for r in halves]

    for r, xn_half in zip(halves, xn_halves):
        pool_scr[HALO + r.start:HALO + r.stop, :] = _dot(xn_half, w_pi_ref[...])
    xn = jnp.concatenate(xn_halves, axis=0)
    cq = _dot(xn, w_cq_ref[...])
    ckvr = _dot(xn, w_ckv_ref[...])
    pool_gate = _dot(xn, w_pg_ref[...])

    cq_n = _rms(cq, q_g_ref[...]).astype(BF16)
    q_t = _dot_nt(w_qbt_ref[...], cq_n) * Q_SCALE
    k_nope, v_t, k_rope = _latent_kv(ckvr, kv_g_ref, w_kb_ref, w_vbt_ref, kcos_ref[...], ksin_ref[...])
    attn_gate = _dot(xn, w_ag_ref[...])

    pooled = []
    for g, w in enumerate(POOL_WINDOWS):
        xg = pool_scr[:, g * POOL_GROUP:(g + 1) * POOL_GROUP]
        win = _window_sum(xg, w)
        pooled.append((win[HALO:, :] * (1.0 / w) - xg[HALO:, :]).astype(BF16))
    pool_scr[0:HALO, :] = pool_scr[tm:tm + HALO, :]

    row0 = pl.multiple_of(i * tm, tm)
    tile_rows = pl.ds(row0, tm)
    k_rope_bf = k_rope.astype(BF16)
    denominator_rows = _denominator_rows(tm)
    for h in range(N_HEADS):
        k_scr[h, tile_rows, 0:LANES] = k_nope[:, h * QK_NOPE:(h + 1) * QK_NOPE].astype(BF16)
        k_scr[h, tile_rows, LANES:QK_PAD] = k_rope_bf
        vt_scr[h, 0:V_HEAD, tile_rows] = v_t[h * V_HEAD:(h + 1) * V_HEAD, :].astype(BF16)
        vt_scr[h, V_HEAD:V_AUG, tile_rows] = denominator_rows

    cos_q = qcos_ref[...]
    sin_q = qsin_ref[...]
    half = QK_ROPE // 2
    for h in range(N_HEADS):
        base = h * (QK_NOPE + QK_ROPE)
        x1 = q_t[base + QK_NOPE:base + QK_NOPE + half, :]
        x2 = q_t[base + QK_NOPE + half:base + QK_NOPE + QK_ROPE, :]
        qt_scr[h, 0:QK_NOPE, :] = q_t[base:base + QK_NOPE, :].astype(BF16)
        qt_scr[h, QK_NOPE:QK_NOPE + half, :] = (x1 * cos_q - x2 * sin_q).astype(BF16)
        qt_scr[h, QK_NOPE + half:QK_NOPE + QK_ROPE, :] = (x1 * sin_q + x2 * cos_q).astype(BF16)
        qt_scr[h, QK_NOPE + QK_ROPE:QK_PAD, :] = jnp.zeros((QK_PAD - QK_NOPE - QK_ROPE, tm), BF16)

    mixed = [_dot(pooled[g], pool_w_ref[g]) for g in range(len(POOL_WINDOWS))]
    pool_out = (_silu(pool_gate) * (jnp.concatenate(mixed, axis=1) * pool_scale_ref[...])).astype(BF16)

    gate = _silu(attn_gate)

    def _scores(h, rows):
        return _dot(k_scr[h, rows, :], qt_scr[h])

    def _exact_update(h, blocks):
        m_prev = m_scr[h]
        m_new = m_prev
        for s, _ in blocks:
            m_new = jnp.maximum(m_new, jnp.max(s, axis=0, keepdims=True))
        acc = jnp.exp2(m_prev - m_new) * acc_scr[h]
        for s, vt_blk in blocks:
            acc = acc + _dot(vt_blk, jnp.exp2(s - m_new).astype(BF16))
        acc_scr[h] = acc
        m_scr[h] = m_new

    def _stale_update(h, blocks):
        m_ref = m_scr[h]
        acc = acc_scr[h]
        rise = rise_scr[...]
        lift = jnp.zeros_like(m_ref)
        for s, vt_blk in blocks:
            t = s - m_ref
            top = jnp.max(t, axis=0, keepdims=True)
            acc = acc + _dot(vt_blk, jnp.exp2(t).astype(BF16))
            rise = jnp.maximum(rise, top)
            lift = jnp.maximum(lift, top)
        acc_scr[h] = acc * jnp.exp2(-lift)
        m_scr[h] = m_ref + lift
        rise_scr[...] = rise

    key_chunk = lax.broadcasted_iota(jnp.int32, (tm, tm), 0) // CHUNK
    query_chunk = lax.broadcasted_iota(jnp.int32, (tm, tm), 1) // CHUNK
    diag_valid = key_chunk <= query_chunk

    def _attend(update, filler):
        s_meta = [_dot(kmeta_scr[h], qt_scr[h]) for h in range(N_HEADS)]
        for h in range(N_HEADS):
            m_scr[h] = jnp.max(s_meta[h], axis=0, keepdims=True)
            acc_scr[h] = jnp.zeros((V_AUG, tm), F32)

        s_next = jnp.where(diag_valid, _scores(0, tile_rows), MASK_VALUE)
        for h in range(N_HEADS):
            s = s_next
            if h + 1 < N_HEADS:
                s_next = jnp.where(diag_valid, _scores(h + 1, tile_rows), MASK_VALUE)
            else:
                s_scr[...] = _scores(0, pl.ds(0, tm))
                filler()
            update(h, [(s_meta[h], vtmeta_scr[h]), (s, vt_scr[h, :, tile_rows])])

        def _earlier_tile(j, carry):
            rows = pl.ds(pl.multiple_of(j * tm, tm), tm)
            next_rows = pl.ds(pl.multiple_of((j + 1) * tm, tm), tm)
            s_next = s_scr[...]
            for h in range(N_HEADS):
                s = s_next
                if h + 1 < N_HEADS:
                    s_next = _scores(h + 1, rows)
                else:
                    s_scr[...] = _scores(0, next_rows)
                update(h, [(s, vt_scr[h, :, rows])])
            return carry

        lax.fori_loop(0, i, _earlier_tile, 0)

    def _emit_output(mix_pool):
        attn = []
        for h in range(N_HEADS):
            acc = acc_scr[h]
            attn.append((acc[0:V_HEAD, :] * (1.0 / acc[V_HEAD:V_HEAD + 1, :])).T)
        attn_out = (gate * jnp.concatenate(attn, axis=1)).astype(BF16)
        for r in halves:
            mix = mix_pool[r, :] + _dot(attn_out[r, :], w_out_ref[D_POOL:, :])
            out_ref[r, :] = _rms(x_ref[r, :] + mix, final_g_ref[...])

    pool_projection = []
    rise_scr[...] = jnp.full((1, tm), MASK_VALUE, F32)
    _attend(_stale_update, lambda: pool_projection.append(_dot(pool_out, w_out_ref[0:D_POOL, :])))
    mix_pool = pool_projection[0]
    _emit_output(mix_pool)

    @pl.when(jnp.max(rise_scr[...]) >= EXP_HEADROOM)
    def _redo_exactly():
        _attend(_exact_update, lambda: None)
        _emit_output(mix_pool)


def _rope_tables(length):
    half = QK_ROPE // 2
    inv_freq = 1.0 / (ROPE_THETA ** (np.arange(half, dtype=np.float64) / half))
    ang = np.arange(length, dtype=np.float64)[:, None] * inv_freq[None, :]
    return np.cos(ang), np.sin(ang)


def kernel(x, meta_tokens, norm_g, w_in, q_norm_g, w_q_b, kv_norm_g, w_kv_b, pool_w, pool_scale, w_out, final_norm_g):
    batch, seq, d_model = x.shape
    assert d_model == D_MODEL and seq % ROW_TILE == 0 and meta_tokens.shape == (N_META, D_MODEL)
    assert norm_g.shape[0] == 1 and w_in.shape == (1, D_MODEL, D_IN), "single-layer block"
    tm = ROW_TILE

    cos, sin = _rope_tables(N_META + seq)
    kcos = np.tile(cos, (1, LANES // (QK_ROPE // 2))).astype(np.float32)
    ksin = np.tile(np.concatenate([-sin, sin], axis=1), (1, LANES // QK_ROPE)).astype(np.float32)
    qcos = np.ascontiguousarray(cos[N_META:].T).astype(np.float32)
    qsin = np.ascontiguousarray(sin[N_META:].T).astype(np.float32)

    def whole(a):
        return pl.BlockSpec(a.shape, lambda b, i: (0,) * a.ndim)

    def fetched_once(a, squeeze_layer):
        shape = ((None,) + a.shape[1:]) if squeeze_layer else a.shape
        return pl.BlockSpec(shape, lambda b, i: (0,) * a.ndim, pipeline_mode=pl.Buffered(1))

    row_spec = pl.BlockSpec((None, tm, D_MODEL), lambda b, i: (b, i, 0))
    ktable_spec = pl.BlockSpec((tm, LANES), lambda b, i: (i, 0))
    qtable_spec = pl.BlockSpec((QK_ROPE // 2, tm), lambda b, i: (0, i))

    w_in_t = jnp.transpose(w_in[0])

    operands = [
        (x, row_spec),
        (meta_tokens, None),
        (kcos[N_META:], ktable_spec),
        (ksin[N_META:], ktable_spec),
        (kcos[:N_META], None),
        (ksin[:N_META], None),
        (qcos, qtable_spec),
        (qsin, qtable_spec),
        (norm_g, None),
        (w_in_t, fetched_once(w_in_t, False)),
        (q_norm_g, None), (w_q_b, fetched_once(w_q_b, True)),
        (kv_norm_g, None), (w_kv_b, fetched_once(w_kv_b, True)),
        (pool_w, fetched_once(pool_w, True)),
        (pool_scale, None),
        (w_out, fetched_once(w_out, True)),
        (final_norm_g.reshape(1, D_MODEL), None),
    ]
    args = [a for a, _ in operands]
    in_specs = [whole(a) if s is None else s for a, s in operands]

    return pl.pallas_call(
        _fused_kernel,
        out_shape=jax.ShapeDtypeStruct(x.shape, x.dtype),
        grid=(batch, seq // tm),
        in_specs=in_specs,
        out_specs=row_spec,
        scratch_shapes=[
            pltpu.VMEM((D_MODEL, D_IN_PADDED), BF16),
            pltpu.VMEM((N_HEADS * (QK_NOPE + QK_ROPE), Q_LORA), BF16),
            pltpu.VMEM((KV_LORA, N_HEADS * QK_NOPE), BF16),
            pltpu.VMEM((N_HEADS * V_HEAD, KV_LORA), BF16),
            pltpu.VMEM((len(POOL_WINDOWS), POOL_GROUP, POOL_GROUP), BF16),
            pltpu.VMEM((D_MODEL, D_MODEL), BF16),
            pltpu.VMEM((N_HEADS, seq, QK_PAD), BF16),
            pltpu.VMEM((N_HEADS, V_AUG, seq), BF16),
            pltpu.VMEM((N_HEADS, N_META, QK_PAD), BF16),
            pltpu.VMEM((N_HEADS, V_AUG, N_META), BF16),
            pltpu.VMEM((HALO + tm, D_POOL), F32),
            pltpu.VMEM((HALO, D_POOL), F32),
            pltpu.VMEM((N_HEADS, QK_PAD, tm), BF16),
            pltpu.VMEM((tm, tm), F32),
            pltpu.VMEM((N_HEADS, 1, tm), F32),
            pltpu.VMEM((N_HEADS, V_AUG, tm), F32),
            pltpu.VMEM((1, tm), F32),
        ],
        compiler_params=pltpu.CompilerParams(
            dimension_semantics=("arbitrary", "arbitrary"),
            vmem_limit_bytes=VMEM_LIMIT_BYTES,
        ),
        name="hybrid_pool_mla_block",
    )(*args)
```

```python
import functools
import math

import numpy as np
import jax
import jax.numpy as jnp
from jax import lax
from jax.experimental import pallas as pl
from jax.experimental.pallas import tpu as pltpu

D_MODEL = 1024
N_META = 16
CHUNK = 64
D_POOL = 512
POOL_WINDOWS = (2, 4, 8, 16)
POOL_GROUP = 128
N_HEADS = 4
QK_NOPE = 128
QK_ROPE = 64
V_HEAD = 128
D_ATTN = N_HEADS * V_HEAD
Q_LORA = 256
KV_LORA = 128
ROPE_THETA = 10000.0
EPS = 1e-6
Q_SCALE = (QK_NOPE + QK_ROPE) ** -0.5 * math.log2(math.e)

O_CQ = 2 * D_POOL
O_CKV = O_CQ + Q_LORA
O_KR = O_CKV + KV_LORA
O_AG = O_KR + QK_ROPE
D_IN = O_AG + D_ATTN
S_AG = O_AG + QK_ROPE
D_IN_PADDED = S_AG + D_ATTN

LANES = 128
BF16_ROWS = 16
QK_PAD = 2 * LANES
V_AUG = V_HEAD + BF16_ROWS
HALO = 16
ROW_TILE = 512
EXP_HEADROOM = 64.0
MASK_VALUE = -1e30
VMEM_LIMIT_BYTES = 56 * 1024 * 1024

F32 = jnp.float32
BF16 = jnp.bfloat16


def _dot(a, b):
    return jnp.dot(a, b, preferred_element_type=F32)


def _dot_nt(a, b):
    return lax.dot_general(a, b, (((1,), (1,)), ((), ())), preferred_element_type=F32)


def _rms(x, g):
    return x * lax.rsqrt(jnp.mean(x * x, axis=-1, keepdims=True) + EPS) * g


def _silu(x):
    h = 0.5 * x
    return h + h * jnp.tanh(h)


def _rope_lanes(x, cos_t, sin_t):
    lane = lax.broadcasted_iota(jnp.int32, x.shape, 1)
    first_half = (lane % QK_ROPE) < (QK_ROPE // 2)
    partner = jnp.where(first_half,
                        pltpu.roll(x, LANES - QK_ROPE // 2, axis=1),
                        pltpu.roll(x, QK_ROPE // 2, axis=1))
    return x * cos_t + partner * sin_t


def _latent_kv(ckvr, kv_g_ref, w_kb_ref, w_vbt_ref, cos_t, sin_t):
    ckv_n = _rms(ckvr[:, :KV_LORA], kv_g_ref[...]).astype(BF16)
    k_nope = _dot(ckv_n, w_kb_ref[...])
    v_t = _dot_nt(w_vbt_ref[...], ckv_n)
    k_rope = _rope_lanes(ckvr[:, KV_LORA:], cos_t, sin_t)
    return k_nope, v_t, k_rope


def _window_sum(xg, w):
    y = xg
    span = 1
    while span < w:
        y = y + pltpu.roll(y, span, axis=0)
        span *= 2
    return y


def _denominator_rows(n):
    return (lax.broadcasted_iota(jnp.int32, (BF16_ROWS, n), 0) == 0).astype(BF16)


def _fused_kernel(tiles_per_seq, x_ref, meta_ref, kcos_ref, ksin_ref, mcos_ref, msin_ref, qcos_ref, qsin_ref,
                  norm_g_ref, w_in_ref, q_g_ref, w_qb_ref, kv_g_ref, w_kvb_ref,
                  pool_w_f32_ref, pool_scale_ref, w_out_f32_ref, final_g_ref,
                  out_ref,
                  w_in_s, w_qbt_ref, w_kb_ref, w_vbt_ref, pool_w_ref, w_out_ref,
                  k_scr, vt_scr, kmeta_scr, vtmeta_scr, pool_scr, halo_scr,
                  qt_scr, s_scr, m_scr, acc_scr, rise_scr):
    b = pl.program_id(0)
    i = pl.program_id(1)
    tm = ROW_TILE
    first_step = (b == 0) & (i == 0)

    w_pi_ref = w_in_s.at[:, 0:D_POOL]
    w_pg_ref = w_in_s.at[:, D_POOL:O_CQ]
    w_cq_ref = w_in_s.at[:, O_CQ:O_CKV]
    w_ckv_ref = w_in_s.at[:, O_CKV:S_AG]
    w_ag_ref = w_in_s.at[:, S_AG:S_AG + D_ATTN]

    @pl.when(first_step)
    def _prepare_weights():
        for c in range(O_KR // LANES):
            w_in_s[:, c * LANES:(c + 1) * LANES] = w_in_ref[c * LANES:(c + 1) * LANES, :].T.astype(BF16)
        k_rope_rows = w_in_ref[O_KR:O_AG, :]
        w_in_s[:, O_KR:S_AG] = jnp.concatenate([k_rope_rows, k_rope_rows], axis=0).T.astype(BF16)
        for c in range(D_ATTN // LANES):
            w_in_s[:, S_AG + c * LANES:S_AG + (c + 1) * LANES] = (
                w_in_ref[O_AG + c * LANES:O_AG + (c + 1) * LANES, :].T.astype(BF16))
        quarter = D_MODEL // 4
        for r in range(4):
            rs = slice(r * quarter, (r + 1) * quarter)
            w_out_ref[rs, :] = w_out_f32_ref[rs, :].astype(BF16)
        w_qbt_ref[...] = w_qb_ref[...].T.astype(BF16)
        for h in range(N_HEADS):
            base = h * (QK_NOPE + V_HEAD)
            w_kb_ref[:, h * QK_NOPE:(h + 1) * QK_NOPE] = w_kvb_ref[:, base:base + QK_NOPE].astype(BF16)
            w_vbt_ref[h * V_HEAD:(h + 1) * V_HEAD, :] = w_kvb_ref[:, base + QK_NOPE:base + QK_NOPE + V_HEAD].T.astype(BF16)
        pool_w_ref[...] = pool_w_f32_ref[...].astype(BF16)

    @pl.when(first_step)
    def _meta_prologue():
        xn = _rms(meta_ref[...], norm_g_ref[...]).astype(BF16)
        halo_scr[...] = _dot(xn, w_pi_ref[...])
        k_nope, v_t, k_rope = _latent_kv(_dot(xn, w_ckv_ref[...]), kv_g_ref, w_kb_ref, w_vbt_ref,
                                         mcos_ref[...], msin_ref[...])
        for h in range(N_HEADS):
            kmeta_scr[h, :, 0:LANES] = k_nope[:, h * QK_NOPE:(h + 1) * QK_NOPE].astype(BF16)
            kmeta_scr[h, :, LANES:QK_PAD] = k_rope.astype(BF16)
            vtmeta_scr[h, 0:V_HEAD, :] = v_t[h * V_HEAD:(h + 1) * V_HEAD, :].astype(BF16)
            vtmeta_scr[h, V_HEAD:V_AUG, :] = _denominator_rows(N_META)

    @pl.when(i == 0)
    def _seed_halo():
        pool_scr[0:HALO, :] = halo_scr[...]

    hm = tm // 2
    halves = (slice(0, hm), slice(hm, tm))
    quarters = tuple(slice(r * (tm // 4), (r + 1) * (tm // 4)) for r in range(4))
    xn_halves = [_rms(x_ref[r, :], norm_g_ref[...]).astype(BF16) for r in halves]

    for r, xn_half in zip(halves, xn_halves):
        pool_scr[HALO + r.start:HALO + r.stop, :] = _dot(xn_half, w_pi_ref[...])
    xn = jnp.concatenate(xn_halves, axis=0)
    cq = _dot(xn, w_cq_ref[...])
    ckvr = _dot(xn, w_ckv_ref[...])
    pool_gate = _dot(xn, w_pg_ref[...])

    cq_n = _rms(cq, q_g_ref[...]).astype(BF16)
    q_t = _dot_nt(w_qbt_ref[...], cq_n) * Q_SCALE
    k_nope, v_t, k_rope = _latent_kv(ckvr, kv_g_ref, w_kb_ref, w_vbt_ref, kcos_ref[...], ksin_ref[...])
    attn_gate = _dot(xn, w_ag_ref[...])

    pooled = []
    for g, w in enumerate(POOL_WINDOWS):
        xg = pool_scr[:, g * POOL_GROUP:(g + 1) * POOL_GROUP]
        win = _window_sum(xg, w)
        pooled.append((win[HALO:, :] * (1.0 / w) - xg[HALO:, :]).astype(BF16))
    pool_scr[0:HALO, :] = pool_scr[tm:tm + HALO, :]

    row0 = pl.multiple_of(i * tm, tm)
    tile_rows = pl.ds(row0, tm)
    k_rope_bf = k_rope.astype(BF16)
    denominator_rows = _denominator_rows(tm)
    for h in range(N_HEADS):
        k_scr[h, tile_rows, 0:LANES] = k_nope[:, h * QK_NOPE:(h + 1) * QK_NOPE].astype(BF16)
        k_scr[h, tile_rows, LANES:QK_PAD] = k_rope_bf
        vt_scr[h, 0:V_HEAD, tile_rows] = v_t[h * V_HEAD:(h + 1) * V_HEAD, :].astype(BF16)
        vt_scr[h, V_HEAD:V_AUG, tile_rows] = denominator_rows

    cos_q = qcos_ref[...]
    sin_q = qsin_ref[...]
    half = QK_ROPE // 2
    for h in range(N_HEADS):
        base = h * (QK_NOPE + QK_ROPE)
        x1 = q_t[base + QK_NOPE:base + QK_NOPE + half, :]
        x2 = q_t[base + QK_NOPE + half:base + QK_NOPE + QK_ROPE, :]
        qt_scr[h, 0:QK_NOPE, :] = q_t[base:base + QK_NOPE, :].astype(BF16)
        qt_scr[h, QK_NOPE:QK_NOPE + half, :] = (x1 * cos_q - x2 * sin_q).astype(BF16)
        qt_scr[h, QK_NOPE + half:QK_NOPE + QK_ROPE, :] = (x1 * sin_q + x2 * cos_q).astype(BF16)
        qt_scr[h, QK_NOPE + QK_ROPE:QK_PAD, :] = jnp.zeros((QK_PAD - QK_NOPE - QK_ROPE, tm), BF16)

    mixed = [_dot(pooled[g], pool_w_ref[g]) for g in range(len(POOL_WINDOWS))]
    pool_out = (_silu(pool_gate) * (jnp.concatenate(mixed, axis=1) * pool_scale_ref[...])).astype(BF16)

    gate = _silu(attn_gate)

    def _scores(h, rows):
        return _dot(k_scr[h, rows, :], qt_scr[h])

    def _exact_update(h, blocks):
        m_prev = m_scr[h]
        m_new = m_prev
        for s, _ in blocks:
            m_new = jnp.maximum(m_new, jnp.max(s, axis=0, keepdims=True))
        acc = jnp.exp2(m_prev - m_new) * acc_scr[h]
        for s, vt_blk in blocks:
            acc = acc + _dot(vt_blk, jnp.exp2(s - m_new).astype(BF16))
        acc_scr[h] = acc
        m_scr[h] = m_new

    def _stale_update(h, blocks):
        m_ref = m_scr[h]
        acc = acc_scr[h]
        rise = rise_scr[...]
        lift = jnp.zeros_like(m_ref)
        for s, vt_blk in blocks:
            t = s - m_ref
            top = jnp.max(t, axis=0, keepdims=True)
            acc = acc + _dot(vt_blk, jnp.exp2(t).astype(BF16))
            rise = jnp.maximum(rise, top)
            lift = jnp.maximum(lift, top)
        acc_scr[h] = acc * jnp.exp2(-lift)
        m_scr[h] = m_ref + lift
        rise_scr[...] = rise

    key_chunk = lax.broadcasted_iota(jnp.int32, (tm, tm), 0) // CHUNK
    query_chunk = lax.broadcasted_iota(jnp.int32, (tm, tm), 1) // CHUNK
    diag_valid = key_chunk <= query_chunk

    def _start_from_meta_keys():
        s_meta = [_dot(kmeta_scr[h], qt_scr[h]) for h in range(N_HEADS)]
        for h in range(N_HEADS):
            m_scr[h] = jnp.max(s_meta[h], axis=0, keepdims=True)
            acc_scr[h] = jnp.zeros((V_AUG, tm), F32)
        return s_meta

    def _attend_unrolled(n_earlier, update):
        s_meta = _start_from_meta_keys()
        own_rows = pl.ds(n_earlier * tm, tm)
        units = ([(None, h) for h in range(N_HEADS)]
                 + [(j, h) for j in range(n_earlier) for h in range(N_HEADS)])

        def unit_scores(unit):
            j, h = unit
            if j is None:
                return jnp.where(diag_valid, _scores(h, own_rows), MASK_VALUE)
            return _scores(h, pl.ds(j * tm, tm))

        s_next = unit_scores(units[0])
        for n, (j, h) in enumerate(units):
            s = s_next
            if n + 1 < len(units):
                s_next = unit_scores(units[n + 1])
            if j is None:
                update(h, [(s_meta[h], vtmeta_scr[h]), (s, vt_scr[h, :, own_rows])])
            else:
                update(h, [(s, vt_scr[h, :, pl.ds(j * tm, tm)])])

    def _attend_looped(update):
        s_meta = _start_from_meta_keys()
        s_next = jnp.where(diag_valid, _scores(0, tile_rows), MASK_VALUE)
        for h in range(N_HEADS):
            s = s_next
            if h + 1 < N_HEADS:
                s_next = jnp.where(diag_valid, _scores(h + 1, tile_rows), MASK_VALUE)
            else:
                s_scr[...] = _scores(0, pl.ds(0, tm))
            update(h, [(s_meta[h], vtmeta_scr[h]), (s, vt_scr[h, :, tile_rows])])

        def _earlier_tile(j, carry):
            rows = pl.ds(pl.multiple_of(j * tm, tm), tm)
            next_rows = pl.ds(pl.multiple_of((j + 1) * tm, tm), tm)
            s_next = s_scr[...]
            for h in range(N_HEADS):
                s = s_next
                if h + 1 < N_HEADS:
                    s_next = _scores(h + 1, rows)
                else:
                    s_scr[...] = _scores(0, next_rows)
                update(h, [(s, vt_scr[h, :, rows])])
            return carry

        lax.fori_loop(0, i, _earlier_tile, 0)

    mix_pool = _dot(pool_out, w_out_ref[0:D_POOL, :])

    def _emit_output():
        attn = []
        for h in range(N_HEADS):
            acc = acc_scr[h]
            attn.append((acc[0:V_HEAD, :] * (1.0 / acc[V_HEAD:V_HEAD + 1, :])).T)
        attn_out = (gate * jnp.concatenate(attn, axis=1)).astype(BF16)
        for r in halves:
            mix = mix_pool[r, :] + _dot(attn_out[r, :], w_out_ref[D_POOL:, :])
            out_ref[r, :] = _rms(x_ref[r, :] + mix, final_g_ref[...])

    rise_scr[...] = jnp.full((1, tm), MASK_VALUE, F32)
    for n_earlier in range(tiles_per_seq):
        @pl.when(i == n_earlier)
        def _tile_at(n_earlier=n_earlier):
            _attend_unrolled(n_earlier, _stale_update)
            _emit_output()

    @pl.when(jnp.max(rise_scr[...]) >= EXP_HEADROOM)
    def _redo_exactly():
        _attend_looped(_exact_update)
        _emit_output()


def _rope_tables(length):
    half = QK_ROPE // 2
    inv_freq = 1.0 / (ROPE_THETA ** (np.arange(half, dtype=np.float64) / half))
    ang = np.arange(length, dtype=np.float64)[:, None] * inv_freq[None, :]
    return np.cos(ang), np.sin(ang)


def kernel(x, meta_tokens, norm_g, w_in, q_norm_g, w_q_b, kv_norm_g, w_kv_b, pool_w, pool_scale, w_out, final_norm_g):
    batch, seq, d_model = x.shape
    assert d_model == D_MODEL and seq % ROW_TILE == 0 and meta_tokens.shape == (N_META, D_MODEL)
    assert norm_g.shape[0] == 1 and w_in.shape == (1, D_MODEL, D_IN), "single-layer block"
    tm = ROW_TILE

    cos, sin = _rope_tables(N_META + seq)
    kcos = np.tile(cos, (1, LANES // (QK_ROPE // 2))).astype(np.float32)
    ksin = np.tile(np.concatenate([-sin, sin], axis=1), (1, LANES // QK_ROPE)).astype(np.float32)
    qcos = np.ascontiguousarray(cos[N_META:].T).astype(np.float32)
    qsin = np.ascontiguousarray(sin[N_META:].T).astype(np.float32)

    def whole(a):
        return pl.BlockSpec(a.shape, lambda b, i: (0,) * a.ndim)

    def fetched_once(a, squeeze_layer):
        shape = ((None,) + a.shape[1:]) if squeeze_layer else a.shape
        return pl.BlockSpec(shape, lambda b, i: (0,) * a.ndim, pipeline_mode=pl.Buffered(1))

    row_spec = pl.BlockSpec((None, tm, D_MODEL), lambda b, i: (b, i, 0))
    ktable_spec = pl.BlockSpec((tm, LANES), lambda b, i: (i, 0))
    qtable_spec = pl.BlockSpec((QK_ROPE // 2, tm), lambda b, i: (0, i))

    w_in_t = jnp.transpose(w_in[0])

    operands = [
        (x, row_spec),
        (meta_tokens, None),
        (kcos[N_META:], ktable_spec),
        (ksin[N_META:], ktable_spec),
        (kcos[:N_META], None),
        (ksin[:N_META], None),
        (qcos, qtable_spec),
        (qsin, qtable_spec),
        (norm_g, None),
        (w_in_t, fetched_once(w_in_t, False)),
        (q_norm_g, None), (w_q_b, fetched_once(w_q_b, True)),
        (kv_norm_g, None), (w_kv_b, fetched_once(w_kv_b, True)),
        (pool_w, fetched_once(pool_w, True)),
        (pool_scale, None),
        (w_out, fetched_once(w_out, True)),
        (final_norm_g.reshape(1, D_MODEL), None),
    ]
    args = [a for a, _ in operands]
    in_specs = [whole(a) if s is None else s for a, s in operands]

    return pl.pallas_call(
        functools.partial(_fused_kernel, seq // tm),
        out_shape=jax.ShapeDtypeStruct(x.shape, x.dtype),
        grid=(batch, seq // tm),
        in_specs=in_specs,
        out_specs=row_spec,
        scratch_shapes=[
            pltpu.VMEM((D_MODEL, D_IN_PADDED), BF16),
            pltpu.VMEM((N_HEADS * (QK_NOPE + QK_ROPE), Q_LORA), BF16),
            pltpu.VMEM((KV_LORA, N_HEADS * QK_NOPE), BF16),
            pltpu.VMEM((N_HEADS * V_HEAD, KV_LORA), BF16),
            pltpu.VMEM((len(POOL_WINDOWS), POOL_GROUP, POOL_GROUP), BF16),
            pltpu.VMEM((D_MODEL, D_MODEL), BF16),
            pltpu.VMEM((N_HEADS, seq, QK_PAD), BF16),
            pltpu.VMEM((N_HEADS, V_AUG, seq), BF16),
            pltpu.VMEM((N_HEADS, N_META, QK_PAD), BF16),
            pltpu.VMEM((N_HEADS, V_AUG, N_META), BF16),
            pltpu.VMEM((HALO + tm, D_POOL), F32),
            pltpu.VMEM((HALO, D_POOL), F32),
            pltpu.VMEM((N_HEADS, QK_PAD, tm), BF16),
            pltpu.VMEM((tm, tm), F32),
            pltpu.VMEM((N_HEADS, 1, tm), F32),
            pltpu.VMEM((N_HEADS, V_AUG, tm), F32),
            pltpu.VMEM((1, tm), F32),
        ],
        compiler_params=pltpu.CompilerParams(
            dimension_semantics=("arbitrary", "arbitrary"),
            vmem_limit_bytes=VMEM_LIMIT_BYTES,
        ),
        name="hybrid_pool_mla_block",
    )(*args)
```

```python
import functools
import math

import numpy as np
import jax
import jax.numpy as jnp
from jax import lax
from jax.experimental import pallas as pl
from jax.experimental.pallas import tpu as pltpu

D_MODEL = 1024
N_META = 16
CHUNK = 64
D_POOL = 512
POOL_WINDOWS = (2, 4, 8, 16)
POOL_GROUP = 128
N_HEADS = 4
QK_NOPE = 128
QK_ROPE = 64
V_HEAD = 128
D_ATTN = N_HEADS * V_HEAD
Q_LORA = 256
KV_LORA = 128
ROPE_THETA = 10000.0
EPS = 1e-6
Q_SCALE = (QK_NOPE + QK_ROPE) ** -0.5 * math.log2(math.e)

O_CQ = 2 * D_POOL
O_CKV = O_CQ + Q_LORA
O_KR = O_CKV + KV_LORA
O_AG = O_KR + QK_ROPE
D_IN = O_AG + D_ATTN
S_AG = O_AG + QK_ROPE
D_IN_PADDED = S_AG + D_ATTN

LANES = 128
BF16_ROWS = 16
QK_PAD = 2 * LANES
V_AUG = V_HEAD + BF16_ROWS
HALO = 16
ROW_TILE = 512
EXP_HEADROOM = 64.0
MASK_VALUE = -1e30
VMEM_LIMIT_BYTES = 56 * 1024 * 1024

F32 = jnp.float32
BF16 = jnp.bfloat16


def _dot(a, b):
    return jnp.dot(a, b, preferred_element_type=F32)


def _dot_nt(a, b):
    return lax.dot_general(a, b, (((1,), (1,)), ((), ())), preferred_element_type=F32)


def _rms(x, g):
    return x * lax.rsqrt(jnp.mean(x * x, axis=-1, keepdims=True) + EPS) * g


def _silu(x):
    h = 0.5 * x
    return h + h * jnp.tanh(h)


def _rope_lanes(x, cos_t, sin_t):
    lane = lax.broadcasted_iota(jnp.int32, x.shape, 1)
    first_half = (lane % QK_ROPE) < (QK_ROPE // 2)
    partner = jnp.where(first_half,
                        pltpu.roll(x, LANES - QK_ROPE // 2, axis=1),
                        pltpu.roll(x, QK_ROPE // 2, axis=1))
    return x * cos_t + partner * sin_t


def _latent_kv(ckvr, kv_g_ref, w_kb_ref, w_vbt_ref, cos_t, sin_t):
    ckv_n = _rms(ckvr[:, :KV_LORA], kv_g_ref[...]).astype(BF16)
    k_nope = _dot(ckv_n, w_kb_ref[...])
    v_t = _dot_nt(w_vbt_ref[...], ckv_n)
    k_rope = _rope_lanes(ckvr[:, KV_LORA:], cos_t, sin_t)
    return k_nope, v_t, k_rope


def _window_sum(xg, w):
    y = xg
    span = 1
    while span < w:
        y = y + pltpu.roll(y, span, axis=0)
        span *= 2
    return y


def _denominator_rows(n):
    return (lax.broadcasted_iota(jnp.int32, (BF16_ROWS, n), 0) == 0).astype(BF16)


def _fused_kernel(tiles_per_seq, x_ref, meta_ref, kcos_ref, ksin_ref, mcos_ref, msin_ref, qcos_ref, qsin_ref,
                  norm_g_ref, w_in_ref, q_g_ref, w_qb_ref, kv_g_ref, w_kvb_ref,
                  pool_w_f32_ref, pool_scale_ref, w_out_f32_ref, final_g_ref,
                  out_ref,
                  w_in_s, w_qbt_ref, w_kb_ref, w_vbt_ref, pool_w_ref, w_out_ref,
                  k_scr, vt_scr, kmeta_scr, vtmeta_scr, pool_scr, halo_scr,
                  qt_scr, s_scr, m_scr, acc_scr, rise_scr):
    b = pl.program_id(0)
    i = pl.program_id(1)
    tm = ROW_TILE
    first_step = (b == 0) & (i == 0)

    w_pi_ref = w_in_s.at[:, 0:D_POOL]
    w_pg_ref = w_in_s.at[:, D_POOL:O_CQ]
    w_cq_ref = w_in_s.at[:, O_CQ:O_CKV]
    w_ckv_ref = w_in_s.at[:, O_CKV:S_AG]
    w_ag_ref = w_in_s.at[:, S_AG:S_AG + D_ATTN]

    @pl.when(first_step)
    def _prepare_weights():
        for c in range(O_KR // LANES):
            w_in_s[:, c * LANES:(c + 1) * LANES] = w_in_ref[c * LANES:(c + 1) * LANES, :].T.astype(BF16)
        k_rope_rows = w_in_ref[O_KR:O_AG, :]
        w_in_s[:, O_KR:S_AG] = jnp.concatenate([k_rope_rows, k_rope_rows], axis=0).T.astype(BF16)
        for c in range(D_ATTN // LANES):
            w_in_s[:, S_AG + c * LANES:S_AG + (c + 1) * LANES] = (
                w_in_ref[O_AG + c * LANES:O_AG + (c + 1) * LANES, :].T.astype(BF16))
        quarter = D_MODEL // 4
        for r in range(4):
            rs = slice(r * quarter, (r + 1) * quarter)
            w_out_ref[rs, :] = w_out_f32_ref[rs, :].astype(BF16)
        w_qbt_ref[...] = w_qb_ref[...].T.astype(BF16)
        for h in range(N_HEADS):
            base = h * (QK_NOPE + V_HEAD)
            w_kb_ref[:, h * QK_NOPE:(h + 1) * QK_NOPE] = w_kvb_ref[:, base:base + QK_NOPE].astype(BF16)
            w_vbt_ref[h * V_HEAD:(h + 1) * V_HEAD, :] = w_kvb_ref[:, base + QK_NOPE:base + QK_NOPE + V_HEAD].T.astype(BF16)
        pool_w_ref[...] = pool_w_f32_ref[...].astype(BF16)

    @pl.when(first_step)
    def _meta_prologue():
        xn = _rms(meta_ref[...], norm_g_ref[...]).astype(BF16)
        halo_scr[...] = _dot(xn, w_pi_ref[...])
        k_nope, v_t, k_rope = _latent_kv(_dot(xn, w_ckv_ref[...]), kv_g_ref, w_kb_ref, w_vbt_ref,
                                         mcos_ref[...], msin_ref[...])
        for h in range(N_HEADS):
            kmeta_scr[h, :, 0:LANES] = k_nope[:, h * QK_NOPE:(h + 1) * QK_NOPE].astype(BF16)
            kmeta_scr[h, :, LANES:QK_PAD] = k_rope.astype(BF16)
            vtmeta_scr[h, 0:V_HEAD, :] = v_t[h * V_HEAD:(h + 1) * V_HEAD, :].astype(BF16)
            vtmeta_scr[h, V_HEAD:V_AUG, :] = _denominator_rows(N_META)

    @pl.when(i == 0)
    def _seed_halo():
        pool_scr[0:HALO, :] = halo_scr[...]

    hm = tm // 2
    halves = (slice(0, hm), slice(hm, tm))
    quarters = tuple(slice(r * (tm // 4), (r + 1) * (tm // 4)) for r in range(4))
    xn_halves = [_rms(x_ref[r, :], norm_g_ref[...]).astype(BF16) for r in halves]

    for r, xn_half in zip(halves, xn_halves):
        pool_scr[HALO + r.start:HALO + r.stop, :] = _dot(xn_half, w_pi_ref[...])
    xn = jnp.concatenate(xn_halves, axis=0)
    cq = _dot(xn, w_cq_ref[...])
    ckvr = _dot(xn, w_ckv_ref[...])
    pool_gate = _dot(xn, w_pg_ref[...])

    cq_n = _rms(cq, q_g_ref[...]).astype(BF16)
    q_t = _dot_nt(w_qbt_ref[...], cq_n) * Q_SCALE
    k_nope, v_t, k_rope = _latent_kv(ckvr, kv_g_ref, w_kb_ref, w_vbt_ref, kcos_ref[...], ksin_ref[...])

    pooled = []
    for g, w in enumerate(POOL_WINDOWS):
        xg = pool_scr[:, g * POOL_GROUP:(g + 1) * POOL_GROUP]
        win = _window_sum(xg, w)
        pooled.append((win[HALO:, :] * (1.0 / w) - xg[HALO:, :]).astype(BF16))
    pool_scr[0:HALO, :] = pool_scr[tm:tm + HALO, :]

    row0 = pl.multiple_of(i * tm, tm)
    tile_rows = pl.ds(row0, tm)
    k_rope_bf = k_rope.astype(BF16)
    denominator_rows = _denominator_rows(tm)
    for h in range(N_HEADS):
        k_scr[h, tile_rows, 0:LANES] = k_nope[:, h * QK_NOPE:(h + 1) * QK_NOPE].astype(BF16)
        k_scr[h, tile_rows, LANES:QK_PAD] = k_rope_bf
        vt_scr[h, 0:V_HEAD, tile_rows] = v_t[h * V_HEAD:(h + 1) * V_HEAD, :].astype(BF16)
        vt_scr[h, V_HEAD:V_AUG, tile_rows] = denominator_rows

    cos_q = qcos_ref[...]
    sin_q = qsin_ref[...]
    half = QK_ROPE // 2
    for h in range(N_HEADS):
        base = h * (QK_NOPE + QK_ROPE)
        x1 = q_t[base + QK_NOPE:base + QK_NOPE + half, :]
        x2 = q_t[base + QK_NOPE + half:base + QK_NOPE + QK_ROPE, :]
        qt_scr[h, 0:QK_NOPE, :] = q_t[base:base + QK_NOPE, :].astype(BF16)
        qt_scr[h, QK_NOPE:QK_NOPE + half, :] = (x1 * cos_q - x2 * sin_q).astype(BF16)
        qt_scr[h, QK_NOPE + half:QK_NOPE + QK_ROPE, :] = (x1 * sin_q + x2 * cos_q).astype(BF16)
        qt_scr[h, QK_NOPE + QK_ROPE:QK_PAD, :] = jnp.zeros((QK_PAD - QK_NOPE - QK_ROPE, tm), BF16)

    mixed = [_dot(pooled[g], pool_w_ref[g]) for g in range(len(POOL_WINDOWS))]
    pool_out = (_silu(pool_gate) * (jnp.concatenate(mixed, axis=1) * pool_scale_ref[...])).astype(BF16)

    def _scores(h, rows):
        return _dot(k_scr[h, rows, :], qt_scr[h])

    def _exact_update(h, blocks):
        m_prev = m_scr[h]
        m_new = m_prev
        for s, _ in blocks:
            m_new = jnp.maximum(m_new, jnp.max(s, axis=0, keepdims=True))
        acc = jnp.exp2(m_prev - m_new) * acc_scr[h]
        for s, vt_blk in blocks:
            acc = acc + _dot(vt_blk, jnp.exp2(s - m_new).astype(BF16))
        acc_scr[h] = acc
        m_scr[h] = m_new

    def _stale_update(h, blocks):
        m_ref = m_scr[h]
        acc = acc_scr[h]
        rise = rise_scr[...]
        lift = jnp.zeros_like(m_ref)
        for s, vt_blk in blocks:
            t = s - m_ref
            top = jnp.max(t, axis=0, keepdims=True)
            acc = acc + _dot(vt_blk, jnp.exp2(t).astype(BF16))
            rise = jnp.maximum(rise, top)
            lift = jnp.maximum(lift, top)
        acc_scr[h] = acc * jnp.exp2(-lift)
        m_scr[h] = m_ref + lift
        rise_scr[...] = rise

    key_chunk = lax.broadcasted_iota(jnp.int32, (tm, tm), 0) // CHUNK
    query_chunk = lax.broadcasted_iota(jnp.int32, (tm, tm), 1) // CHUNK
    diag_valid = key_chunk <= query_chunk

    def _start_from_meta_keys():
        s_meta = [_dot(kmeta_scr[h], qt_scr[h]) for h in range(N_HEADS)]
        for h in range(N_HEADS):
            m_scr[h] = jnp.max(s_meta[h], axis=0, keepdims=True)
            acc_scr[h] = jnp.zeros((V_AUG, tm), F32)
        return s_meta

    def _attend_unrolled(n_earlier, update):
        s_meta = _start_from_meta_keys()
        own_rows = pl.ds(n_earlier * tm, tm)
        units = ([(None, h) for h in range(N_HEADS)]
                 + [(j, h) for j in range(n_earlier) for h in range(N_HEADS)])

        def unit_scores(unit):
            j, h = unit
            if j is None:
                return jnp.where(diag_valid, _scores(h, own_rows), MASK_VALUE)
            return _scores(h, pl.ds(j * tm, tm))

        s_next = unit_scores(units[0])
        for n, (j, h) in enumerate(units):
            s = s_next
            if n + 1 < len(units):
                s_next = unit_scores(units[n + 1])
            if n == 0:
                attn_gate = _dot(xn, w_ag_ref[...])
            if j is None:
                update(h, [(s_meta[h], vtmeta_scr[h]), (s, vt_scr[h, :, own_rows])])
            else:
                update(h, [(s, vt_scr[h, :, pl.ds(j * tm, tm)])])
        return attn_gate

    def _attend_looped(update):
        s_meta = _start_from_meta_keys()
        s_next = jnp.where(diag_valid, _scores(0, tile_rows), MASK_VALUE)
        for h in range(N_HEADS):
            s = s_next
            if h + 1 < N_HEADS:
                s_next = jnp.where(diag_valid, _scores(h + 1, tile_rows), MASK_VALUE)
            else:
                s_scr[...] = _scores(0, pl.ds(0, tm))
            update(h, [(s_meta[h], vtmeta_scr[h]), (s, vt_scr[h, :, tile_rows])])

        def _earlier_tile(j, carry):
            rows = pl.ds(pl.multiple_of(j * tm, tm), tm)
            next_rows = pl.ds(pl.multiple_of((j + 1) * tm, tm), tm)
            s_next = s_scr[...]
            for h in range(N_HEADS):
                s = s_next
                if h + 1 < N_HEADS:
                    s_next = _scores(h + 1, rows)
                else:
                    s_scr[...] = _scores(0, next_rows)
                update(h, [(s, vt_scr[h, :, rows])])
            return carry

        lax.fori_loop(0, i, _earlier_tile, 0)

    def _emit_output(attn_gate):
        mix_pool = _dot(pool_out, w_out_ref[0:D_POOL, :])
        attn = []
        for h in range(N_HEADS):
            acc = acc_scr[h]
            attn.append((acc[0:V_HEAD, :] * (1.0 / acc[V_HEAD:V_HEAD + 1, :])).T)
        attn_out = (_silu(attn_gate) * jnp.concatenate(attn, axis=1)).astype(BF16)
        for r in halves:
            mix = mix_pool[r, :] + _dot(attn_out[r, :], w_out_ref[D_POOL:, :])
            out_ref[r, :] = _rms(x_ref[r, :] + mix, final_g_ref[...])

    rise_scr[...] = jnp.full((1, tm), MASK_VALUE, F32)
    for n_earlier in range(tiles_per_seq):
        @pl.when(i == n_earlier)
        def _tile_at(n_earlier=n_earlier):
            _emit_output(_attend_unrolled(n_earlier, _stale_update))

    @pl.when(jnp.max(rise_scr[...]) >= EXP_HEADROOM)
    def _redo_exactly():
        _attend_looped(_exact_update)
        _emit_output(_dot(xn, w_ag_ref[...]))


def _rope_tables(length):
    half = QK_ROPE // 2
    inv_freq = 1.0 / (ROPE_THETA ** (np.arange(half, dtype=np.float64) / half))
    ang = np.arange(length, dtype=np.float64)[:, None] * inv_freq[None, :]
    return np.cos(ang), np.sin(ang)


def kernel(x, meta_tokens, norm_g, w_in, q_norm_g, w_q_b, kv_norm_g, w_kv_b, pool_w, pool_scale, w_out, final_norm_g):
    batch, seq, d_model = x.shape
    assert d_model == D_MODEL and seq % ROW_TILE == 0 and meta_tokens.shape == (N_META, D_MODEL)
    assert norm_g.shape[0] == 1 and w_in.shape == (1, D_MODEL, D_IN), "single-layer block"
    tm = ROW_TILE

    cos, sin = _rope_tables(N_META + seq)
    kcos = np.tile(cos, (1, LANES // (QK_ROPE // 2))).astype(np.float32)
    ksin = np.tile(np.concatenate([-sin, sin], axis=1), (1, LANES // QK_ROPE)).astype(np.float32)
    qcos = np.ascontiguousarray(cos[N_META:].T).astype(np.float32)
    qsin = np.ascontiguousarray(sin[N_META:].T).astype(np.float32)

    def whole(a):
        return pl.BlockSpec(a.shape, lambda b, i: (0,) * a.ndim)

    def fetched_once(a, squeeze_layer):
        shape = ((None,) + a.shape[1:]) if squeeze_layer else a.shape
        return pl.BlockSpec(shape, lambda b, i: (0,) * a.ndim, pipeline_mode=pl.Buffered(1))

    row_spec = pl.BlockSpec((None, tm, D_MODEL), lambda b, i: (b, i, 0))
    ktable_spec = pl.BlockSpec((tm, LANES), lambda b, i: (i, 0))
    qtable_spec = pl.BlockSpec((QK_ROPE // 2, tm), lambda b, i: (0, i))

    w_in_t = jnp.transpose(w_in[0])

    operands = [
        (x, row_spec),
        (meta_tokens, None),
        (kcos[N_META:], ktable_spec),
        (ksin[N_META:], ktable_spec),
        (kcos[:N_META], None),
        (ksin[:N_META], None),
        (qcos, qtable_spec),
        (qsin, qtable_spec),
        (norm_g, None),
        (w_in_t, fetched_once(w_in_t, False)),
        (q_norm_g, None), (w_q_b, fetched_once(w_q_b, True)),
        (kv_norm_g, None), (w_kv_b, fetched_once(w_kv_b, True)),
        (pool_w, fetched_once(pool_w, True)),
        (pool_scale, None),
        (w_out, fetched_once(w_out, True)),
        (final_norm_g.reshape(1, D_MODEL), None),
    ]
    args = [a for a, _ in operands]
    in_specs = [whole(a) if s is None else s for a, s in operands]

    return pl.pallas_call(
        functools.partial(_fused_kernel, seq // tm),
        out_shape=jax.ShapeDtypeStruct(x.shape, x.dtype),
        grid=(batch, seq // tm),
        in_specs=in_specs,
        out_specs=row_spec,
        scratch_shapes=[
            pltpu.VMEM((D_MODEL, D_IN_PADDED), BF16),
            pltpu.VMEM((N_HEADS * (QK_NOPE + QK_ROPE), Q_LORA), BF16),
            pltpu.VMEM((KV_LORA, N_HEADS * QK_NOPE), BF16),
            pltpu.VMEM((N_HEADS * V_HEAD, KV_LORA), BF16),
            pltpu.VMEM((len(POOL_WINDOWS), POOL_GROUP, POOL_GROUP), BF16),
            pltpu.VMEM((D_MODEL, D_MODEL), BF16),
            pltpu.VMEM((N_HEADS, seq, QK_PAD), BF16),
            pltpu.VMEM((N_HEADS, V_AUG, seq), BF16),
            pltpu.VMEM((N_HEADS, N_META, QK_PAD), BF16),
            pltpu.VMEM((N_HEADS, V_AUG, N_META), BF16),
            pltpu.VMEM((HALO + tm, D_POOL), F32),
            pltpu.VMEM((HALO, D_POOL), F32),
            pltpu.VMEM((N_HEADS, QK_PAD, tm), BF16),
            pltpu.VMEM((tm, tm), F32),
            pltpu.VMEM((N_HEADS, 1, tm), F32),
            pltpu.VMEM((N_HEADS, V_AUG, tm), F32),
            pltpu.VMEM((1, tm), F32),
        ],
        compiler_params=pltpu.CompilerParams(
            dimension_semantics=("arbitrary", "arbitrary"),
            vmem_limit_bytes=VMEM_LIMIT_BYTES,
        ),
        name="hybrid_pool_mla_block",
    )(*args)
```

```python
import functools
import math

import numpy as np
import jax
import jax.numpy as jnp
from jax import lax
from jax.experimental import pallas as pl
from jax.experimental.pallas import tpu as pltpu

D_MODEL = 1024
N_META = 16
CHUNK = 64
D_POOL = 512
POOL_WINDOWS = (2, 4, 8, 16)
POOL_GROUP = 128
N_HEADS = 4
QK_NOPE = 128
QK_ROPE = 64
V_HEAD = 128
D_ATTN = N_HEADS * V_HEAD
Q_LORA = 256
KV_LORA = 128
ROPE_THETA = 10000.0
EPS = 1e-6
Q_SCALE = (QK_NOPE + QK_ROPE) ** -0.5 * math.log2(math.e)

O_CQ = 2 * D_POOL
O_CKV = O_CQ + Q_LORA
O_KR = O_CKV + KV_LORA
O_AG = O_KR + QK_ROPE
D_IN = O_AG + D_ATTN
S_AG = O_AG + QK_ROPE
D_IN_PADDED = S_AG + D_ATTN

LANES = 128
BF16_ROWS = 16
QK_PAD = 2 * LANES
V_AUG = V_HEAD + BF16_ROWS
HALO = 16
ROW_TILE = 512
EXP_HEADROOM = 64.0
MASK_VALUE = -1e30
VMEM_LIMIT_BYTES = 56 * 1024 * 1024

F32 = jnp.float32
BF16 = jnp.bfloat16


def _dot(a, b):
    return jnp.dot(a, b, preferred_element_type=F32)


def _dot_nt(a, b):
    return lax.dot_general(a, b, (((1,), (1,)), ((), ())), preferred_element_type=F32)


def _rms(x, g):
    return x * lax.rsqrt(jnp.mean(x * x, axis=-1, keepdims=True) + EPS) * g


def _silu(x):
    h = 0.5 * x
    return h + h * jnp.tanh(h)


def _rope_lanes(x, cos_t, sin_t):
    lane = lax.broadcasted_iota(jnp.int32, x.shape, 1)
    first_half = (lane % QK_ROPE) < (QK_ROPE // 2)
    partner = jnp.where(first_half,
                        pltpu.roll(x, LANES - QK_ROPE // 2, axis=1),
                        pltpu.roll(x, QK_ROPE // 2, axis=1))
    return x * cos_t + partner * sin_t


def _latent_kv(ckvr, kv_g_ref, w_kb_ref, w_vbt_ref, cos_t, sin_t):
    ckv_n = _rms(ckvr[:, :KV_LORA], kv_g_ref[...]).astype(BF16)
    k_nope = _dot(ckv_n, w_kb_ref[...])
    v_t = _dot_nt(w_vbt_ref[...], ckv_n)
    k_rope = _rope_lanes(ckvr[:, KV_LORA:], cos_t, sin_t)
    return k_nope, v_t, k_rope


def _window_sum(xg, w):
    y = xg
    span = 1
    while span < w:
        y = y + pltpu.roll(y, span, axis=0)
        span *= 2
    return y


def _denominator_rows(n):
    return (lax.broadcasted_iota(jnp.int32, (BF16_ROWS, n), 0) == 0).astype(BF16)


def _fused_kernel(tiles_per_seq, x_ref, meta_ref, kcos_ref, ksin_ref, mcos_ref, msin_ref, qcos_ref, qsin_ref,
                  norm_g_ref, w_in_ref, q_g_ref, w_qb_ref, kv_g_ref, w_kvb_ref,
                  pool_w_f32_ref, pool_scale_ref, w_out_f32_ref, final_g_ref,
                  out_ref,
                  w_in_s, w_qbt_ref, w_kb_ref, w_vbt_ref, pool_w_ref, w_out_ref,
                  k_scr, vt_scr, kmeta_scr, vtmeta_scr, pool_scr, halo_scr,
                  qt_scr, s_scr, m_scr, acc_scr, rise_scr):
    b = pl.program_id(0)
    i = pl.program_id(1)
    tm = ROW_TILE
    first_step = (b == 0) & (i == 0)

    w_pi_ref = w_in_s.at[:, 0:D_POOL]
    w_pg_ref = w_in_s.at[:, D_POOL:O_CQ]
    w_cq_ref = w_in_s.at[:, O_CQ:O_CKV]
    w_ckv_ref = w_in_s.at[:, O_CKV:S_AG]
    w_ag_ref = w_in_s.at[:, S_AG:S_AG + D_ATTN]

    @pl.when(first_step)
    def _prepare_weights():
        for c in range(O_KR // LANES):
            w_in_s[:, c * LANES:(c + 1) * LANES] = w_in_ref[c * LANES:(c + 1) * LANES, :].T.astype(BF16)
        k_rope_rows = w_in_ref[O_KR:O_AG, :]
        w_in_s[:, O_KR:S_AG] = jnp.concatenate([k_rope_rows, k_rope_rows], axis=0).T.astype(BF16)
        for c in range(D_ATTN // LANES):
            w_in_s[:, S_AG + c * LANES:S_AG + (c + 1) * LANES] = (
                w_in_ref[O_AG + c * LANES:O_AG + (c + 1) * LANES, :].T.astype(BF16))
        quarter = D_MODEL // 4
        for r in range(4):
            rs = slice(r * quarter, (r + 1) * quarter)
            w_out_ref[rs, :] = w_out_f32_ref[rs, :].astype(BF16)
        w_qbt_ref[...] = w_qb_ref[...].T.astype(BF16)
        for h in range(N_HEADS):
            base = h * (QK_NOPE + V_HEAD)
            w_kb_ref[:, h * QK_NOPE:(h + 1) * QK_NOPE] = w_kvb_ref[:, base:base + QK_NOPE].astype(BF16)
            w_vbt_ref[h * V_HEAD:(h + 1) * V_HEAD, :] = w_kvb_ref[:, base + QK_NOPE:base + QK_NOPE + V_HEAD].T.astype(BF16)
        pool_w_ref[...] = pool_w_f32_ref[...].astype(BF16)

    @pl.when(first_step)
    def _meta_prologue():
        xn = _rms(meta_ref[...], norm_g_ref[...]).astype(BF16)
        halo_scr[...] = _dot(xn, w_pi_ref[...])
        k_nope, v_t, k_rope = _latent_kv(_dot(xn, w_ckv_ref[...]), kv_g_ref, w_kb_ref, w_vbt_ref,
                                         mcos_ref[...], msin_ref[...])
        for h in range(N_HEADS):
            kmeta_scr[h, :, 0:LANES] = k_nope[:, h * QK_NOPE:(h + 1) * QK_NOPE].astype(BF16)
            kmeta_scr[h, :, LANES:QK_PAD] = k_rope.astype(BF16)
            vtmeta_scr[h, 0:V_HEAD, :] = v_t[h * V_HEAD:(h + 1) * V_HEAD, :].astype(BF16)
            vtmeta_scr[h, V_HEAD:V_AUG, :] = _denominator_rows(N_META)

    @pl.when(i == 0)
    def _seed_halo():
        pool_scr[0:HALO, :] = halo_scr[...]

    def _process_tile(n_earlier):
        hm = tm // 2
        halves = (slice(0, hm), slice(hm, tm))
        xn_halves = [_rms(x_ref[r, :], norm_g_ref[...]).astype(BF16) for r in halves]

        for r, xn_half in zip(halves, xn_halves):
            pool_scr[HALO + r.start:HALO + r.stop, :] = _dot(xn_half, w_pi_ref[...])
        xn = jnp.concatenate(xn_halves, axis=0)
        cq = _dot(xn, w_cq_ref[...])
        ckvr = _dot(xn, w_ckv_ref[...])
        pool_gate = _dot(xn, w_pg_ref[...])

        cq_n = _rms(cq, q_g_ref[...]).astype(BF16)
        q_t = _dot_nt(w_qbt_ref[...], cq_n) * Q_SCALE
        k_nope, v_t, k_rope = _latent_kv(ckvr, kv_g_ref, w_kb_ref, w_vbt_ref, kcos_ref[...], ksin_ref[...])

        pooled = []
        for g, w in enumerate(POOL_WINDOWS):
            xg = pool_scr[:, g * POOL_GROUP:(g + 1) * POOL_GROUP]
            win = _window_sum(xg, w)
            pooled.append((win[HALO:, :] * (1.0 / w) - xg[HALO:, :]).astype(BF16))

        row0 = pl.multiple_of(i * tm, tm)
        tile_rows = pl.ds(row0, tm)
        k_rope_bf = k_rope.astype(BF16)
        denominator_rows = _denominator_rows(tm)
        for h in range(N_HEADS):
            k_scr[h, tile_rows, 0:LANES] = k_nope[:, h * QK_NOPE:(h + 1) * QK_NOPE].astype(BF16)
            k_scr[h, tile_rows, LANES:QK_PAD] = k_rope_bf
            vt_scr[h, 0:V_HEAD, tile_rows] = v_t[h * V_HEAD:(h + 1) * V_HEAD, :].astype(BF16)
            vt_scr[h, V_HEAD:V_AUG, tile_rows] = denominator_rows

        cos_q = qcos_ref[...]
        sin_q = qsin_ref[...]
        half = QK_ROPE // 2
        for h in range(N_HEADS):
            base = h * (QK_NOPE + QK_ROPE)
            x1 = q_t[base + QK_NOPE:base + QK_NOPE + half, :]
            x2 = q_t[base + QK_NOPE + half:base + QK_NOPE + QK_ROPE, :]
            qt_scr[h, 0:QK_NOPE, :] = q_t[base:base + QK_NOPE, :].astype(BF16)
            qt_scr[h, QK_NOPE:QK_NOPE + half, :] = (x1 * cos_q - x2 * sin_q).astype(BF16)
            qt_scr[h, QK_NOPE + half:QK_NOPE + QK_ROPE, :] = (x1 * sin_q + x2 * cos_q).astype(BF16)
            qt_scr[h, QK_NOPE + QK_ROPE:QK_PAD, :] = jnp.zeros((QK_PAD - QK_NOPE - QK_ROPE, tm), BF16)

        mixed = [_dot(pooled[g], pool_w_ref[g]) for g in range(len(POOL_WINDOWS))]
        pool_out = (_silu(pool_gate) * (jnp.concatenate(mixed, axis=1) * pool_scale_ref[...])).astype(BF16)

        def _scores(h, rows):
            return _dot(k_scr[h, rows, :], qt_scr[h])

        def _exact_update(h, blocks):
            m_prev = m_scr[h]
            m_new = m_prev
            for s, _ in blocks:
                m_new = jnp.maximum(m_new, jnp.max(s, axis=0, keepdims=True))
            acc = jnp.exp2(m_prev - m_new) * acc_scr[h]
            for s, vt_blk in blocks:
                acc = acc + _dot(vt_blk, jnp.exp2(s - m_new).astype(BF16))
            acc_scr[h] = acc
            m_scr[h] = m_new

        def _stale_update(h, blocks):
            m_ref = m_scr[h]
            acc = acc_scr[h]
            rise = rise_scr[...]
            lift = jnp.zeros_like(m_ref)
            for s, vt_blk in blocks:
                t = s - m_ref
                top = jnp.max(t, axis=0, keepdims=True)
                acc = acc + _dot(vt_blk, jnp.exp2(t).astype(BF16))
                rise = jnp.maximum(rise, top)
                lift = jnp.maximum(lift, top)
            acc_scr[h] = acc * jnp.exp2(-lift)
            m_scr[h] = m_ref + lift
            rise_scr[...] = rise

        key_chunk = lax.broadcasted_iota(jnp.int32, (tm, tm), 0) // CHUNK
        query_chunk = lax.broadcasted_iota(jnp.int32, (tm, tm), 1) // CHUNK
        diag_valid = key_chunk <= query_chunk

        def _start_from_meta_keys():
            s_meta = [_dot(kmeta_scr[h], qt_scr[h]) for h in range(N_HEADS)]
            for h in range(N_HEADS):
                m_scr[h] = jnp.max(s_meta[h], axis=0, keepdims=True)
                acc_scr[h] = jnp.zeros((V_AUG, tm), F32)
            return s_meta

        def _attend_unrolled(update):
            s_meta = _start_from_meta_keys()
            own_rows = pl.ds(n_earlier * tm, tm)
            units = ([(None, h) for h in range(N_HEADS)]
                     + [(j, h) for j in range(n_earlier) for h in range(N_HEADS)])

            def unit_scores(unit):
                j, h = unit
                if j is None:
                    return jnp.where(diag_valid, _scores(h, own_rows), MASK_VALUE)
                return _scores(h, pl.ds(j * tm, tm))

            s_next = unit_scores(units[0])
            for n, (j, h) in enumerate(units):
                s = s_next
                if n + 1 < len(units):
                    s_next = unit_scores(units[n + 1])
                if n == 0:
                    attn_gate = _dot(xn, w_ag_ref[...])
                if j is None:
                    update(h, [(s_meta[h], vtmeta_scr[h]), (s, vt_scr[h, :, own_rows])])
                else:
                    update(h, [(s, vt_scr[h, :, pl.ds(j * tm, tm)])])
            return attn_gate

        def _attend_looped(update):
            s_meta = _start_from_meta_keys()
            s_next = jnp.where(diag_valid, _scores(0, tile_rows), MASK_VALUE)
            for h in range(N_HEADS):
                s = s_next
                if h + 1 < N_HEADS:
                    s_next = jnp.where(diag_valid, _scores(h + 1, tile_rows), MASK_VALUE)
                else:
                    s_scr[...] = _scores(0, pl.ds(0, tm))
                update(h, [(s_meta[h], vtmeta_scr[h]), (s, vt_scr[h, :, tile_rows])])

            def _earlier_tile(j, carry):
                rows = pl.ds(pl.multiple_of(j * tm, tm), tm)
                next_rows = pl.ds(pl.multiple_of((j + 1) * tm, tm), tm)
                s_next = s_scr[...]
                for h in range(N_HEADS):
                    s = s_next
                    if h + 1 < N_HEADS:
                        s_next = _scores(h + 1, rows)
                    else:
                        s_scr[...] = _scores(0, next_rows)
                    update(h, [(s, vt_scr[h, :, rows])])
                return carry

            lax.fori_loop(0, i, _earlier_tile, 0)
            return _dot(xn, w_ag_ref[...])

        attn_gate = _attend_looped(_exact_update) if n_earlier is None else _attend_unrolled(_stale_update)

        mix_pool = _dot(pool_out, w_out_ref[0:D_POOL, :])
        attn = []
        for h in range(N_HEADS):
            acc = acc_scr[h]
            attn.append((acc[0:V_HEAD, :] * (1.0 / acc[V_HEAD:V_HEAD + 1, :])).T)
        attn_out = (_silu(attn_gate) * jnp.concatenate(attn, axis=1)).astype(BF16)
        for r in halves:
            mix = mix_pool[r, :] + _dot(attn_out[r, :], w_out_ref[D_POOL:, :])
            out_ref[r, :] = _rms(x_ref[r, :] + mix, final_g_ref[...])

    rise_scr[...] = jnp.full((1, tm), MASK_VALUE, F32)
    for n_earlier in range(tiles_per_seq):
        @pl.when(i == n_earlier)
        def _tile_at(n_earlier=n_earlier):
            _process_tile(n_earlier)

    @pl.when(jnp.max(rise_scr[...]) >= EXP_HEADROOM)
    def _redo_exactly():
        _process_tile(None)

    pool_scr[0:HALO, :] = pool_scr[tm:tm + HALO, :]


def _rope_tables(length):
    half = QK_ROPE // 2
    inv_freq = 1.0 / (ROPE_THETA ** (np.arange(half, dtype=np.float64) / half))
    ang = np.arange(length, dtype=np.float64)[:, None] * inv_freq[None, :]
    return np.cos(ang), np.sin(ang)


def kernel(x, meta_tokens, norm_g, w_in, q_norm_g, w_q_b, kv_norm_g, w_kv_b, pool_w, pool_scale, w_out, final_norm_g):
    batch, seq, d_model = x.shape
    assert d_model == D_MODEL and seq % ROW_TILE == 0 and meta_tokens.shape == (N_META, D_MODEL)
    assert norm_g.shape[0] == 1 and w_in.shape == (1, D_MODEL, D_IN), "single-layer block"
    tm = ROW_TILE

    cos, sin = _rope_tables(N_META + seq)
    kcos = np.tile(cos, (1, LANES // (QK_ROPE // 2))).astype(np.float32)
    ksin = np.tile(np.concatenate([-sin, sin], axis=1), (1, LANES // QK_ROPE)).astype(np.float32)
    qcos = np.ascontiguousarray(cos[N_META:].T).astype(np.float32)
    qsin = np.ascontiguousarray(sin[N_META:].T).astype(np.float32)

    def whole(a):
        return pl.BlockSpec(a.shape, lambda b, i: (0,) * a.ndim)

    def fetched_once(a, squeeze_layer):
        shape = ((None,) + a.shape[1:]) if squeeze_layer else a.shape
        return pl.BlockSpec(shape, lambda b, i: (0,) * a.ndim, pipeline_mode=pl.Buffered(1))

    row_spec = pl.BlockSpec((None, tm, D_MODEL), lambda b, i: (b, i, 0))
    ktable_spec = pl.BlockSpec((tm, LANES), lambda b, i: (i, 0))
    qtable_spec = pl.BlockSpec((QK_ROPE // 2, tm), lambda b, i: (0, i))

    w_in_t = jnp.transpose(w_in[0])

    operands = [
        (x, row_spec),
        (meta_tokens, None),
        (kcos[N_META:], ktable_spec),
        (ksin[N_META:], ktable_spec),
        (kcos[:N_META], None),
        (ksin[:N_META], None),
        (qcos, qtable_spec),
        (qsin, qtable_spec),
        (norm_g, None),
        (w_in_t, fetched_once(w_in_t, False)),
        (q_norm_g, None), (w_q_b, fetched_once(w_q_b, True)),
        (kv_norm_g, None), (w_kv_b, fetched_once(w_kv_b, True)),
        (pool_w, fetched_once(pool_w, True)),
        (pool_scale, None),
        (w_out, fetched_once(w_out, True)),
        (final_norm_g.reshape(1, D_MODEL), None),
    ]
    args = [a for a, _ in operands]
    in_specs = [whole(a) if s is None else s for a, s in operands]

    return pl.pallas_call(
        functools.partial(_fused_kernel, seq // tm),
        out_shape=jax.ShapeDtypeStruct(x.shape, x.dtype),
        grid=(batch, seq // tm),
        in_specs=in_specs,
        out_specs=row_spec,
        scratch_shapes=[
            pltpu.VMEM((D_MODEL, D_IN_PADDED), BF16),
            pltpu.VMEM((N_HEADS * (QK_NOPE + QK_ROPE), Q_LORA), BF16),
            pltpu.VMEM((KV_LORA, N_HEADS * QK_NOPE), BF16),
            pltpu.VMEM((N_HEADS * V_HEAD, KV_LORA), BF16),
            pltpu.VMEM((len(POOL_WINDOWS), POOL_GROUP, POOL_GROUP), BF16),
            pltpu.VMEM((D_MODEL, D_MODEL), BF16),
            pltpu.VMEM((N_HEADS, seq, QK_PAD), BF16),
            pltpu.VMEM((N_HEADS, V_AUG, seq), BF16),
            pltpu.VMEM((N_HEADS, N_META, QK_PAD), BF16),
            pltpu.VMEM((N_HEADS, V_AUG, N_META), BF16),
            pltpu.VMEM((HALO + tm, D_POOL), F32),
            pltpu.VMEM((HALO, D_POOL), F32),
            pltpu.VMEM((N_HEADS, QK_PAD, tm), BF16),
            pltpu.VMEM((tm, tm), F32),
            pltpu.VMEM((N_HEADS, 1, tm), F32),
            pltpu.VMEM((N_HEADS, V_AUG, tm), F32),
            pltpu.VMEM((1, tm), F32),
        ],
        compiler_params=pltpu.CompilerParams(
            dimension_semantics=("arbitrary", "arbitrary"),
            vmem_limit_bytes=VMEM_LIMIT_BYTES,
        ),
        name="hybrid_pool_mla_block",
    )(*args)
```

```python
import functools
import math

import numpy as np
import jax
import jax.numpy as jnp
from jax import lax
from jax.experimental import pallas as pl
from jax.experimental.pallas import tpu as pltpu

D_MODEL = 1024
N_META = 16
CHUNK = 64
D_POOL = 512
POOL_WINDOWS = (2, 4, 8, 16)
POOL_GROUP = 128
N_HEADS = 4
QK_NOPE = 128
QK_ROPE = 64
V_HEAD = 128
D_ATTN = N_HEADS * V_HEAD
Q_LORA = 256
KV_LORA = 128
ROPE_THETA = 10000.0
EPS = 1e-6
Q_SCALE = (QK_NOPE + QK_ROPE) ** -0.5 * math.log2(math.e)

O_CQ = 2 * D_POOL
O_CKV = O_CQ + Q_LORA
O_KR = O_CKV + KV_LORA
O_AG = O_KR + QK_ROPE
D_IN = O_AG + D_ATTN
S_AG = O_AG + QK_ROPE
D_IN_PADDED = S_AG + D_ATTN

LANES = 128
BF16_ROWS = 16
QK_PAD = 2 * LANES
V_AUG = V_HEAD + BF16_ROWS
HALO = 16
ROW_TILE = 512
EXP_HEADROOM = 64.0
MASK_VALUE = -1e30
VMEM_LIMIT_BYTES = 56 * 1024 * 1024

F32 = jnp.float32
BF16 = jnp.bfloat16


def _dot(a, b):
    return jnp.dot(a, b, preferred_element_type=F32)


def _dot_nt(a, b):
    return lax.dot_general(a, b, (((1,), (1,)), ((), ())), preferred_element_type=F32)


def _rms(x, g):
    return x * lax.rsqrt(jnp.mean(x * x, axis=-1, keepdims=True) + EPS) * g


def _silu(x):
    h = 0.5 * x
    return h + h * jnp.tanh(h)


def _rope_lanes(x, cos_t, sin_t):
    lane = lax.broadcasted_iota(jnp.int32, x.shape, 1)
    first_half = (lane % QK_ROPE) < (QK_ROPE // 2)
    partner = jnp.where(first_half,
                        pltpu.roll(x, LANES - QK_ROPE // 2, axis=1),
                        pltpu.roll(x, QK_ROPE // 2, axis=1))
    return x * cos_t + partner * sin_t


def _latent_kv(ckvr, kv_g_ref, w_kb_ref, w_vbt_ref, cos_t, sin_t):
    ckv_n = _rms(ckvr[:, :KV_LORA], kv_g_ref[...]).astype(BF16)
    k_nope = _dot(ckv_n, w_kb_ref[...])
    v_t = _dot_nt(w_vbt_ref[...], ckv_n)
    k_rope = _rope_lanes(ckvr[:, KV_LORA:], cos_t, sin_t)
    return k_nope, v_t, k_rope


def _window_sum(xg, w):
    y = xg
    span = 1
    while span < w:
        y = y + pltpu.roll(y, span, axis=0)
        span *= 2
    return y


def _denominator_rows(n):
    return (lax.broadcasted_iota(jnp.int32, (BF16_ROWS, n), 0) == 0).astype(BF16)


def _fused_kernel(tiles_per_seq, x_ref, meta_ref, kcos_ref, ksin_ref, mcos_ref, msin_ref, qcos_ref, qsin_ref,
                  norm_g_ref, w_in_ref, q_g_ref, w_qb_ref, kv_g_ref, w_kvb_ref,
                  pool_w_f32_ref, pool_scale_ref, w_out_f32_ref, final_g_ref,
                  out_ref,
                  w_in_s, w_qbt_ref, w_kb_ref, w_vbt_ref, pool_w_ref, w_out_ref,
                  k_scr, vt_scr, kmeta_scr, vtmeta_scr, pool_scr, halo_scr,
                  qt_scr, s_scr, m_scr, acc_scr, rise_scr):
    b = pl.program_id(0)
    i = pl.program_id(1)
    tm = ROW_TILE
    first_step = (b == 0) & (i == 0)

    w_pi_ref = w_in_s.at[:, 0:D_POOL]
    w_pg_ref = w_in_s.at[:, D_POOL:O_CQ]
    w_cq_ref = w_in_s.at[:, O_CQ:O_CKV]
    w_ckv_ref = w_in_s.at[:, O_CKV:S_AG]
    w_ag_ref = w_in_s.at[:, S_AG:S_AG + D_ATTN]

    @pl.when(first_step)
    def _prepare_weights():
        for c in range(O_KR // LANES):
            w_in_s[:, c * LANES:(c + 1) * LANES] = w_in_ref[c * LANES:(c + 1) * LANES, :].T.astype(BF16)
        k_rope_rows = w_in_ref[O_KR:O_AG, :]
        w_in_s[:, O_KR:S_AG] = jnp.concatenate([k_rope_rows, k_rope_rows], axis=0).T.astype(BF16)
        for c in range(D_ATTN // LANES):
            w_in_s[:, S_AG + c * LANES:S_AG + (c + 1) * LANES] = (
                w_in_ref[O_AG + c * LANES:O_AG + (c + 1) * LANES, :].T.astype(BF16))
        quarter = D_MODEL // 4
        for r in range(4):
            rs = slice(r * quarter, (r + 1) * quarter)
            w_out_ref[rs, :] = w_out_f32_ref[rs, :].astype(BF16)
        w_qbt_ref[...] = w_qb_ref[...].T.astype(BF16)
        for h in range(N_HEADS):
            base = h * (QK_NOPE + V_HEAD)
            w_kb_ref[:, h * QK_NOPE:(h + 1) * QK_NOPE] = w_kvb_ref[:, base:base + QK_NOPE].astype(BF16)
            w_vbt_ref[h * V_HEAD:(h + 1) * V_HEAD, :] = w_kvb_ref[:, base + QK_NOPE:base + QK_NOPE + V_HEAD].T.astype(BF16)
        pool_w_ref[...] = pool_w_f32_ref[...].astype(BF16)

    @pl.when(first_step)
    def _meta_prologue():
        xn = _rms(meta_ref[...], norm_g_ref[...]).astype(BF16)
        halo_scr[...] = _dot(xn, w_pi_ref[...])
        k_nope, v_t, k_rope = _latent_kv(_dot(xn, w_ckv_ref[...]), kv_g_ref, w_kb_ref, w_vbt_ref,
                                         mcos_ref[...], msin_ref[...])
        for h in range(N_HEADS):
            kmeta_scr[h, :, 0:LANES] = k_nope[:, h * QK_NOPE:(h + 1) * QK_NOPE].astype(BF16)
            kmeta_scr[h, :, LANES:QK_PAD] = k_rope.astype(BF16)
            vtmeta_scr[h, 0:V_HEAD, :] = v_t[h * V_HEAD:(h + 1) * V_HEAD, :].astype(BF16)
            vtmeta_scr[h, V_HEAD:V_AUG, :] = _denominator_rows(N_META)

    @pl.when(i == 0)
    def _seed_halo():
        pool_scr[0:HALO, :] = halo_scr[...]

    def _process_tile(n_earlier):
        hm = tm // 2
        halves = (slice(0, hm), slice(hm, tm))
        xn_halves = [_rms(x_ref[r, :], norm_g_ref[...]).astype(BF16) for r in halves]

        for r, xn_half in zip(halves, xn_halves):
            pool_scr[HALO + r.start:HALO + r.stop, :] = _dot(xn_half, w_pi_ref[...])
        xn = jnp.concatenate(xn_halves, axis=0)
        cq = _dot(xn, w_cq_ref[...])
        ckvr = _dot(xn, w_ckv_ref[...])

        pooled = []
        for g, w in enumerate(POOL_WINDOWS):
            xg = pool_scr[:, g * POOL_GROUP:(g + 1) * POOL_GROUP]
            win = _window_sum(xg, w)
            pooled.append((win[HALO:, :] * (1.0 / w) - xg[HALO:, :]).astype(BF16))
        mixed = [_dot(pooled[g], pool_w_ref[g]) for g in range(len(POOL_WINDOWS))]
        pool_gate = _dot(xn, w_pg_ref[...])

        cq_n = _rms(cq, q_g_ref[...]).astype(BF16)
        q_t = _dot_nt(w_qbt_ref[...], cq_n) * Q_SCALE
        k_nope, v_t, k_rope = _latent_kv(ckvr, kv_g_ref, w_kb_ref, w_vbt_ref, kcos_ref[...], ksin_ref[...])

        row0 = pl.multiple_of(i * tm, tm)
        tile_rows = pl.ds(row0, tm)
        k_rope_bf = k_rope.astype(BF16)
        denominator_rows = _denominator_rows(tm)
        for h in range(N_HEADS):
            k_scr[h, tile_rows, 0:LANES] = k_nope[:, h * QK_NOPE:(h + 1) * QK_NOPE].astype(BF16)
            k_scr[h, tile_rows, LANES:QK_PAD] = k_rope_bf
            vt_scr[h, 0:V_HEAD, tile_rows] = v_t[h * V_HEAD:(h + 1) * V_HEAD, :].astype(BF16)
            vt_scr[h, V_HEAD:V_AUG, tile_rows] = denominator_rows

        cos_q = qcos_ref[...]
        sin_q = qsin_ref[...]
        half = QK_ROPE // 2
        for h in range(N_HEADS):
            base = h * (QK_NOPE + QK_ROPE)
            x1 = q_t[base + QK_NOPE:base + QK_NOPE + half, :]
            x2 = q_t[base + QK_NOPE + half:base + QK_NOPE + QK_ROPE, :]
            qt_scr[h, 0:QK_NOPE, :] = q_t[base:base + QK_NOPE, :].astype(BF16)
            qt_scr[h, QK_NOPE:QK_NOPE + half, :] = (x1 * cos_q - x2 * sin_q).astype(BF16)
            qt_scr[h, QK_NOPE + half:QK_NOPE + QK_ROPE, :] = (x1 * sin_q + x2 * cos_q).astype(BF16)
            qt_scr[h, QK_NOPE + QK_ROPE:QK_PAD, :] = jnp.zeros((QK_PAD - QK_NOPE - QK_ROPE, tm), BF16)

        pool_out = (_silu(pool_gate) * (jnp.concatenate(mixed, axis=1) * pool_scale_ref[...])).astype(BF16)

        def _scores(h, rows):
            return _dot(k_scr[h, rows, :], qt_scr[h])

        def _exact_update(h, blocks):
            m_prev = m_scr[h]
            m_new = m_prev
            for s, _ in blocks:
                m_new = jnp.maximum(m_new, jnp.max(s, axis=0, keepdims=True))
            acc = jnp.exp2(m_prev - m_new) * acc_scr[h]
            for s, vt_blk in blocks:
                acc = acc + _dot(vt_blk, jnp.exp2(s - m_new).astype(BF16))
            acc_scr[h] = acc
            m_scr[h] = m_new

        def _stale_update(h, blocks):
            m_ref = m_scr[h]
            acc = acc_scr[h]
            rise = rise_scr[...]
            lift = jnp.zeros_like(m_ref)
            for s, vt_blk in blocks:
                t = s - m_ref
                top = jnp.max(t, axis=0, keepdims=True)
                acc = acc + _dot(vt_blk, jnp.exp2(t).astype(BF16))
                rise = jnp.maximum(rise, top)
                lift = jnp.maximum(lift, top)
            acc_scr[h] = acc * jnp.exp2(-lift)
            m_scr[h] = m_ref + lift
            rise_scr[...] = rise

        key_chunk = lax.broadcasted_iota(jnp.int32, (tm, tm), 0) // CHUNK
        query_chunk = lax.broadcasted_iota(jnp.int32, (tm, tm), 1) // CHUNK
        diag_valid = key_chunk <= query_chunk

        def _start_from_meta_keys():
            s_meta = [_dot(kmeta_scr[h], qt_scr[h]) for h in range(N_HEADS)]
            for h in range(N_HEADS):
                m_scr[h] = jnp.max(s_meta[h], axis=0, keepdims=True)
                acc_scr[h] = jnp.zeros((V_AUG, tm), F32)
            return s_meta

        def _attend_unrolled(update):
            s_meta = _start_from_meta_keys()
            own_rows = pl.ds(n_earlier * tm, tm)
            units = ([(None, h) for h in range(N_HEADS)]
                     + [(j, h) for j in range(n_earlier) for h in range(N_HEADS)])

            def unit_scores(unit):
                j, h = unit
                if j is None:
                    return jnp.where(diag_valid, _scores(h, own_rows), MASK_VALUE)
                return _scores(h, pl.ds(j * tm, tm))

            s_next = unit_scores(units[0])
            for n, (j, h) in enumerate(units):
                s = s_next
                if n + 1 < len(units):
                    s_next = unit_scores(units[n + 1])
                if n == 0:
                    attn_gate = _dot(xn, w_ag_ref[...])
                if j is None:
                    update(h, [(s_meta[h], vtmeta_scr[h]), (s, vt_scr[h, :, own_rows])])
                else:
                    update(h, [(s, vt_scr[h, :, pl.ds(j * tm, tm)])])
            return attn_gate

        def _attend_looped(update):
            s_meta = _start_from_meta_keys()
            s_next = jnp.where(diag_valid, _scores(0, tile_rows), MASK_VALUE)
            for h in range(N_HEADS):
                s = s_next
                if h + 1 < N_HEADS:
                    s_next = jnp.where(diag_valid, _scores(h + 1, tile_rows), MASK_VALUE)
                else:
                    s_scr[...] = _scores(0, pl.ds(0, tm))
                update(h, [(s_meta[h], vtmeta_scr[h]), (s, vt_scr[h, :, tile_rows])])

            def _earlier_tile(j, carry):
                rows = pl.ds(pl.multiple_of(j * tm, tm), tm)
                next_rows = pl.ds(pl.multiple_of((j + 1) * tm, tm), tm)
                s_next = s_scr[...]
                for h in range(N_HEADS):
                    s = s_next
                    if h + 1 < N_HEADS:
                        s_next = _scores(h + 1, rows)
                    else:
                        s_scr[...] = _scores(0, next_rows)
                    update(h, [(s, vt_scr[h, :, rows])])
                return carry

            lax.fori_loop(0, i, _earlier_tile, 0)
            return _dot(xn, w_ag_ref[...])

        attn_gate = _attend_looped(_exact_update) if n_earlier is None else _attend_unrolled(_stale_update)

        mix_pool = _dot(pool_out, w_out_ref[0:D_POOL, :])
        attn = []
        for h in range(N_HEADS):
            acc = acc_scr[h]
            attn.append((acc[0:V_HEAD, :] * (1.0 / acc[V_HEAD:V_HEAD + 1, :])).T)
        attn_out = (_silu(attn_gate) * jnp.concatenate(attn, axis=1)).astype(BF16)
        for r in halves:
            mix = mix_pool[r, :] + _dot(attn_out[r, :], w_out_ref[D_POOL:, :])
            out_ref[r, :] = _rms(x_ref[r, :] + mix, final_g_ref[...])

    rise_scr[...] = jnp.full((1, tm), MASK_VALUE, F32)
    for n_earlier in range(tiles_per_seq):
        @pl.when(i == n_earlier)
        def _tile_at(n_earlier=n_earlier):
            _process_tile(n_earlier)

    @pl.when(jnp.max(rise_scr[...]) >= EXP_HEADROOM)
    def _redo_exactly():
        _process_tile(None)

    pool_scr[0:HALO, :] = pool_scr[tm:tm + HALO, :]


def _rope_tables(length):
    half = QK_ROPE // 2
    inv_freq = 1.0 / (ROPE_THETA ** (np.arange(half, dtype=np.float64) / half))
    ang = np.arange(length, dtype=np.float64)[:, None] * inv_freq[None, :]
    return np.cos(ang), np.sin(ang)


def kernel(x, meta_tokens, norm_g, w_in, q_norm_g, w_q_b, kv_norm_g, w_kv_b, pool_w, pool_scale, w_out, final_norm_g):
    batch, seq, d_model = x.shape
    assert d_model == D_MODEL and seq % ROW_TILE == 0 and meta_tokens.shape == (N_META, D_MODEL)
    assert norm_g.shape[0] == 1 and w_in.shape == (1, D_MODEL, D_IN), "single-layer block"
    tm = ROW_TILE

    cos, sin = _rope_tables(N_META + seq)
    kcos = np.tile(cos, (1, LANES // (QK_ROPE // 2))).astype(np.float32)
    ksin = np.tile(np.concatenate([-sin, sin], axis=1), (1, LANES // QK_ROPE)).astype(np.float32)
    qcos = np.ascontiguousarray(cos[N_META:].T).astype(np.float32)
    qsin = np.ascontiguousarray(sin[N_META:].T).astype(np.float32)

    def whole(a):
        return pl.BlockSpec(a.shape, lambda b, i: (0,) * a.ndim)

    def fetched_once(a, squeeze_layer):
        shape = ((None,) + a.shape[1:]) if squeeze_layer else a.shape
        return pl.BlockSpec(shape, lambda b, i: (0,) * a.ndim, pipeline_mode=pl.Buffered(1))

    row_spec = pl.BlockSpec((None, tm, D_MODEL), lambda b, i: (b, i, 0))
    ktable_spec = pl.BlockSpec((tm, LANES), lambda b, i: (i, 0))
    qtable_spec = pl.BlockSpec((QK_ROPE // 2, tm), lambda b, i: (0, i))

    w_in_t = jnp.transpose(w_in[0])

    operands = [
        (x, row_spec),
        (meta_tokens, None),
        (kcos[N_META:], ktable_spec),
        (ksin[N_META:], ktable_spec),
        (kcos[:N_META], None),
        (ksin[:N_META], None),
        (qcos, qtable_spec),
        (qsin, qtable_spec),
        (norm_g, None),
        (w_in_t, fetched_once(w_in_t, False)),
        (q_norm_g, None), (w_q_b, fetched_once(w_q_b, True)),
        (kv_norm_g, None), (w_kv_b, fetched_once(w_kv_b, True)),
        (pool_w, fetched_once(pool_w, True)),
        (pool_scale, None),
        (w_out, fetched_once(w_out, True)),
        (final_norm_g.reshape(1, D_MODEL), None),
    ]
    args = [a for a, _ in operands]
    in_specs = [whole(a) if s is None else s for a, s in operands]

    return pl.pallas_call(
        functools.partial(_fused_kernel, seq // tm),
        out_shape=jax.ShapeDtypeStruct(x.shape, x.dtype),
        grid=(batch, seq // tm),
        in_specs=in_specs,
        out_specs=row_spec,
        scratch_shapes=[
            pltpu.VMEM((D_MODEL, D_IN_PADDED), BF16),
            pltpu.VMEM((N_HEADS * (QK_NOPE + QK_ROPE), Q_LORA), BF16),
            pltpu.VMEM((KV_LORA, N_HEADS * QK_NOPE), BF16),
            pltpu.VMEM((N_HEADS * V_HEAD, KV_LORA), BF16),
            pltpu.VMEM((len(POOL_WINDOWS), POOL_GROUP, POOL_GROUP), BF16),
            pltpu.VMEM((D_MODEL, D_MODEL), BF16),
            pltpu.VMEM((N_HEADS, seq, QK_PAD), BF16),
            pltpu.VMEM((N_HEADS, V_AUG, seq), BF16),
            pltpu.VMEM((N_HEADS, N_META, QK_PAD), BF16),
            pltpu.VMEM((N_HEADS, V_AUG, N_META), BF16),
            pltpu.VMEM((HALO + tm, D_POOL), F32),
            pltpu.VMEM((HALO, D_POOL), F32),
            pltpu.VMEM((N_HEADS, QK_PAD, tm), BF16),
            pltpu.VMEM((tm, tm), F32),
            pltpu.VMEM((N_HEADS, 1, tm), F32),
            pltpu.VMEM((N_HEADS, V_AUG, tm), F32),
            pltpu.VMEM((1, tm), F32),
        ],
        compiler_params=pltpu.CompilerParams(
            dimension_semantics=("arbitrary", "arbitrary"),
            vmem_limit_bytes=VMEM_LIMIT_BYTES,
        ),
        name="hybrid_pool_mla_block",
    )(*args)
```

```python
import functools
import math

import numpy as np
import jax
import jax.numpy as jnp
from jax import lax
from jax.experimental import pallas as pl
from jax.experimental.pallas import tpu as pltpu

D_MODEL = 1024
N_META = 16
CHUNK = 64
D_POOL = 512
POOL_WINDOWS = (2, 4, 8, 16)
POOL_GROUP = 128
N_HEADS = 4
QK_NOPE = 128
QK_ROPE = 64
V_HEAD = 128
D_ATTN = N_HEADS * V_HEAD
Q_LORA = 256
KV_LORA = 128
ROPE_THETA = 10000.0
EPS = 1e-6
Q_SCALE = (QK_NOPE + QK_ROPE) ** -0.5 * math.log2(math.e)

O_CQ = 2 * D_POOL
O_CKV = O_CQ + Q_LORA
O_KR = O_CKV + KV_LORA
O_AG = O_KR + QK_ROPE
D_IN = O_AG + D_ATTN
S_AG = O_AG + QK_ROPE
D_IN_PADDED = S_AG + D_ATTN

LANES = 128
BF16_ROWS = 16
QK_PAD = 2 * LANES
V_AUG = V_HEAD + BF16_ROWS
HALO = 16
ROW_TILE = 512
TILES_PER_STEP = 2
EXP_HEADROOM = 64.0
MASK_VALUE = -1e30
VMEM_LIMIT_BYTES = 60 * 1024 * 1024

F32 = jnp.float32
BF16 = jnp.bfloat16


def _dot(a, b):
    return jnp.dot(a, b, preferred_element_type=F32)


def _dot_nt(a, b):
    return lax.dot_general(a, b, (((1,), (1,)), ((), ())), preferred_element_type=F32)


def _rms(x, g):
    return x * lax.rsqrt(jnp.mean(x * x, axis=-1, keepdims=True) + EPS) * g


def _silu(x):
    h = 0.5 * x
    return h + h * jnp.tanh(h)


def _rope_lanes(x, cos_t, sin_t):
    lane = lax.broadcasted_iota(jnp.int32, x.shape, 1)
    first_half = (lane % QK_ROPE) < (QK_ROPE // 2)
    partner = jnp.where(first_half,
                        pltpu.roll(x, LANES - QK_ROPE // 2, axis=1),
                        pltpu.roll(x, QK_ROPE // 2, axis=1))
    return x * cos_t + partner * sin_t


def _latent_kv(ckvr, kv_g_ref, w_kb_ref, w_vbt_ref, cos_t, sin_t):
    ckv_n = _rms(ckvr[:, :KV_LORA], kv_g_ref[...]).astype(BF16)
    k_nope = _dot(ckv_n, w_kb_ref[...])
    v_t = _dot_nt(w_vbt_ref[...], ckv_n)
    k_rope = _rope_lanes(ckvr[:, KV_LORA:], cos_t, sin_t)
    return k_nope, v_t, k_rope


def _window_sum(xg, w):
    y = xg
    span = 1
    while span < w:
        y = y + pltpu.roll(y, span, axis=0)
        span *= 2
    return y


def _denominator_rows(n):
    return (lax.broadcasted_iota(jnp.int32, (BF16_ROWS, n), 0) == 0).astype(BF16)


def _fused_kernel(tiles_per_seq, x_ref, meta_ref, kcos_ref, ksin_ref, mcos_ref, msin_ref, qcos_ref, qsin_ref,
                  norm_g_ref, w_in_ref, q_g_ref, w_qb_ref, kv_g_ref, w_kvb_ref,
                  pool_w_f32_ref, pool_scale_ref, w_out_f32_ref, final_g_ref,
                  out_ref,
                  w_in_s, w_qbt_ref, w_kb_ref, w_vbt_ref, pool_w_ref, w_out_ref,
                  k_scr, vt_scr, kmeta_scr, vtmeta_scr, pool_scr, halo_scr,
                  qt_scr, s_scr, m_scr, acc_scr, rise_scr, halo_in_scr):
    b = pl.program_id(0)
    pair = pl.program_id(1)
    tm = ROW_TILE
    first_step = (b == 0) & (pair == 0)

    w_pi_ref = w_in_s.at[:, 0:D_POOL]
    w_pg_ref = w_in_s.at[:, D_POOL:O_CQ]
    w_cq_ref = w_in_s.at[:, O_CQ:O_CKV]
    w_ckv_ref = w_in_s.at[:, O_CKV:S_AG]
    w_ag_ref = w_in_s.at[:, S_AG:S_AG + D_ATTN]

    @pl.when(first_step)
    def _prepare_weights():
        for c in range(O_KR // LANES):
            w_in_s[:, c * LANES:(c + 1) * LANES] = w_in_ref[c * LANES:(c + 1) * LANES, :].T.astype(BF16)
        k_rope_rows = w_in_ref[O_KR:O_AG, :]
        w_in_s[:, O_KR:S_AG] = jnp.concatenate([k_rope_rows, k_rope_rows], axis=0).T.astype(BF16)
        for c in range(D_ATTN // LANES):
            w_in_s[:, S_AG + c * LANES:S_AG + (c + 1) * LANES] = (
                w_in_ref[O_AG + c * LANES:O_AG + (c + 1) * LANES, :].T.astype(BF16))
        quarter = D_MODEL // 4
        for r in range(4):
            rs = slice(r * quarter, (r + 1) * quarter)
            w_out_ref[rs, :] = w_out_f32_ref[rs, :].astype(BF16)
        w_qbt_ref[...] = w_qb_ref[...].T.astype(BF16)
        for h in range(N_HEADS):
            base = h * (QK_NOPE + V_HEAD)
            w_kb_ref[:, h * QK_NOPE:(h + 1) * QK_NOPE] = w_kvb_ref[:, base:base + QK_NOPE].astype(BF16)
            w_vbt_ref[h * V_HEAD:(h + 1) * V_HEAD, :] = w_kvb_ref[:, base + QK_NOPE:base + QK_NOPE + V_HEAD].T.astype(BF16)
        pool_w_ref[...] = pool_w_f32_ref[...].astype(BF16)

    @pl.when(first_step)
    def _meta_prologue():
        xn = _rms(meta_ref[...], norm_g_ref[...]).astype(BF16)
        halo_scr[...] = _dot(xn, w_pi_ref[...])
        k_nope, v_t, k_rope = _latent_kv(_dot(xn, w_ckv_ref[...]), kv_g_ref, w_kb_ref, w_vbt_ref,
                                         mcos_ref[...], msin_ref[...])
        for h in range(N_HEADS):
            kmeta_scr[h, :, 0:LANES] = k_nope[:, h * QK_NOPE:(h + 1) * QK_NOPE].astype(BF16)
            kmeta_scr[h, :, LANES:QK_PAD] = k_rope.astype(BF16)
            vtmeta_scr[h, 0:V_HEAD, :] = v_t[h * V_HEAD:(h + 1) * V_HEAD, :].astype(BF16)
            vtmeta_scr[h, V_HEAD:V_AUG, :] = _denominator_rows(N_META)

    @pl.when(pair == 0)
    def _seed_halo():
        pool_scr[0:HALO, :] = halo_scr[...]

    halo_in_scr[...] = pool_scr[0:HALO, :]

    def _advance_halo():
        pool_scr[0:HALO, :] = pool_scr[tm:tm + HALO, :]

    def _process_tile(n_earlier, sub):
        i = pair * TILES_PER_STEP + sub
        step_rows = slice(sub * tm, (sub + 1) * tm)
        x_tile = x_ref.at[step_rows, :]
        out_tile = out_ref.at[step_rows, :]
        hm = tm // 2
        halves = (slice(0, hm), slice(hm, tm))
        xn_halves = [_rms(x_tile[r, :], norm_g_ref[...]).astype(BF16) for r in halves]

        for r, xn_half in zip(halves, xn_halves):
            pool_scr[HALO + r.start:HALO + r.stop, :] = _dot(xn_half, w_pi_ref[...])
        xn = jnp.concatenate(xn_halves, axis=0)
        cq = _dot(xn, w_cq_ref[...])
        ckvr = _dot(xn, w_ckv_ref[...])

        pooled = []
        for g, w in enumerate(POOL_WINDOWS):
            xg = pool_scr[:, g * POOL_GROUP:(g + 1) * POOL_GROUP]
            win = _window_sum(xg, w)
            pooled.append((win[HALO:, :] * (1.0 / w) - xg[HALO:, :]).astype(BF16))
        mixed = [_dot(pooled[g], pool_w_ref[g]) for g in range(len(POOL_WINDOWS))]
        pool_gate = _dot(xn, w_pg_ref[...])

        cq_n = _rms(cq, q_g_ref[...]).astype(BF16)
        q_t = _dot_nt(w_qbt_ref[...], cq_n) * Q_SCALE
        k_nope, v_t, k_rope = _latent_kv(ckvr, kv_g_ref, w_kb_ref, w_vbt_ref,
                                         kcos_ref[step_rows, :], ksin_ref[step_rows, :])

        row0 = pl.multiple_of(i * tm, tm)
        tile_rows = pl.ds(row0, tm)
        k_rope_bf = k_rope.astype(BF16)
        denominator_rows = _denominator_rows(tm)
        for h in range(N_HEADS):
            k_scr[h, tile_rows, 0:LANES] = k_nope[:, h * QK_NOPE:(h + 1) * QK_NOPE].astype(BF16)
            k_scr[h, tile_rows, LANES:QK_PAD] = k_rope_bf
            vt_scr[h, 0:V_HEAD, tile_rows] = v_t[h * V_HEAD:(h + 1) * V_HEAD, :].astype(BF16)
            vt_scr[h, V_HEAD:V_AUG, tile_rows] = denominator_rows

        cos_q = qcos_ref[:, step_rows]
        sin_q = qsin_ref[:, step_rows]
        half = QK_ROPE // 2
        for h in range(N_HEADS):
            base = h * (QK_NOPE + QK_ROPE)
            x1 = q_t[base + QK_NOPE:base + QK_NOPE + half, :]
            x2 = q_t[base + QK_NOPE + half:base + QK_NOPE + QK_ROPE, :]
            qt_scr[h, 0:QK_NOPE, :] = q_t[base:base + QK_NOPE, :].astype(BF16)
            qt_scr[h, QK_NOPE:QK_NOPE + half, :] = (x1 * cos_q - x2 * sin_q).astype(BF16)
            qt_scr[h, QK_NOPE + half:QK_NOPE + QK_ROPE, :] = (x1 * sin_q + x2 * cos_q).astype(BF16)
            qt_scr[h, QK_NOPE + QK_ROPE:QK_PAD, :] = jnp.zeros((QK_PAD - QK_NOPE - QK_ROPE, tm), BF16)

        pool_out = (_silu(pool_gate) * (jnp.concatenate(mixed, axis=1) * pool_scale_ref[...])).astype(BF16)

        def _scores(h, rows):
            return _dot(k_scr[h, rows, :], qt_scr[h])

        def _exact_update(h, blocks):
            m_prev = m_scr[h]
            m_new = m_prev
            for s, _ in blocks:
                m_new = jnp.maximum(m_new, jnp.max(s, axis=0, keepdims=True))
            acc = jnp.exp2(m_prev - m_new) * acc_scr[h]
            for s, vt_blk in blocks:
                acc = acc + _dot(vt_blk, jnp.exp2(s - m_new).astype(BF16))
            acc_scr[h] = acc
            m_scr[h] = m_new

        def _stale_update(h, blocks):
            m_ref = m_scr[h]
            acc = acc_scr[h]
            rise = rise_scr[...]
            lift = jnp.zeros_like(m_ref)
            for s, vt_blk in blocks:
                t = s - m_ref
                top = jnp.max(t, axis=0, keepdims=True)
                acc = acc + _dot(vt_blk, jnp.exp2(t).astype(BF16))
                rise = jnp.maximum(rise, top)
                lift = jnp.maximum(lift, top)
            acc_scr[h] = acc * jnp.exp2(-lift)
            m_scr[h] = m_ref + lift
            rise_scr[...] = rise

        key_chunk = lax.broadcasted_iota(jnp.int32, (tm, tm), 0) // CHUNK
        query_chunk = lax.broadcasted_iota(jnp.int32, (tm, tm), 1) // CHUNK
        diag_valid = key_chunk <= query_chunk

        def _start_from_meta_keys():
            s_meta = [_dot(kmeta_scr[h], qt_scr[h]) for h in range(N_HEADS)]
            for h in range(N_HEADS):
                m_scr[h] = jnp.max(s_meta[h], axis=0, keepdims=True)
                acc_scr[h] = jnp.zeros((V_AUG, tm), F32)
            return s_meta

        def _attend_unrolled(update):
            s_meta = _start_from_meta_keys()
            own_rows = pl.ds(n_earlier * tm, tm)
            units = ([(None, h) for h in range(N_HEADS)]
                     + [(j, h) for j in range(n_earlier) for h in range(N_HEADS)])

            def unit_scores(unit):
                j, h = unit
                if j is None:
                    return jnp.where(diag_valid, _scores(h, own_rows), MASK_VALUE)
                return _scores(h, pl.ds(j * tm, tm))

            s_next = unit_scores(units[0])
            for n, (j, h) in enumerate(units):
                s = s_next
                if n + 1 < len(units):
                    s_next = unit_scores(units[n + 1])
                if n == 0:
                    attn_gate = _dot(xn, w_ag_ref[...])
                if j is None:
                    update(h, [(s_meta[h], vtmeta_scr[h]), (s, vt_scr[h, :, own_rows])])
                else:
                    update(h, [(s, vt_scr[h, :, pl.ds(j * tm, tm)])])
            return attn_gate

        def _attend_looped(update):
            s_meta = _start_from_meta_keys()
            s_next = jnp.where(diag_valid, _scores(0, tile_rows), MASK_VALUE)
            for h in range(N_HEADS):
                s = s_next
                if h + 1 < N_HEADS:
                    s_next = jnp.where(diag_valid, _scores(h + 1, tile_rows), MASK_VALUE)
                else:
                    s_scr[...] = _scores(0, pl.ds(0, tm))
                update(h, [(s_meta[h], vtmeta_scr[h]), (s, vt_scr[h, :, tile_rows])])

            def _earlier_tile(j, carry):
                rows = pl.ds(pl.multiple_of(j * tm, tm), tm)
                next_rows = pl.ds(pl.multiple_of((j + 1) * tm, tm), tm)
                s_next = s_scr[...]
                for h in range(N_HEADS):
                    s = s_next
                    if h + 1 < N_HEADS:
                        s_next = _scores(h + 1, rows)
                    else:
                        s_scr[...] = _scores(0, next_rows)
                    update(h, [(s, vt_scr[h, :, rows])])
                return carry

            lax.fori_loop(0, i, _earlier_tile, 0)
            return _dot(xn, w_ag_ref[...])

        attn_gate = _attend_looped(_exact_update) if n_earlier is None else _attend_unrolled(_stale_update)

        mix_pool = _dot(pool_out, w_out_ref[0:D_POOL, :])
        attn = []
        for h in range(N_HEADS):
            acc = acc_scr[h]
            attn.append((acc[0:V_HEAD, :] * (1.0 / acc[V_HEAD:V_HEAD + 1, :])).T)
        attn_out = (_silu(attn_gate) * jnp.concatenate(attn, axis=1)).astype(BF16)
        for r in halves:
            mix = mix_pool[r, :] + _dot(attn_out[r, :], w_out_ref[D_POOL:, :])
            out_tile[r, :] = _rms(x_tile[r, :] + mix, final_g_ref[...])

    rise_scr[...] = jnp.full((1, tm), MASK_VALUE, F32)
    for k in range(tiles_per_seq // TILES_PER_STEP):
        @pl.when(pair == k)
        def _tiles_at(k=k):
            for sub in range(TILES_PER_STEP):
                if sub:
                    _advance_halo()
                _process_tile(k * TILES_PER_STEP + sub, sub)

    @pl.when(jnp.max(rise_scr[...]) >= EXP_HEADROOM)
    def _redo_exactly():
        pool_scr[0:HALO, :] = halo_in_scr[...]
        for sub in range(TILES_PER_STEP):
            if sub:
                _advance_halo()
            _process_tile(None, sub)

    _advance_halo()


def _rope_tables(length):
    half = QK_ROPE // 2
    inv_freq = 1.0 / (ROPE_THETA ** (np.arange(half, dtype=np.float64) / half))
    ang = np.arange(length, dtype=np.float64)[:, None] * inv_freq[None, :]
    return np.cos(ang), np.sin(ang)


def kernel(x, meta_tokens, norm_g, w_in, q_norm_g, w_q_b, kv_norm_g, w_kv_b, pool_w, pool_scale, w_out, final_norm_g):
    batch, seq, d_model = x.shape
    assert d_model == D_MODEL and seq % (TILES_PER_STEP * ROW_TILE) == 0 and meta_tokens.shape == (N_META, D_MODEL)
    assert norm_g.shape[0] == 1 and w_in.shape == (1, D_MODEL, D_IN), "single-layer block"
    tm = ROW_TILE

    cos, sin = _rope_tables(N_META + seq)
    kcos = np.tile(cos, (1, LANES // (QK_ROPE // 2))).astype(np.float32)
    ksin = np.tile(np.concatenate([-sin, sin], axis=1), (1, LANES // QK_ROPE)).astype(np.float32)
    qcos = np.ascontiguousarray(cos[N_META:].T).astype(np.float32)
    qsin = np.ascontiguousarray(sin[N_META:].T).astype(np.float32)

    def whole(a):
        return pl.BlockSpec(a.shape, lambda b, i: (0,) * a.ndim)

    def fetched_once(a, squeeze_layer):
        shape = ((None,) + a.shape[1:]) if squeeze_layer else a.shape
        return pl.BlockSpec(shape, lambda b, i: (0,) * a.ndim, pipeline_mode=pl.Buffered(1))

    step_rows = TILES_PER_STEP * tm
    row_spec = pl.BlockSpec((None, step_rows, D_MODEL), lambda b, i: (b, i, 0))
    ktable_spec = pl.BlockSpec((step_rows, LANES), lambda b, i: (i, 0))
    qtable_spec = pl.BlockSpec((QK_ROPE // 2, step_rows), lambda b, i: (0, i))

    w_in_t = jnp.transpose(w_in[0])

    operands = [
        (x, row_spec),
        (meta_tokens, None),
        (kcos[N_META:], ktable_spec),
        (ksin[N_META:], ktable_spec),
        (kcos[:N_META], None),
        (ksin[:N_META], None),
        (qcos, qtable_spec),
        (qsin, qtable_spec),
        (norm_g, None),
        (w_in_t, fetched_once(w_in_t, False)),
        (q_norm_g, None), (w_q_b, fetched_once(w_q_b, True)),
        (kv_norm_g, None), (w_kv_b, fetched_once(w_kv_b, True)),
        (pool_w, fetched_once(pool_w, True)),
        (pool_scale, None),
        (w_out, fetched_once(w_out, True)),
        (final_norm_g.reshape(1, D_MODEL), None),
    ]
    args = [a for a, _ in operands]
    in_specs = [whole(a) if s is None else s for a, s in operands]

    return pl.pallas_call(
        functools.partial(_fused_kernel, seq // tm),
        out_shape=jax.ShapeDtypeStruct(x.shape, x.dtype),
        grid=(batch, seq // step_rows),
        in_specs=in_specs,
        out_specs=row_spec,
        scratch_shapes=[
            pltpu.VMEM((D_MODEL, D_IN_PADDED), BF16),
            pltpu.VMEM((N_HEADS * (QK_NOPE + QK_ROPE), Q_LORA), BF16),
            pltpu.VMEM((KV_LORA, N_HEADS * QK_NOPE), BF16),
            pltpu.VMEM((N_HEADS * V_HEAD, KV_LORA), BF16),
            pltpu.VMEM((len(POOL_WINDOWS), POOL_GROUP, POOL_GROUP), BF16),
            pltpu.VMEM((D_MODEL, D_MODEL), BF16),
            pltpu.VMEM((N_HEADS, seq, QK_PAD), BF16),
            pltpu.VMEM((N_HEADS, V_AUG, seq), BF16),
            pltpu.VMEM((N_HEADS, N_META, QK_PAD), BF16),
            pltpu.VMEM((N_HEADS, V_AUG, N_META), BF16),
            pltpu.VMEM((HALO + tm, D_POOL), F32),
            pltpu.VMEM((HALO, D_POOL), F32),
            pltpu.VMEM((N_HEADS, QK_PAD, tm), BF16),
            pltpu.VMEM((tm, tm), F32),
            pltpu.VMEM((N_HEADS, 1, tm), F32),
            pltpu.VMEM((N_HEADS, V_AUG, tm), F32),
            pltpu.VMEM((1, tm), F32),
            pltpu.VMEM((HALO, D_POOL), F32),
        ],
        compiler_params=pltpu.CompilerParams(
            dimension_semantics=("arbitrary", "arbitrary"),
            vmem_limit_bytes=VMEM_LIMIT_BYTES,
        ),
        name="hybrid_pool_mla_block",
    )(*args)
```

```python
import functools
import math

import numpy as np
import jax
import jax.numpy as jnp
from jax import lax
from jax.experimental import pallas as pl
from jax.experimental.pallas import tpu as pltpu

D_MODEL = 1024
N_META = 16
CHUNK = 64
D_POOL = 512
POOL_WINDOWS = (2, 4, 8, 16)
POOL_GROUP = 128
N_HEADS = 4
QK_NOPE = 128
QK_ROPE = 64
V_HEAD = 128
D_ATTN = N_HEADS * V_HEAD
Q_LORA = 256
KV_LORA = 128
ROPE_THETA = 10000.0
EPS = 1e-6
Q_SCALE = (QK_NOPE + QK_ROPE) ** -0.5 * math.log2(math.e)

O_CQ = 2 * D_POOL
O_CKV = O_CQ + Q_LORA
O_KR = O_CKV + KV_LORA
O_AG = O_KR + QK_ROPE
D_IN = O_AG + D_ATTN
S_AG = O_AG + QK_ROPE
D_IN_PADDED = S_AG + D_ATTN

LANES = 128
BF16_ROWS = 16
QK_PAD = 2 * LANES
V_AUG = V_HEAD + BF16_ROWS
HALO = 16
ROW_TILE = 512
EXP_HEADROOM = 64.0
MASK_VALUE = -1e30
VMEM_LIMIT_BYTES = 56 * 1024 * 1024

F32 = jnp.float32
BF16 = jnp.bfloat16


def _dot(a, b):
    return jnp.dot(a, b, preferred_element_type=F32)


def _dot_nt(a, b):
    return lax.dot_general(a, b, (((1,), (1,)), ((), ())), preferred_element_type=F32)


def _rms(x, g):
    return x * lax.rsqrt(jnp.mean(x * x, axis=-1, keepdims=True) + EPS) * g


def _silu(x):
    h = 0.5 * x
    return h + h * jnp.tanh(h)


def _rope_lanes(x, cos_t, sin_t):
    lane = lax.broadcasted_iota(jnp.int32, x.shape, 1)
    first_half = (lane % QK_ROPE) < (QK_ROPE // 2)
    partner = jnp.where(first_half,
                        pltpu.roll(x, LANES - QK_ROPE // 2, axis=1),
                        pltpu.roll(x, QK_ROPE // 2, axis=1))
    return x * cos_t + partner * sin_t


def _latent_kv(ckvr, kv_g_ref, w_kb_ref, w_vbt_ref, cos_t, sin_t):
    ckv_n = _rms(ckvr[:, :KV_LORA], kv_g_ref[...]).astype(BF16)
    k_nope = _dot(ckv_n, w_kb_ref[...])
    v_t = _dot_nt(w_vbt_ref[...], ckv_n)
    k_rope = _rope_lanes(ckvr[:, KV_LORA:], cos_t, sin_t)
    return k_nope, v_t, k_rope


def _window_sum(xg, w):
    y = xg
    span = 1
    while span < w:
        y = y + pltpu.roll(y, span, axis=0)
        span *= 2
    return y


def _denominator_rows(n):
    return (lax.broadcasted_iota(jnp.int32, (BF16_ROWS, n), 0) == 0).astype(BF16)


def _fused_kernel(tiles_per_seq, x_ref, meta_ref, kcos_ref, ksin_ref, mcos_ref, msin_ref, qcos_ref, qsin_ref,
                  norm_g_ref, w_in_ref, q_g_ref, w_qb_ref, kv_g_ref, w_kvb_ref,
                  pool_w_f32_ref, pool_scale_ref, w_out_f32_ref, final_g_ref,
                  out_ref,
                  w_in_s, w_qbt_ref, w_kb_ref, w_vbt_ref, pool_w_ref, w_out_ref,
                  k_scr, vt_scr, kmeta_scr, vtmeta_scr, pool_scr, halo_scr,
                  qt_scr, s_scr, m_scr, acc_scr, rise_scr):
    b = pl.program_id(0)
    i = pl.program_id(1)
    tm = ROW_TILE
    first_step = (b == 0) & (i == 0)

    w_pi_ref = w_in_s.at[:, 0:D_POOL]
    w_pg_ref = w_in_s.at[:, D_POOL:O_CQ]
    w_cq_ref = w_in_s.at[:, O_CQ:O_CKV]
    w_ckv_ref = w_in_s.at[:, O_CKV:S_AG]
    w_ag_ref = w_in_s.at[:, S_AG:S_AG + D_ATTN]

    @pl.when(first_step)
    def _prepare_weights():
        for c in range(O_KR // LANES):
            w_in_s[:, c * LANES:(c + 1) * LANES] = w_in_ref[c * LANES:(c + 1) * LANES, :].T.astype(BF16)
        k_rope_rows = w_in_ref[O_KR:O_AG, :]
        w_in_s[:, O_KR:S_AG] = jnp.concatenate([k_rope_rows, k_rope_rows], axis=0).T.astype(BF16)
        for c in range(D_ATTN // LANES):
            w_in_s[:, S_AG + c * LANES:S_AG + (c + 1) * LANES] = (
                w_in_ref[O_AG + c * LANES:O_AG + (c + 1) * LANES, :].T.astype(BF16))
        quarter = D_MODEL // 4
        for r in range(4):
            rs = slice(r * quarter, (r + 1) * quarter)
            w_out_ref[rs, :] = w_out_f32_ref[rs, :].astype(BF16)
        w_qbt_ref[...] = w_qb_ref[...].T.astype(BF16)
        for h in range(N_HEADS):
            base = h * (QK_NOPE + V_HEAD)
            w_kb_ref[:, h * QK_NOPE:(h + 1) * QK_NOPE] = w_kvb_ref[:, base:base + QK_NOPE].astype(BF16)
            w_vbt_ref[h * V_HEAD:(h + 1) * V_HEAD, :] = w_kvb_ref[:, base + QK_NOPE:base + QK_NOPE + V_HEAD].T.astype(BF16)
        pool_w_ref[...] = pool_w_f32_ref[...].astype(BF16)

    @pl.when(first_step)
    def _meta_prologue():
        xn = _rms(meta_ref[...], norm_g_ref[...]).astype(BF16)
        halo_scr[...] = _dot(xn, w_pi_ref[...])
        k_nope, v_t, k_rope = _latent_kv(_dot(xn, w_ckv_ref[...]), kv_g_ref, w_kb_ref, w_vbt_ref,
                                         mcos_ref[...], msin_ref[...])
        for h in range(N_HEADS):
            kmeta_scr[h, :, 0:LANES] = k_nope[:, h * QK_NOPE:(h + 1) * QK_NOPE].astype(BF16)
            kmeta_scr[h, :, LANES:QK_PAD] = k_rope.astype(BF16)
            vtmeta_scr[h, 0:V_HEAD, :] = v_t[h * V_HEAD:(h + 1) * V_HEAD, :].astype(BF16)
            vtmeta_scr[h, V_HEAD:V_AUG, :] = _denominator_rows(N_META)

    @pl.when(i == 0)
    def _seed_halo():
        pool_scr[0:HALO, :] = halo_scr[...]

    def _process_tile(n_earlier):
        hm = tm // 2
        halves = (slice(0, hm), slice(hm, tm))
        xn_halves = [_rms(x_ref[r, :], norm_g_ref[...]).astype(BF16) for r in halves]

        for r, xn_half in zip(halves, xn_halves):
            pool_scr[HALO + r.start:HALO + r.stop, :] = _dot(xn_half, w_pi_ref[...])
        xn = jnp.concatenate(xn_halves, axis=0)
        cq = _dot(xn, w_cq_ref[...])
        ckvr = _dot(xn, w_ckv_ref[...])

        pooled = []
        for g, w in enumerate(POOL_WINDOWS):
            xg = pool_scr[:, g * POOL_GROUP:(g + 1) * POOL_GROUP]
            win = _window_sum(xg, w)
            pooled.append((win[HALO:, :] * (1.0 / w) - xg[HALO:, :]).astype(BF16))
        mixed = [_dot(pooled[g], pool_w_ref[g]) for g in range(len(POOL_WINDOWS))]
        pool_gate = _dot(xn, w_pg_ref[...])

        cq_n = _rms(cq, q_g_ref[...]).astype(BF16)
        q_t = _dot_nt(w_qbt_ref[...], cq_n) * Q_SCALE
        k_nope, v_t, k_rope = _latent_kv(ckvr, kv_g_ref, w_kb_ref, w_vbt_ref, kcos_ref[...], ksin_ref[...])

        row0 = pl.multiple_of(i * tm, tm)
        tile_rows = pl.ds(row0, tm)
        k_rope_bf = k_rope.astype(BF16)
        denominator_rows = _denominator_rows(tm)
        for h in range(N_HEADS):
            k_scr[h, tile_rows, 0:LANES] = k_nope[:, h * QK_NOPE:(h + 1) * QK_NOPE].astype(BF16)
            k_scr[h, tile_rows, LANES:QK_PAD] = k_rope_bf
            vt_scr[h, 0:V_HEAD, tile_rows] = v_t[h * V_HEAD:(h + 1) * V_HEAD, :].astype(BF16)
            vt_scr[h, V_HEAD:V_AUG, tile_rows] = denominator_rows

        cos_q = qcos_ref[...]
        sin_q = qsin_ref[...]
        half = QK_ROPE // 2
        for h in range(N_HEADS):
            base = h * (QK_NOPE + QK_ROPE)
            x1 = q_t[base + QK_NOPE:base + QK_NOPE + half, :]
            x2 = q_t[base + QK_NOPE + half:base + QK_NOPE + QK_ROPE, :]
            qt_scr[h, 0:QK_NOPE, :] = q_t[base:base + QK_NOPE, :].astype(BF16)
            qt_scr[h, QK_NOPE:QK_NOPE + half, :] = (x1 * cos_q - x2 * sin_q).astype(BF16)
            qt_scr[h, QK_NOPE + half:QK_NOPE + QK_ROPE, :] = (x1 * sin_q + x2 * cos_q).astype(BF16)
            qt_scr[h, QK_NOPE + QK_ROPE:QK_PAD, :] = jnp.zeros((QK_PAD - QK_NOPE - QK_ROPE, tm), BF16)

        pool_out = (_silu(pool_gate) * (jnp.concatenate(mixed, axis=1) * pool_scale_ref[...])).astype(BF16)

        def _scores(h, rows):
            return _dot(k_scr[h, rows, :], qt_scr[h])

        def _exact_update(h, blocks):
            m_prev = m_scr[h]
            m_new = m_prev
            for s, _ in blocks:
                m_new = jnp.maximum(m_new, jnp.max(s, axis=0, keepdims=True))
            acc = jnp.exp2(m_prev - m_new) * acc_scr[h]
            for s, vt_blk in blocks:
                acc = acc + _dot(vt_blk, jnp.exp2(s - m_new).astype(BF16))
            acc_scr[h] = acc
            m_scr[h] = m_new

        def _stale_update(h, blocks, cols=slice(0, tm)):
            m_ref = m_scr[h, :, cols]
            acc = acc_scr[h, :, cols]
            rise = rise_scr[:, cols]
            lift = jnp.zeros_like(m_ref)
            for s, vt_blk in blocks:
                t = s - m_ref
                top = jnp.max(t, axis=0, keepdims=True)
                acc = acc + _dot(vt_blk, jnp.exp2(t).astype(BF16))
                rise = jnp.maximum(rise, top)
                lift = jnp.maximum(lift, top)
            acc_scr[h, :, cols] = acc * jnp.exp2(-lift)
            m_scr[h, :, cols] = m_ref + lift
            rise_scr[:, cols] = rise

        key_chunk = lax.broadcasted_iota(jnp.int32, (tm, tm), 0) // CHUNK
        query_chunk = lax.broadcasted_iota(jnp.int32, (tm, tm), 1) // CHUNK
        diag_valid = key_chunk <= query_chunk

        def _start_from_meta_keys():
            s_meta = [_dot(kmeta_scr[h], qt_scr[h]) for h in range(N_HEADS)]
            for h in range(N_HEADS):
                m_scr[h] = jnp.max(s_meta[h], axis=0, keepdims=True)
                acc_scr[h] = jnp.zeros((V_AUG, tm), F32)
            return s_meta

        def _attend_unrolled(update):
            s_meta = _start_from_meta_keys()
            hq = tm // 2
            near_rows = pl.ds(n_earlier * tm, hq)
            far_rows = pl.ds(n_earlier * tm + hq, hq)
            late_queries = slice(hq, tm)
            near_valid = diag_valid[0:hq, :]
            far_valid = diag_valid[hq:, hq:]
            units = []
            for pair in range(0, N_HEADS, 2):
                units += [("near", h) for h in (pair, pair + 1)] + [("far", h) for h in (pair, pair + 1)]
            units += [(j, h) for j in range(n_earlier) for h in range(N_HEADS)]

            def unit_scores(unit):
                j, h = unit
                if j == "near":
                    return jnp.where(near_valid, _scores(h, near_rows), MASK_VALUE)
                if j == "far":
                    return jnp.where(far_valid, _dot(k_scr[h, far_rows, :], qt_scr[h, :, late_queries]),
                                     MASK_VALUE)
                return _scores(h, pl.ds(j * tm, tm))

            s_next = unit_scores(units[0])
            for n, (j, h) in enumerate(units):
                s = s_next
                if n + 1 < len(units):
                    s_next = unit_scores(units[n + 1])
                if n == 0:
                    attn_gate = _dot(xn, w_ag_ref[...])
                if j == "near":
                    update(h, [(s_meta[h], vtmeta_scr[h]), (s, vt_scr[h, :, near_rows])])
                elif j == "far":
                    update(h, [(s, vt_scr[h, :, far_rows])], late_queries)
                else:
                    update(h, [(s, vt_scr[h, :, pl.ds(j * tm, tm)])])
            return attn_gate

        def _attend_looped(update):
            s_meta = _start_from_meta_keys()
            s_next = jnp.where(diag_valid, _scores(0, tile_rows), MASK_VALUE)
            for h in range(N_HEADS):
                s = s_next
                if h + 1 < N_HEADS:
                    s_next = jnp.where(diag_valid, _scores(h + 1, tile_rows), MASK_VALUE)
                else:
                    s_scr[...] = _scores(0, pl.ds(0, tm))
                update(h, [(s_meta[h], vtmeta_scr[h]), (s, vt_scr[h, :, tile_rows])])

            def _earlier_tile(j, carry):
                rows = pl.ds(pl.multiple_of(j * tm, tm), tm)
                next_rows = pl.ds(pl.multiple_of((j + 1) * tm, tm), tm)
                s_next = s_scr[...]
                for h in range(N_HEADS):
                    s = s_next
                    if h + 1 < N_HEADS:
                        s_next = _scores(h + 1, rows)
                    else:
                        s_scr[...] = _scores(0, next_rows)
                    update(h, [(s, vt_scr[h, :, rows])])
                return carry

            lax.fori_loop(0, i, _earlier_tile, 0)
            return _dot(xn, w_ag_ref[...])

        attn_gate = _attend_looped(_exact_update) if n_earlier is None else _attend_unrolled(_stale_update)

        mix_pool = _dot(pool_out, w_out_ref[0:D_POOL, :])
        attn = []
        for h in range(N_HEADS):
            acc = acc_scr[h]
            attn.append((acc[0:V_HEAD, :] * (1.0 / acc[V_HEAD:V_HEAD + 1, :])).T)
        attn_out = (_silu(attn_gate) * jnp.concatenate(attn, axis=1)).astype(BF16)
        for r in halves:
            mix = mix_pool[r, :] + _dot(attn_out[r, :], w_out_ref[D_POOL:, :])
            out_ref[r, :] = _rms(x_ref[r, :] + mix, final_g_ref[...])

    rise_scr[...] = jnp.full((1, tm), MASK_VALUE, F32)
    for n_earlier in range(tiles_per_seq):
        @pl.when(i == n_earlier)
        def _tile_at(n_earlier=n_earlier):
            _process_tile(n_earlier)

    @pl.when(jnp.max(rise_scr[...]) >= EXP_HEADROOM)
    def _redo_exactly():
        _process_tile(None)

    pool_scr[0:HALO, :] = pool_scr[tm:tm + HALO, :]


def _rope_tables(length):
    half = QK_ROPE // 2
    inv_freq = 1.0 / (ROPE_THETA ** (np.arange(half, dtype=np.float64) / half))
    ang = np.arange(length, dtype=np.float64)[:, None] * inv_freq[None, :]
    return np.cos(ang), np.sin(ang)


def kernel(x, meta_tokens, norm_g, w_in, q_norm_g, w_q_b, kv_norm_g, w_kv_b, pool_w, pool_scale, w_out, final_norm_g):
    batch, seq, d_model = x.shape
    assert d_model == D_MODEL and seq % ROW_TILE == 0 and meta_tokens.shape == (N_META, D_MODEL)
    assert norm_g.shape[0] == 1 and w_in.shape == (1, D_MODEL, D_IN), "single-layer block"
    tm = ROW_TILE

    cos, sin = _rope_tables(N_META + seq)
    kcos = np.tile(cos, (1, LANES // (QK_ROPE // 2))).astype(np.float32)
    ksin = np.tile(np.concatenate([-sin, sin], axis=1), (1, LANES // QK_ROPE)).astype(np.float32)
    qcos = np.ascontiguousarray(cos[N_META:].T).astype(np.float32)
    qsin = np.ascontiguousarray(sin[N_META:].T).astype(np.float32)

    def whole(a):
        return pl.BlockSpec(a.shape, lambda b, i: (0,) * a.ndim)

    def fetched_once(a, squeeze_layer):
        shape = ((None,) + a.shape[1:]) if squeeze_layer else a.shape
        return pl.BlockSpec(shape, lambda b, i: (0,) * a.ndim, pipeline_mode=pl.Buffered(1))

    row_spec = pl.BlockSpec((None, tm, D_MODEL), lambda b, i: (b, i, 0))
    ktable_spec = pl.BlockSpec((tm, LANES), lambda b, i: (i, 0))
    qtable_spec = pl.BlockSpec((QK_ROPE // 2, tm), lambda b, i: (0, i))

    w_in_t = jnp.transpose(w_in[0])

    operands = [
        (x, row_spec),
        (meta_tokens, None),
        (kcos[N_META:], ktable_spec),
        (ksin[N_META:], ktable_spec),
        (kcos[:N_META], None),
        (ksin[:N_META], None),
        (qcos, qtable_spec),
        (qsin, qtable_spec),
        (norm_g, None),
        (w_in_t, fetched_once(w_in_t, False)),
        (q_norm_g, None), (w_q_b, fetched_once(w_q_b, True)),
        (kv_norm_g, None), (w_kv_b, fetched_once(w_kv_b, True)),
        (pool_w, fetched_once(pool_w, True)),
        (pool_scale, None),
        (w_out, fetched_once(w_out, True)),
        (final_norm_g.reshape(1, D_MODEL), None),
    ]
    args = [a for a, _ in operands]
    in_specs = [whole(a) if s is None else s for a, s in operands]

    return pl.pallas_call(
        functools.partial(_fused_kernel, seq // tm),
        out_shape=jax.ShapeDtypeStruct(x.shape, x.dtype),
        grid=(batch, seq // tm),
        in_specs=in_specs,
        out_specs=row_spec,
        scratch_shapes=[
            pltpu.VMEM((D_MODEL, D_IN_PADDED), BF16),
            pltpu.VMEM((N_HEADS * (QK_NOPE + QK_ROPE), Q_LORA), BF16),
            pltpu.VMEM((KV_LORA, N_HEADS * QK_NOPE), BF16),
            pltpu.VMEM((N_HEADS * V_HEAD, KV_LORA), BF16),
            pltpu.VMEM((len(POOL_WINDOWS), POOL_GROUP, POOL_GROUP), BF16),
            pltpu.VMEM((D_MODEL, D_MODEL), BF16),
            pltpu.VMEM((N_HEADS, seq, QK_PAD), BF16),
            pltpu.VMEM((N_HEADS, V_AUG, seq), BF16),
            pltpu.VMEM((N_HEADS, N_META, QK_PAD), BF16),
            pltpu.VMEM((N_HEADS, V_AUG, N_META), BF16),
            pltpu.VMEM((HALO + tm, D_POOL), F32),
            pltpu.VMEM((HALO, D_POOL), F32),
            pltpu.VMEM((N_HEADS, QK_PAD, tm), BF16),
            pltpu.VMEM((tm, tm), F32),
            pltpu.VMEM((N_HEADS, 1, tm), F32),
            pltpu.VMEM((N_HEADS, V_AUG, tm), F32),
            pltpu.VMEM((1, tm), F32),
        ],
        compiler_params=pltpu.CompilerParams(
            dimension_semantics=("arbitrary", "arbitrary"),
            vmem_limit_bytes=VMEM_LIMIT_BYTES,
        ),
        name="hybrid_pool_mla_block",
    )(*args)
```

```python
import functools
import math

import numpy as np
import jax
import jax.numpy as jnp
from jax import lax
from jax.experimental import pallas as pl
from jax.experimental.pallas import tpu as pltpu

D_MODEL = 1024
N_META = 16
CHUNK = 64
D_POOL = 512
POOL_WINDOWS = (2, 4, 8, 16)
POOL_GROUP = 128
N_HEADS = 4
QK_NOPE = 128
QK_ROPE = 64
V_HEAD = 128
D_ATTN = N_HEADS * V_HEAD
Q_LORA = 256
KV_LORA = 128
ROPE_THETA = 10000.0
EPS = 1e-6
Q_SCALE = (QK_NOPE + QK_ROPE) ** -0.5 * math.log2(math.e)

O_CQ = 2 * D_POOL
O_CKV = O_CQ + Q_LORA
O_KR = O_CKV + KV_LORA
O_AG = O_KR + QK_ROPE
D_IN = O_AG + D_ATTN
S_AG = O_AG + QK_ROPE
D_IN_PADDED = S_AG + D_ATTN

LANES = 128
BF16_ROWS = 16
QK_PAD = 2 * LANES
V_AUG = V_HEAD + BF16_ROWS
HALO = 16
ROW_TILE = 512
EXP_HEADROOM = 64.0
MASK_VALUE = -1e30
VMEM_LIMIT_BYTES = 56 * 1024 * 1024

F32 = jnp.float32
BF16 = jnp.bfloat16


def _dot(a, b):
    return jnp.dot(a, b, preferred_element_type=F32)


def _dot_nt(a, b):
    return lax.dot_general(a, b, (((1,), (1,)), ((), ())), preferred_element_type=F32)


def _rms(x, g):
    return x * lax.rsqrt(jnp.mean(x * x, axis=-1, keepdims=True) + EPS) * g


def _silu(x):
    h = 0.5 * x
    return h + h * jnp.tanh(h)


def _rope_lanes(x, cos_t, sin_t):
    lane = lax.broadcasted_iota(jnp.int32, x.shape, 1)
    first_half = (lane % QK_ROPE) < (QK_ROPE // 2)
    partner = jnp.where(first_half,
                        pltpu.roll(x, LANES - QK_ROPE // 2, axis=1),
                        pltpu.roll(x, QK_ROPE // 2, axis=1))
    return x * cos_t + partner * sin_t


def _latent_kv(ckvr, kv_g_ref, w_kb_ref, w_vbt_ref, cos_t, sin_t):
    ckv_n = _rms(ckvr[:, :KV_LORA], kv_g_ref[...]).astype(BF16)
    k_nope = _dot(ckv_n, w_kb_ref[...])
    v_t = _dot_nt(w_vbt_ref[...], ckv_n)
    k_rope = _rope_lanes(ckvr[:, KV_LORA:], cos_t, sin_t)
    return k_nope, v_t, k_rope


def _window_sum(xg, w):
    y = xg
    span = 1
    while span < w:
        y = y + pltpu.roll(y, span, axis=0)
        span *= 2
    return y


def _denominator_rows(n):
    return (lax.broadcasted_iota(jnp.int32, (BF16_ROWS, n), 0) == 0).astype(BF16)


def _fused_kernel(tiles_per_seq, x_ref, meta_ref, kcos_ref, ksin_ref, mcos_ref, msin_ref, qcos_ref, qsin_ref,
                  norm_g_ref, w_in_ref, q_g_ref, w_qb_ref, kv_g_ref, w_kvb_ref,
                  pool_w_f32_ref, pool_scale_ref, w_out_f32_ref, final_g_ref,
                  out_ref,
                  w_in_s, w_qbt_ref, w_kb_ref, w_vbt_ref, pool_w_ref, w_out_ref,
                  k_scr, vt_scr, kmeta_scr, vtmeta_scr, pool_scr, halo_scr,
                  qt_scr, s_scr, m_scr, acc_scr, rise_scr, worst_rise_ref):
    b = pl.program_id(0)
    i = pl.program_id(1)
    tm = ROW_TILE
    first_step = (b == 0) & (i == 0)

    w_pi_ref = w_in_s.at[:, 0:D_POOL]
    w_pg_ref = w_in_s.at[:, D_POOL:O_CQ]
    w_cq_ref = w_in_s.at[:, O_CQ:O_CKV]
    w_ckv_ref = w_in_s.at[:, O_CKV:S_AG]
    w_ag_ref = w_in_s.at[:, S_AG:S_AG + D_ATTN]

    @pl.when(first_step)
    def _prepare_weights():
        for c in range(O_KR // LANES):
            w_in_s[:, c * LANES:(c + 1) * LANES] = w_in_ref[c * LANES:(c + 1) * LANES, :].T.astype(BF16)
        k_rope_rows = w_in_ref[O_KR:O_AG, :]
        w_in_s[:, O_KR:S_AG] = jnp.concatenate([k_rope_rows, k_rope_rows], axis=0).T.astype(BF16)
        for c in range(D_ATTN // LANES):
            w_in_s[:, S_AG + c * LANES:S_AG + (c + 1) * LANES] = (
                w_in_ref[O_AG + c * LANES:O_AG + (c + 1) * LANES, :].T.astype(BF16))
        quarter = D_MODEL // 4
        for r in range(4):
            rs = slice(r * quarter, (r + 1) * quarter)
            w_out_ref[rs, :] = w_out_f32_ref[rs, :].astype(BF16)
        w_qbt_ref[...] = w_qb_ref[...].T.astype(BF16)
        for h in range(N_HEADS):
            base = h * (QK_NOPE + V_HEAD)
            w_kb_ref[:, h * QK_NOPE:(h + 1) * QK_NOPE] = w_kvb_ref[:, base:base + QK_NOPE].astype(BF16)
            w_vbt_ref[h * V_HEAD:(h + 1) * V_HEAD, :] = w_kvb_ref[:, base + QK_NOPE:base + QK_NOPE + V_HEAD].T.astype(BF16)
        pool_w_ref[...] = pool_w_f32_ref[...].astype(BF16)

    @pl.when(first_step)
    def _meta_prologue():
        xn = _rms(meta_ref[...], norm_g_ref[...]).astype(BF16)
        halo_scr[...] = _dot(xn, w_pi_ref[...])
        k_nope, v_t, k_rope = _latent_kv(_dot(xn, w_ckv_ref[...]), kv_g_ref, w_kb_ref, w_vbt_ref,
                                         mcos_ref[...], msin_ref[...])
        for h in range(N_HEADS):
            kmeta_scr[h, :, 0:LANES] = k_nope[:, h * QK_NOPE:(h + 1) * QK_NOPE].astype(BF16)
            kmeta_scr[h, :, LANES:QK_PAD] = k_rope.astype(BF16)
            vtmeta_scr[h, 0:V_HEAD, :] = v_t[h * V_HEAD:(h + 1) * V_HEAD, :].astype(BF16)
            vtmeta_scr[h, V_HEAD:V_AUG, :] = _denominator_rows(N_META)

    @pl.when(i == 0)
    def _seed_halo():
        pool_scr[0:HALO, :] = halo_scr[...]

    def _process_tile(n_earlier):
        hm = tm // 2
        halves = (slice(0, hm), slice(hm, tm))
        xn_halves = [_rms(x_ref[r, :], norm_g_ref[...]).astype(BF16) for r in halves]

        for r, xn_half in zip(halves, xn_halves):
            pool_scr[HALO + r.start:HALO + r.stop, :] = _dot(xn_half, w_pi_ref[...])
        xn = jnp.concatenate(xn_halves, axis=0)
        cq = _dot(xn, w_cq_ref[...])
        ckvr = _dot(xn, w_ckv_ref[...])

        pooled = []
        for g, w in enumerate(POOL_WINDOWS):
            xg = pool_scr[:, g * POOL_GROUP:(g + 1) * POOL_GROUP]
            win = _window_sum(xg, w)
            pooled.append((win[HALO:, :] * (1.0 / w) - xg[HALO:, :]).astype(BF16))
        mixed = [_dot(pooled[g], pool_w_ref[g]) for g in range(len(POOL_WINDOWS))]
        pool_gate = _dot(xn, w_pg_ref[...])

        cq_n = _rms(cq, q_g_ref[...]).astype(BF16)
        q_t = _dot_nt(w_qbt_ref[...], cq_n) * Q_SCALE
        k_nope, v_t, k_rope = _latent_kv(ckvr, kv_g_ref, w_kb_ref, w_vbt_ref, kcos_ref[...], ksin_ref[...])

        row0 = pl.multiple_of(i * tm, tm)
        tile_rows = pl.ds(row0, tm)
        k_rope_bf = k_rope.astype(BF16)
        denominator_rows = _denominator_rows(tm)
        for h in range(N_HEADS):
            k_scr[h, tile_rows, 0:LANES] = k_nope[:, h * QK_NOPE:(h + 1) * QK_NOPE].astype(BF16)
            k_scr[h, tile_rows, LANES:QK_PAD] = k_rope_bf
            vt_scr[h, 0:V_HEAD, tile_rows] = v_t[h * V_HEAD:(h + 1) * V_HEAD, :].astype(BF16)
            vt_scr[h, V_HEAD:V_AUG, tile_rows] = denominator_rows

        cos_q = qcos_ref[...]
        sin_q = qsin_ref[...]
        half = QK_ROPE // 2
        for h in range(N_HEADS):
            base = h * (QK_NOPE + QK_ROPE)
            x1 = q_t[base + QK_NOPE:base + QK_NOPE + half, :]
            x2 = q_t[base + QK_NOPE + half:base + QK_NOPE + QK_ROPE, :]
            qt_scr[h, 0:QK_NOPE, :] = q_t[base:base + QK_NOPE, :].astype(BF16)
            qt_scr[h, QK_NOPE:QK_NOPE + half, :] = (x1 * cos_q - x2 * sin_q).astype(BF16)
            qt_scr[h, QK_NOPE + half:QK_NOPE + QK_ROPE, :] = (x1 * sin_q + x2 * cos_q).astype(BF16)
            qt_scr[h, QK_NOPE + QK_ROPE:QK_PAD, :] = jnp.zeros((QK_PAD - QK_NOPE - QK_ROPE, tm), BF16)

        pool_out = (_silu(pool_gate) * (jnp.concatenate(mixed, axis=1) * pool_scale_ref[...])).astype(BF16)

        def _scores(h, rows):
            return _dot(k_scr[h, rows, :], qt_scr[h])

        def _exact_update(h, blocks):
            m_prev = m_scr[h]
            m_new = m_prev
            for s, _ in blocks:
                m_new = jnp.maximum(m_new, jnp.max(s, axis=0, keepdims=True))
            acc = jnp.exp2(m_prev - m_new) * acc_scr[h]
            for s, vt_blk in blocks:
                acc = acc + _dot(vt_blk, jnp.exp2(s - m_new).astype(BF16))
            acc_scr[h] = acc
            m_scr[h] = m_new

        def _stale_update(h, blocks, cols=slice(0, tm)):
            m_ref = m_scr[h, :, cols]
            acc = acc_scr[h, :, cols]
            rise = rise_scr[:, cols]
            lift = jnp.zeros_like(m_ref)
            for s, vt_blk in blocks:
                t = s - m_ref
                top = jnp.max(t, axis=0, keepdims=True)
                acc = acc + _dot(vt_blk, jnp.exp2(t).astype(BF16))
                rise = jnp.maximum(rise, top)
                lift = jnp.maximum(lift, top)
            acc_scr[h, :, cols] = acc * jnp.exp2(-lift)
            m_scr[h, :, cols] = m_ref + lift
            rise_scr[:, cols] = rise

        key_chunk = lax.broadcasted_iota(jnp.int32, (tm, tm), 0) // CHUNK
        query_chunk = lax.broadcasted_iota(jnp.int32, (tm, tm), 1) // CHUNK
        diag_valid = key_chunk <= query_chunk

        def _start_from_meta_keys():
            s_meta = [_dot(kmeta_scr[h], qt_scr[h]) for h in range(N_HEADS)]
            for h in range(N_HEADS):
                m_scr[h] = jnp.max(s_meta[h], axis=0, keepdims=True)
                acc_scr[h] = jnp.zeros((V_AUG, tm), F32)
            return s_meta

        def _attend_unrolled(update):
            s_meta = _start_from_meta_keys()
            hq = tm // 2
            near_rows = pl.ds(n_earlier * tm, hq)
            far_rows = pl.ds(n_earlier * tm + hq, hq)
            late_queries = slice(hq, tm)
            near_valid = diag_valid[0:hq, :]
            far_valid = diag_valid[hq:, hq:]
            units = []
            for pair in range(0, N_HEADS, 2):
                units += [("near", h) for h in (pair, pair + 1)] + [("far", h) for h in (pair, pair + 1)]
            units += [(j, h) for j in range(n_earlier) for h in range(N_HEADS)]

            def unit_scores(unit):
                j, h = unit
                if j == "near":
                    return jnp.where(near_valid, _scores(h, near_rows), MASK_VALUE)
                if j == "far":
                    return jnp.where(far_valid, _dot(k_scr[h, far_rows, :], qt_scr[h, :, late_queries]),
                                     MASK_VALUE)
                return _scores(h, pl.ds(j * tm, tm))

            s_next = unit_scores(units[0])
            for n, (j, h) in enumerate(units):
                s = s_next
                if n + 1 < len(units):
                    s_next = unit_scores(units[n + 1])
                if n == 0:
                    attn_gate = _dot(xn, w_ag_ref[...])
                if j == "near":
                    update(h, [(s_meta[h], vtmeta_scr[h]), (s, vt_scr[h, :, near_rows])])
                elif j == "far":
                    update(h, [(s, vt_scr[h, :, far_rows])], late_queries)
                else:
                    update(h, [(s, vt_scr[h, :, pl.ds(j * tm, tm)])])
            return attn_gate

        def _attend_looped(update):
            s_meta = _start_from_meta_keys()
            s_next = jnp.where(diag_valid, _scores(0, tile_rows), MASK_VALUE)
            for h in range(N_HEADS):
                s = s_next
                if h + 1 < N_HEADS:
                    s_next = jnp.where(diag_valid, _scores(h + 1, tile_rows), MASK_VALUE)
                else:
                    s_scr[...] = _scores(0, pl.ds(0, tm))
                update(h, [(s_meta[h], vtmeta_scr[h]), (s, vt_scr[h, :, tile_rows])])

            def _earlier_tile(j, carry):
                rows = pl.ds(pl.multiple_of(j * tm, tm), tm)
                next_rows = pl.ds(pl.multiple_of((j + 1) * tm, tm), tm)
                s_next = s_scr[...]
                for h in range(N_HEADS):
                    s = s_next
                    if h + 1 < N_HEADS:
                        s_next = _scores(h + 1, rows)
                    else:
                        s_scr[...] = _scores(0, next_rows)
                    update(h, [(s, vt_scr[h, :, rows])])
                return carry

            lax.fori_loop(0, i, _earlier_tile, 0)
            return _dot(xn, w_ag_ref[...])

        attn_gate = _attend_looped(_exact_update) if n_earlier is None else _attend_unrolled(_stale_update)
        if n_earlier is not None:
            worst_rise_ref[0] = jnp.max(rise_scr[...])

        mix_pool = _dot(pool_out, w_out_ref[0:D_POOL, :])
        attn = []
        for h in range(N_HEADS):
            acc = acc_scr[h]
            attn.append((acc[0:V_HEAD, :] * (1.0 / acc[V_HEAD:V_HEAD + 1, :])).T)
        attn_out = (_silu(attn_gate) * jnp.concatenate(attn, axis=1)).astype(BF16)
        for r in halves:
            mix = mix_pool[r, :] + _dot(attn_out[r, :], w_out_ref[D_POOL:, :])
            out_ref[r, :] = _rms(x_ref[r, :] + mix, final_g_ref[...])

    rise_scr[...] = jnp.full((1, tm), MASK_VALUE, F32)
    for n_earlier in range(tiles_per_seq):
        @pl.when(i == n_earlier)
        def _tile_at(n_earlier=n_earlier):
            _process_tile(n_earlier)

    @pl.when(worst_rise_ref[0] >= EXP_HEADROOM)
    def _redo_exactly():
        _process_tile(None)

    pool_scr[0:HALO, :] = pool_scr[tm:tm + HALO, :]


def _rope_tables(length):
    half = QK_ROPE // 2
    inv_freq = 1.0 / (ROPE_THETA ** (np.arange(half, dtype=np.float64) / half))
    ang = np.arange(length, dtype=np.float64)[:, None] * inv_freq[None, :]
    return np.cos(ang), np.sin(ang)


def kernel(x, meta_tokens, norm_g, w_in, q_norm_g, w_q_b, kv_norm_g, w_kv_b, pool_w, pool_scale, w_out, final_norm_g):
    batch, seq, d_model = x.shape
    assert d_model == D_MODEL and seq % ROW_TILE == 0 and meta_tokens.shape == (N_META, D_MODEL)
    assert norm_g.shape[0] == 1 and w_in.shape == (1, D_MODEL, D_IN), "single-layer block"
    tm = ROW_TILE

    cos, sin = _rope_tables(N_META + seq)
    kcos = np.tile(cos, (1, LANES // (QK_ROPE // 2))).astype(np.float32)
    ksin = np.tile(np.concatenate([-sin, sin], axis=1), (1, LANES // QK_ROPE)).astype(np.float32)
    qcos = np.ascontiguousarray(cos[N_META:].T).astype(np.float32)
    qsin = np.ascontiguousarray(sin[N_META:].T).astype(np.float32)

    def whole(a):
        return pl.BlockSpec(a.shape, lambda b, i: (0,) * a.ndim)

    def fetched_once(a, squeeze_layer):
        shape = ((None,) + a.shape[1:]) if squeeze_layer else a.shape
        return pl.BlockSpec(shape, lambda b, i: (0,) * a.ndim, pipeline_mode=pl.Buffered(1))

    row_spec = pl.BlockSpec((None, tm, D_MODEL), lambda b, i: (b, i, 0))
    ktable_spec = pl.BlockSpec((tm, LANES), lambda b, i: (i, 0))
    qtable_spec = pl.BlockSpec((QK_ROPE // 2, tm), lambda b, i: (0, i))

    w_in_t = jnp.transpose(w_in[0])

    operands = [
        (x, row_spec),
        (meta_tokens, None),
        (kcos[N_META:], ktable_spec),
        (ksin[N_META:], ktable_spec),
        (kcos[:N_META], None),
        (ksin[:N_META], None),
        (qcos, qtable_spec),
        (qsin, qtable_spec),
        (norm_g, None),
        (w_in_t, fetched_once(w_in_t, False)),
        (q_norm_g, None), (w_q_b, fetched_once(w_q_b, True)),
        (kv_norm_g, None), (w_kv_b, fetched_once(w_kv_b, True)),
        (pool_w, fetched_once(pool_w, True)),
        (pool_scale, None),
        (w_out, fetched_once(w_out, True)),
        (final_norm_g.reshape(1, D_MODEL), None),
    ]
    args = [a for a, _ in operands]
    in_specs = [whole(a) if s is None else s for a, s in operands]

    return pl.pallas_call(
        functools.partial(_fused_kernel, seq // tm),
        out_shape=jax.ShapeDtypeStruct(x.shape, x.dtype),
        grid=(batch, seq // tm),
        in_specs=in_specs,
        out_specs=row_spec,
        scratch_shapes=[
            pltpu.VMEM((D_MODEL, D_IN_PADDED), BF16),
            pltpu.VMEM((N_HEADS * (QK_NOPE + QK_ROPE), Q_LORA), BF16),
            pltpu.VMEM((KV_LORA, N_HEADS * QK_NOPE), BF16),
            pltpu.VMEM((N_HEADS * V_HEAD, KV_LORA), BF16),
            pltpu.VMEM((len(POOL_WINDOWS), POOL_GROUP, POOL_GROUP), BF16),
            pltpu.VMEM((D_MODEL, D_MODEL), BF16),
            pltpu.VMEM((N_HEADS, seq, QK_PAD), BF16),
            pltpu.VMEM((N_HEADS, V_AUG, seq), BF16),
            pltpu.VMEM((N_HEADS, N_META, QK_PAD), BF16),
            pltpu.VMEM((N_HEADS, V_AUG, N_META), BF16),
            pltpu.VMEM((HALO + tm, D_POOL), F32),
            pltpu.VMEM((HALO, D_POOL), F32),
            pltpu.VMEM((N_HEADS, QK_PAD, tm), BF16),
            pltpu.VMEM((tm, tm), F32),
            pltpu.VMEM((N_HEADS, 1, tm), F32),
            pltpu.VMEM((N_HEADS, V_AUG, tm), F32),
            pltpu.VMEM((1, tm), F32),
            pltpu.SMEM((1,), F32),
        ],
        compiler_params=pltpu.CompilerParams(
            dimension_semantics=("arbitrary", "arbitrary"),
            vmem_limit_bytes=VMEM_LIMIT_BYTES,
        ),
        name="hybrid_pool_mla_block",
    )(*args)
```

```python
import functools
import math

import numpy as np
import jax
import jax.numpy as jnp
from jax import lax
from jax.experimental import pallas as pl
from jax.experimental.pallas import tpu as pltpu

D_MODEL = 1024
N_META = 16
CHUNK = 64
D_POOL = 512
POOL_WINDOWS = (2, 4, 8, 16)
POOL_GROUP = 128
N_HEADS = 4
QK_NOPE = 128
QK_ROPE = 64
V_HEAD = 128
D_ATTN = N_HEADS * V_HEAD
Q_LORA = 256
KV_LORA = 128
ROPE_THETA = 10000.0
EPS = 1e-6
Q_SCALE = (QK_NOPE + QK_ROPE) ** -0.5 * math.log2(math.e)

O_CQ = 2 * D_POOL
O_CKV = O_CQ + Q_LORA
O_KR = O_CKV + KV_LORA
O_AG = O_KR + QK_ROPE
D_IN = O_AG + D_ATTN
S_AG = O_AG + QK_ROPE
D_IN_PADDED = S_AG + D_ATTN

LANES = 128
BF16_ROWS = 16
QK_PAD = 2 * LANES
V_AUG = V_HEAD + BF16_ROWS
HALO = 16
ROW_TILE = 512
EXP_HEADROOM = 64.0
MASK_VALUE = -1e30
VMEM_LIMIT_BYTES = 56 * 1024 * 1024

F32 = jnp.float32
BF16 = jnp.bfloat16


def _dot(a, b):
    return jnp.dot(a, b, preferred_element_type=F32)


def _dot_nt(a, b):
    return lax.dot_general(a, b, (((1,), (1,)), ((), ())), preferred_element_type=F32)


def _rms(x, g):
    return x * lax.rsqrt(jnp.mean(x * x, axis=-1, keepdims=True) + EPS) * g


def _silu(x):
    h = 0.5 * x
    return h + h * jnp.tanh(h)


def _rope_lanes(x, cos_t, sin_t):
    lane = lax.broadcasted_iota(jnp.int32, x.shape, 1)
    first_half = (lane % QK_ROPE) < (QK_ROPE // 2)
    partner = jnp.where(first_half,
                        pltpu.roll(x, LANES - QK_ROPE // 2, axis=1),
                        pltpu.roll(x, QK_ROPE // 2, axis=1))
    return x * cos_t + partner * sin_t


def _latent_kv(ckvr, kv_g_ref, w_kb_ref, w_vbt_ref, cos_t, sin_t):
    ckv_n = _rms(ckvr[:, :KV_LORA], kv_g_ref[...]).astype(BF16)
    k_nope = _dot(ckv_n, w_kb_ref[...])
    v_t = _dot_nt(w_vbt_ref[...], ckv_n)
    k_rope = _rope_lanes(ckvr[:, KV_LORA:], cos_t, sin_t)
    return k_nope, v_t, k_rope


def _window_sum(xg, w):
    y = xg
    span = 1
    while span < w:
        y = y + pltpu.roll(y, span, axis=0)
        span *= 2
    return y


def _denominator_rows(n):
    return (lax.broadcasted_iota(jnp.int32, (BF16_ROWS, n), 0) == 0).astype(BF16)


def _fused_kernel(tiles_per_seq, x_ref, meta_ref, kcos_ref, ksin_ref, mcos_ref, msin_ref, qcos_ref, qsin_ref,
                  norm_g_ref, w_in_ref, q_g_ref, w_qb_ref, kv_g_ref, w_kvb_ref,
                  pool_w_f32_ref, pool_scale_ref, w_out_f32_ref, final_g_ref,
                  out_ref,
                  w_in_s, w_qbt_ref, w_kb_ref, w_vbt_ref, pool_w_ref, w_out_ref,
                  k_scr, vt_scr, kmeta_scr, vtmeta_scr, pool_scr, halo_scr,
                  qt_scr, s_scr, m_scr, acc_scr, rise_scr, worst_rise_ref):
    b = pl.program_id(0)
    i = pl.program_id(1)
    tm = ROW_TILE
    first_step = (b == 0) & (i == 0)

    w_pi_ref = w_in_s.at[:, 0:D_POOL]
    w_pg_ref = w_in_s.at[:, D_POOL:O_CQ]
    w_cq_ref = w_in_s.at[:, O_CQ:O_CKV]
    w_ckv_ref = w_in_s.at[:, O_CKV:S_AG]
    w_ag_ref = w_in_s.at[:, S_AG:S_AG + D_ATTN]

    @pl.when(first_step)
    def _prepare_weights():
        for c in range(O_KR // LANES):
            w_in_s[:, c * LANES:(c + 1) * LANES] = w_in_ref[c * LANES:(c + 1) * LANES, :].T.astype(BF16)
        k_rope_rows = w_in_ref[O_KR:O_AG, :]
        w_in_s[:, O_KR:S_AG] = jnp.concatenate([k_rope_rows, k_rope_rows], axis=0).T.astype(BF16)
        for c in range(D_ATTN // LANES):
            w_in_s[:, S_AG + c * LANES:S_AG + (c + 1) * LANES] = (
                w_in_ref[O_AG + c * LANES:O_AG + (c + 1) * LANES, :].T.astype(BF16))
        quarter = D_MODEL // 4
        for r in range(4):
            rs = slice(r * quarter, (r + 1) * quarter)
            w_out_ref[rs, :] = w_out_f32_ref[rs, :].astype(BF16)
        w_qbt_ref[...] = w_qb_ref[...].T.astype(BF16)
        for h in range(N_HEADS):
            base = h * (QK_NOPE + V_HEAD)
            w_kb_ref[:, h * QK_NOPE:(h + 1) * QK_NOPE] = w_kvb_ref[:, base:base + QK_NOPE].astype(BF16)
            w_vbt_ref[h * V_HEAD:(h + 1) * V_HEAD, :] = w_kvb_ref[:, base + QK_NOPE:base + QK_NOPE + V_HEAD].T.astype(BF16)
        pool_w_ref[...] = pool_w_f32_ref[...].astype(BF16)

    @pl.when(first_step)
    def _meta_prologue():
        xn = _rms(meta_ref[...], norm_g_ref[...]).astype(BF16)
        halo_scr[...] = _dot(xn, w_pi_ref[...])
        k_nope, v_t, k_rope = _latent_kv(_dot(xn, w_ckv_ref[...]), kv_g_ref, w_kb_ref, w_vbt_ref,
                                         mcos_ref[...], msin_ref[...])
        for h in range(N_HEADS):
            kmeta_scr[h, :, 0:LANES] = k_nope[:, h * QK_NOPE:(h + 1) * QK_NOPE].astype(BF16)
            kmeta_scr[h, :, LANES:QK_PAD] = k_rope.astype(BF16)
            vtmeta_scr[h, 0:V_HEAD, :] = v_t[h * V_HEAD:(h + 1) * V_HEAD, :].astype(BF16)
            vtmeta_scr[h, V_HEAD:V_AUG, :] = _denominator_rows(N_META)

    @pl.when(i == 0)
    def _seed_halo():
        pool_scr[0:HALO, :] = halo_scr[...]

    def _process_tile(n_earlier):
        hm = tm // 2
        halves = (slice(0, hm), slice(hm, tm))
        xn_halves = [_rms(x_ref[r, :], norm_g_ref[...]).astype(BF16) for r in halves]

        for r, xn_half in zip(halves, xn_halves):
            pool_scr[HALO + r.start:HALO + r.stop, :] = _dot(xn_half, w_pi_ref[...])
        xn = jnp.concatenate(xn_halves, axis=0)
        cq = _dot(xn, w_cq_ref[...])
        ckvr = _dot(xn, w_ckv_ref[...])

        pooled = []
        for g, w in enumerate(POOL_WINDOWS):
            xg = pool_scr[:, g * POOL_GROUP:(g + 1) * POOL_GROUP]
            win = _window_sum(xg, w)
            pooled.append((win[HALO:, :] * (1.0 / w) - xg[HALO:, :]).astype(BF16))
        mixed = [_dot(pooled[g], pool_w_ref[g]) for g in range(len(POOL_WINDOWS))]
        pool_gate = _dot(xn, w_pg_ref[...])

        cq_n = _rms(cq, q_g_ref[...]).astype(BF16)
        q_t = _dot_nt(w_qbt_ref[...], cq_n) * Q_SCALE
        row0 = pl.multiple_of(i * tm, tm)
        tile_rows = pl.ds(row0, tm)
        table_rows = tile_rows if n_earlier is None else pl.ds(n_earlier * tm, tm)
        k_nope, v_t, k_rope = _latent_kv(ckvr, kv_g_ref, w_kb_ref, w_vbt_ref,
                                         kcos_ref[table_rows, :], ksin_ref[table_rows, :])

        k_rope_bf = k_rope.astype(BF16)
        denominator_rows = _denominator_rows(tm)
        for h in range(N_HEADS):
            k_scr[h, tile_rows, 0:LANES] = k_nope[:, h * QK_NOPE:(h + 1) * QK_NOPE].astype(BF16)
            k_scr[h, tile_rows, LANES:QK_PAD] = k_rope_bf
            vt_scr[h, 0:V_HEAD, tile_rows] = v_t[h * V_HEAD:(h + 1) * V_HEAD, :].astype(BF16)
            vt_scr[h, V_HEAD:V_AUG, tile_rows] = denominator_rows

        cos_q = qcos_ref[:, table_rows]
        sin_q = qsin_ref[:, table_rows]
        half = QK_ROPE // 2
        for h in range(N_HEADS):
            base = h * (QK_NOPE + QK_ROPE)
            x1 = q_t[base + QK_NOPE:base + QK_NOPE + half, :]
            x2 = q_t[base + QK_NOPE + half:base + QK_NOPE + QK_ROPE, :]
            qt_scr[h, 0:QK_NOPE, :] = q_t[base:base + QK_NOPE, :].astype(BF16)
            qt_scr[h, QK_NOPE:QK_NOPE + half, :] = (x1 * cos_q - x2 * sin_q).astype(BF16)
            qt_scr[h, QK_NOPE + half:QK_NOPE + QK_ROPE, :] = (x1 * sin_q + x2 * cos_q).astype(BF16)
            qt_scr[h, QK_NOPE + QK_ROPE:QK_PAD, :] = jnp.zeros((QK_PAD - QK_NOPE - QK_ROPE, tm), BF16)

        pool_out = (_silu(pool_gate) * (jnp.concatenate(mixed, axis=1) * pool_scale_ref[...])).astype(BF16)

        def _scores(h, rows):
            return _dot(k_scr[h, rows, :], qt_scr[h])

        def _exact_update(h, blocks):
            m_prev = m_scr[h]
            m_new = m_prev
            for s, _ in blocks:
                m_new = jnp.maximum(m_new, jnp.max(s, axis=0, keepdims=True))
            acc = jnp.exp2(m_prev - m_new) * acc_scr[h]
            for s, vt_blk in blocks:
                acc = acc + _dot(vt_blk, jnp.exp2(s - m_new).astype(BF16))
            acc_scr[h] = acc
            m_scr[h] = m_new

        def _stale_update(h, blocks, cols=slice(0, tm)):
            m_ref = m_scr[h, :, cols]
            acc = acc_scr[h, :, cols]
            rise = rise_scr[:, cols]
            lift = jnp.zeros_like(m_ref)
            for s, vt_blk in blocks:
                t = s - m_ref
                top = jnp.max(t, axis=0, keepdims=True)
                acc = acc + _dot(vt_blk, jnp.exp2(t).astype(BF16))
                rise = jnp.maximum(rise, top)
                lift = jnp.maximum(lift, top)
            acc_scr[h, :, cols] = acc * jnp.exp2(-lift)
            m_scr[h, :, cols] = m_ref + lift
            rise_scr[:, cols] = rise

        key_chunk = lax.broadcasted_iota(jnp.int32, (tm, tm), 0) // CHUNK
        query_chunk = lax.broadcasted_iota(jnp.int32, (tm, tm), 1) // CHUNK
        diag_valid = key_chunk <= query_chunk

        def _start_from_meta_keys():
            s_meta = [_dot(kmeta_scr[h], qt_scr[h]) for h in range(N_HEADS)]
            for h in range(N_HEADS):
                m_scr[h] = jnp.max(s_meta[h], axis=0, keepdims=True)
                acc_scr[h] = jnp.zeros((V_AUG, tm), F32)
            return s_meta

        def _attend_unrolled(update):
            s_meta = _start_from_meta_keys()
            hq = tm // 2
            near_rows = pl.ds(n_earlier * tm, hq)
            far_rows = pl.ds(n_earlier * tm + hq, hq)
            late_queries = slice(hq, tm)
            near_valid = diag_valid[0:hq, :]
            far_valid = diag_valid[hq:, hq:]
            units = []
            for pair in range(0, N_HEADS, 2):
                units += [("near", h) for h in (pair, pair + 1)] + [("far", h) for h in (pair, pair + 1)]
            units += [(j, h) for j in range(n_earlier) for h in range(N_HEADS)]

            def unit_scores(unit):
                j, h = unit
                if j == "near":
                    return jnp.where(near_valid, _scores(h, near_rows), MASK_VALUE)
                if j == "far":
                    return jnp.where(far_valid, _dot(k_scr[h, far_rows, :], qt_scr[h, :, late_queries]),
                                     MASK_VALUE)
                return _scores(h, pl.ds(j * tm, tm))

            s_next = unit_scores(units[0])
            for n, (j, h) in enumerate(units):
                s = s_next
                if n + 1 < len(units):
                    s_next = unit_scores(units[n + 1])
                if n == 0:
                    attn_gate = _dot(xn, w_ag_ref[...])
                if j == "near":
                    update(h, [(s_meta[h], vtmeta_scr[h]), (s, vt_scr[h, :, near_rows])])
                elif j == "far":
                    update(h, [(s, vt_scr[h, :, far_rows])], late_queries)
                else:
                    update(h, [(s, vt_scr[h, :, pl.ds(j * tm, tm)])])
            return attn_gate

        def _attend_looped(update):
            s_meta = _start_from_meta_keys()
            s_next = jnp.where(diag_valid, _scores(0, tile_rows), MASK_VALUE)
            for h in range(N_HEADS):
                s = s_next
                if h + 1 < N_HEADS:
                    s_next = jnp.where(diag_valid, _scores(h + 1, tile_rows), MASK_VALUE)
                else:
                    s_scr[...] = _scores(0, pl.ds(0, tm))
                update(h, [(s_meta[h], vtmeta_scr[h]), (s, vt_scr[h, :, tile_rows])])

            def _earlier_tile(j, carry):
                rows = pl.ds(pl.multiple_of(j * tm, tm), tm)
                next_rows = pl.ds(pl.multiple_of((j + 1) * tm, tm), tm)
                s_next = s_scr[...]
                for h in range(N_HEADS):
                    s = s_next
                    if h + 1 < N_HEADS:
                        s_next = _scores(h + 1, rows)
                    else:
                        s_scr[...] = _scores(0, next_rows)
                    update(h, [(s, vt_scr[h, :, rows])])
                return carry

            lax.fori_loop(0, i, _earlier_tile, 0)
            return _dot(xn, w_ag_ref[...])

        attn_gate = _attend_looped(_exact_update) if n_earlier is None else _attend_unrolled(_stale_update)
        if n_earlier is not None:
            worst_rise_ref[0] = jnp.max(rise_scr[...])

        mix_pool = _dot(pool_out, w_out_ref[0:D_POOL, :])
        attn = []
        for h in range(N_HEADS):
            acc = acc_scr[h]
            attn.append((acc[0:V_HEAD, :] * (1.0 / acc[V_HEAD:V_HEAD + 1, :])).T)
        attn_out = (_silu(attn_gate) * jnp.concatenate(attn, axis=1)).astype(BF16)
        for r in halves:
            mix = mix_pool[r, :] + _dot(attn_out[r, :], w_out_ref[D_POOL:, :])
            out_ref[r, :] = _rms(x_ref[r, :] + mix, final_g_ref[...])

    rise_scr[...] = jnp.full((1, tm), MASK_VALUE, F32)
    for n_earlier in range(tiles_per_seq):
        @pl.when(i == n_earlier)
        def _tile_at(n_earlier=n_earlier):
            _process_tile(n_earlier)

    @pl.when(worst_rise_ref[0] >= EXP_HEADROOM)
    def _redo_exactly():
        _process_tile(None)

    pool_scr[0:HALO, :] = pool_scr[tm:tm + HALO, :]


def _rope_tables(length):
    half = QK_ROPE // 2
    inv_freq = 1.0 / (ROPE_THETA ** (np.arange(half, dtype=np.float64) / half))
    ang = np.arange(length, dtype=np.float64)[:, None] * inv_freq[None, :]
    return np.cos(ang), np.sin(ang)


def kernel(x, meta_tokens, norm_g, w_in, q_norm_g, w_q_b, kv_norm_g, w_kv_b, pool_w, pool_scale, w_out, final_norm_g):
    batch, seq, d_model = x.shape
    assert d_model == D_MODEL and seq % ROW_TILE == 0 and meta_tokens.shape == (N_META, D_MODEL)
    assert norm_g.shape[0] == 1 and w_in.shape == (1, D_MODEL, D_IN), "single-layer block"
    tm = ROW_TILE

    cos, sin = _rope_tables(N_META + seq)
    kcos = np.tile(cos, (1, LANES // (QK_ROPE // 2))).astype(np.float32)
    ksin = np.tile(np.concatenate([-sin, sin], axis=1), (1, LANES // QK_ROPE)).astype(np.float32)
    qcos = np.ascontiguousarray(cos[N_META:].T).astype(np.float32)
    qsin = np.ascontiguousarray(sin[N_META:].T).astype(np.float32)

    def whole(a):
        return pl.BlockSpec(a.shape, lambda b, i: (0,) * a.ndim)

    def fetched_once(a, squeeze_layer):
        shape = ((None,) + a.shape[1:]) if squeeze_layer else a.shape
        return pl.BlockSpec(shape, lambda b, i: (0,) * a.ndim, pipeline_mode=pl.Buffered(1))

    row_spec = pl.BlockSpec((None, tm, D_MODEL), lambda b, i: (b, i, 0))
    def table_spec(a):
        return pl.BlockSpec(a.shape, lambda b, i: (0, 0), pipeline_mode=pl.Buffered(1))

    w_in_t = jnp.transpose(w_in[0])

    operands = [
        (x, row_spec),
        (meta_tokens, None),
        (kcos[N_META:], table_spec(kcos[N_META:])),
        (ksin[N_META:], table_spec(ksin[N_META:])),
        (kcos[:N_META], None),
        (ksin[:N_META], None),
        (qcos, table_spec(qcos)),
        (qsin, table_spec(qsin)),
        (norm_g, None),
        (w_in_t, fetched_once(w_in_t, False)),
        (q_norm_g, None), (w_q_b, fetched_once(w_q_b, True)),
        (kv_norm_g, None), (w_kv_b, fetched_once(w_kv_b, True)),
        (pool_w, fetched_once(pool_w, True)),
        (pool_scale, None),
        (w_out, fetched_once(w_out, True)),
        (final_norm_g.reshape(1, D_MODEL), None),
    ]
    args = [a for a, _ in operands]
    in_specs = [whole(a) if s is None else s for a, s in operands]

    return pl.pallas_call(
        functools.partial(_fused_kernel, seq // tm),
        out_shape=jax.ShapeDtypeStruct(x.shape, x.dtype),
        grid=(batch, seq // tm),
        in_specs=in_specs,
        out_specs=row_spec,
        scratch_shapes=[
            pltpu.VMEM((D_MODEL, D_IN_PADDED), BF16),
            pltpu.VMEM((N_HEADS * (QK_NOPE + QK_ROPE), Q_LORA), BF16),
            pltpu.VMEM((KV_LORA, N_HEADS * QK_NOPE), BF16),
            pltpu.VMEM((N_HEADS * V_HEAD, KV_LORA), BF16),
            pltpu.VMEM((len(POOL_WINDOWS), POOL_GROUP, POOL_GROUP), BF16),
            pltpu.VMEM((D_MODEL, D_MODEL), BF16),
            pltpu.VMEM((N_HEADS, seq, QK_PAD), BF16),
            pltpu.VMEM((N_HEADS, V_AUG, seq), BF16),
            pltpu.VMEM((N_HEADS, N_META, QK_PAD), BF16),
            pltpu.VMEM((N_HEADS, V_AUG, N_META), BF16),
            pltpu.VMEM((HALO + tm, D_POOL), F32),
            pltpu.VMEM((HALO, D_POOL), F32),
            pltpu.VMEM((N_HEADS, QK_PAD, tm), BF16),
            pltpu.VMEM((tm, tm), F32),
            pltpu.VMEM((N_HEADS, 1, tm), F32),
            pltpu.VMEM((N_HEADS, V_AUG, tm), F32),
            pltpu.VMEM((1, tm), F32),
            pltpu.SMEM((1,), F32),
        ],
        compiler_params=pltpu.CompilerParams(
            dimension_semantics=("arbitrary", "arbitrary"),
            vmem_limit_bytes=VMEM_LIMIT_BYTES,
        ),
        name="hybrid_pool_mla_block",
    )(*args)
```

```python
import functools
import math

import numpy as np
import jax
import jax.numpy as jnp
from jax import lax
from jax.experimental import pallas as pl
from jax.experimental.pallas import tpu as pltpu

D_MODEL = 1024
N_META = 16
CHUNK = 64
D_POOL = 512
POOL_WINDOWS = (2, 4, 8, 16)
POOL_GROUP = 128
N_HEADS = 4
QK_NOPE = 128
QK_ROPE = 64
V_HEAD = 128
D_ATTN = N_HEADS * V_HEAD
Q_LORA = 256
KV_LORA = 128
ROPE_THETA = 10000.0
EPS = 1e-6
Q_SCALE = (QK_NOPE + QK_ROPE) ** -0.5 * math.log2(math.e)

O_CQ = 2 * D_POOL
O_CKV = O_CQ + Q_LORA
O_KR = O_CKV + KV_LORA
O_AG = O_KR + QK_ROPE
D_IN = O_AG + D_ATTN
S_AG = O_AG + QK_ROPE
D_IN_PADDED = S_AG + D_ATTN
W_IN_SPLIT = O_CQ

LANES = 128
BF16_ROWS = 16
QK_PAD = 2 * LANES
V_AUG = V_HEAD + BF16_ROWS
HALO = 16
ROW_TILE = 512
EXP_HEADROOM = 64.0
MASK_VALUE = -1e30
VMEM_LIMIT_BYTES = 56 * 1024 * 1024

F32 = jnp.float32
BF16 = jnp.bfloat16


def _dot(a, b):
    return jnp.dot(a, b, preferred_element_type=F32)


def _dot_nt(a, b):
    return lax.dot_general(a, b, (((1,), (1,)), ((), ())), preferred_element_type=F32)


def _rms(x, g):
    return x * lax.rsqrt(jnp.mean(x * x, axis=-1, keepdims=True) + EPS) * g


def _silu(x):
    h = 0.5 * x
    return h + h * jnp.tanh(h)


def _rope_lanes(x, cos_t, sin_t):
    lane = lax.broadcasted_iota(jnp.int32, x.shape, 1)
    first_half = (lane % QK_ROPE) < (QK_ROPE // 2)
    partner = jnp.where(first_half,
                        pltpu.roll(x, LANES - QK_ROPE // 2, axis=1),
                        pltpu.roll(x, QK_ROPE // 2, axis=1))
    return x * cos_t + partner * sin_t


def _latent_kv(ckvr, kv_g_ref, w_kb_ref, w_vbt_ref, cos_t, sin_t):
    ckv_n = _rms(ckvr[:, :KV_LORA], kv_g_ref[...]).astype(BF16)
    k_nope = _dot(ckv_n, w_kb_ref[...])
    v_t = _dot_nt(w_vbt_ref[...], ckv_n)
    k_rope = _rope_lanes(ckvr[:, KV_LORA:], cos_t, sin_t)
    return k_nope, v_t, k_rope


def _window_sum(xg, w):
    y = xg
    span = 1
    while span < w:
        y = y + pltpu.roll(y, span, axis=0)
        span *= 2
    return y


def _denominator_rows(n):
    return (lax.broadcasted_iota(jnp.int32, (BF16_ROWS, n), 0) == 0).astype(BF16)


def _fused_kernel(tiles_per_seq, x_ref, meta_ref, kcos_ref, ksin_ref, mcos_ref, msin_ref, qcos_ref, qsin_ref,
                  norm_g_ref, w_in_hbm, q_g_ref, w_qb_hbm, kv_g_ref, w_kvb_hbm,
                  pool_w_hbm, pool_scale_ref, w_out_hbm, final_g_ref,
                  out_ref,
                  w_in_s, w_qbt_ref, w_kb_ref, w_vbt_ref, pool_w_ref, w_out_ref,
                  k_scr, vt_scr, kmeta_scr, vtmeta_scr, pool_scr, halo_scr,
                  qt_scr, s_scr, m_scr, acc_scr, rise_scr, worst_rise_ref,
                  w_in_ref, w_out_f32_ref, w_qb_ref, w_kvb_ref, pool_w_f32_ref, copy_sems):
    b = pl.program_id(0)
    i = pl.program_id(1)
    tm = ROW_TILE
    first_step = (b == 0) & (i == 0)

    w_pi_ref = w_in_s.at[:, 0:D_POOL]
    w_pg_ref = w_in_s.at[:, D_POOL:O_CQ]
    w_cq_ref = w_in_s.at[:, O_CQ:O_CKV]
    w_ckv_ref = w_in_s.at[:, O_CKV:S_AG]
    w_ag_ref = w_in_s.at[:, S_AG:S_AG + D_ATTN]

    @pl.when(first_step)
    def _prepare_weights():
        copies = [
            pltpu.make_async_copy(w_in_hbm.at[0:W_IN_SPLIT, :], w_in_ref.at[0:W_IN_SPLIT, :], copy_sems.at[0]),
            pltpu.make_async_copy(w_in_hbm.at[W_IN_SPLIT:D_IN, :], w_in_ref.at[W_IN_SPLIT:D_IN, :], copy_sems.at[1]),
            pltpu.make_async_copy(w_out_hbm.at[0], w_out_f32_ref, copy_sems.at[2]),
            pltpu.make_async_copy(w_qb_hbm.at[0], w_qb_ref, copy_sems.at[3]),
            pltpu.make_async_copy(w_kvb_hbm.at[0], w_kvb_ref, copy_sems.at[4]),
            pltpu.make_async_copy(pool_w_hbm.at[0], pool_w_f32_ref, copy_sems.at[5]),
        ]
        for copy in copies:
            copy.start()

        copies[0].wait()
        for c in range(O_KR // LANES):
            if c * LANES == W_IN_SPLIT:
                copies[1].wait()
            w_in_s[:, c * LANES:(c + 1) * LANES] = w_in_ref[c * LANES:(c + 1) * LANES, :].T.astype(BF16)
        k_rope_rows = w_in_ref[O_KR:O_AG, :]
        w_in_s[:, O_KR:S_AG] = jnp.concatenate([k_rope_rows, k_rope_rows], axis=0).T.astype(BF16)
        for c in range(D_ATTN // LANES):
            w_in_s[:, S_AG + c * LANES:S_AG + (c + 1) * LANES] = (
                w_in_ref[O_AG + c * LANES:O_AG + (c + 1) * LANES, :].T.astype(BF16))
        quarter = D_MODEL // 4
        copies[2].wait()
        for r in range(4):
            rs = slice(r * quarter, (r + 1) * quarter)
            w_out_ref[rs, :] = w_out_f32_ref[rs, :].astype(BF16)
        copies[3].wait()
        w_qbt_ref[...] = w_qb_ref[...].T.astype(BF16)
        copies[4].wait()
        for h in range(N_HEADS):
            base = h * (QK_NOPE + V_HEAD)
            w_kb_ref[:, h * QK_NOPE:(h + 1) * QK_NOPE] = w_kvb_ref[:, base:base + QK_NOPE].astype(BF16)
            w_vbt_ref[h * V_HEAD:(h + 1) * V_HEAD, :] = w_kvb_ref[:, base + QK_NOPE:base + QK_NOPE + V_HEAD].T.astype(BF16)
        copies[5].wait()
        pool_w_ref[...] = pool_w_f32_ref[...].astype(BF16)

    @pl.when(first_step)
    def _meta_prologue():
        xn = _rms(meta_ref[...], norm_g_ref[...]).astype(BF16)
        halo_scr[...] = _dot(xn, w_pi_ref[...])
        k_nope, v_t, k_rope = _latent_kv(_dot(xn, w_ckv_ref[...]), kv_g_ref, w_kb_ref, w_vbt_ref,
                                         mcos_ref[...], msin_ref[...])
        for h in range(N_HEADS):
            kmeta_scr[h, :, 0:LANES] = k_nope[:, h * QK_NOPE:(h + 1) * QK_NOPE].astype(BF16)
            kmeta_scr[h, :, LANES:QK_PAD] = k_rope.astype(BF16)
            vtmeta_scr[h, 0:V_HEAD, :] = v_t[h * V_HEAD:(h + 1) * V_HEAD, :].astype(BF16)
            vtmeta_scr[h, V_HEAD:V_AUG, :] = _denominator_rows(N_META)

    @pl.when(i == 0)
    def _seed_halo():
        pool_scr[0:HALO, :] = halo_scr[...]

    def _process_tile(n_earlier):
        hm = tm // 2
        halves = (slice(0, hm), slice(hm, tm))
        xn_halves = [_rms(x_ref[r, :], norm_g_ref[...]).astype(BF16) for r in halves]

        for r, xn_half in zip(halves, xn_halves):
            pool_scr[HALO + r.start:HALO + r.stop, :] = _dot(xn_half, w_pi_ref[...])
        xn = jnp.concatenate(xn_halves, axis=0)
        cq = _dot(xn, w_cq_ref[...])
        ckvr = _dot(xn, w_ckv_ref[...])

        pooled = []
        for g, w in enumerate(POOL_WINDOWS):
            xg = pool_scr[:, g * POOL_GROUP:(g + 1) * POOL_GROUP]
            win = _window_sum(xg, w)
            pooled.append((win[HALO:, :] * (1.0 / w) - xg[HALO:, :]).astype(BF16))
        mixed = [_dot(pooled[g], pool_w_ref[g]) for g in range(len(POOL_WINDOWS))]
        pool_gate = _dot(xn, w_pg_ref[...])

        cq_n = _rms(cq, q_g_ref[...]).astype(BF16)
        q_t = _dot_nt(w_qbt_ref[...], cq_n) * Q_SCALE
        row0 = pl.multiple_of(i * tm, tm)
        tile_rows = pl.ds(row0, tm)
        table_rows = tile_rows if n_earlier is None else pl.ds(n_earlier * tm, tm)
        k_nope, v_t, k_rope = _latent_kv(ckvr, kv_g_ref, w_kb_ref, w_vbt_ref,
                                         kcos_ref[table_rows, :], ksin_ref[table_rows, :])

        k_rope_bf = k_rope.astype(BF16)
        denominator_rows = _denominator_rows(tm)
        for h in range(N_HEADS):
            k_scr[h, tile_rows, 0:LANES] = k_nope[:, h * QK_NOPE:(h + 1) * QK_NOPE].astype(BF16)
            k_scr[h, tile_rows, LANES:QK_PAD] = k_rope_bf
            vt_scr[h, 0:V_HEAD, tile_rows] = v_t[h * V_HEAD:(h + 1) * V_HEAD, :].astype(BF16)
            vt_scr[h, V_HEAD:V_AUG, tile_rows] = denominator_rows

        cos_q = qcos_ref[:, table_rows]
        sin_q = qsin_ref[:, table_rows]
        half = QK_ROPE // 2
        for h in range(N_HEADS):
            base = h * (QK_NOPE + QK_ROPE)
            x1 = q_t[base + QK_NOPE:base + QK_NOPE + half, :]
            x2 = q_t[base + QK_NOPE + half:base + QK_NOPE + QK_ROPE, :]
            qt_scr[h, 0:QK_NOPE, :] = q_t[base:base + QK_NOPE, :].astype(BF16)
            qt_scr[h, QK_NOPE:QK_NOPE + half, :] = (x1 * cos_q - x2 * sin_q).astype(BF16)
            qt_scr[h, QK_NOPE + half:QK_NOPE + QK_ROPE, :] = (x1 * sin_q + x2 * cos_q).astype(BF16)
            qt_scr[h, QK_NOPE + QK_ROPE:QK_PAD, :] = jnp.zeros((QK_PAD - QK_NOPE - QK_ROPE, tm), BF16)

        pool_out = (_silu(pool_gate) * (jnp.concatenate(mixed, axis=1) * pool_scale_ref[...])).astype(BF16)

        def _scores(h, rows):
            return _dot(k_scr[h, rows, :], qt_scr[h])

        def _exact_update(h, blocks):
            m_prev = m_scr[h]
            m_new = m_prev
            for s, _ in blocks:
                m_new = jnp.maximum(m_new, jnp.max(s, axis=0, keepdims=True))
            acc = jnp.exp2(m_prev - m_new) * acc_scr[h]
            for s, vt_blk in blocks:
                acc = acc + _dot(vt_blk, jnp.exp2(s - m_new).astype(BF16))
            acc_scr[h] = acc
            m_scr[h] = m_new

        def _stale_update(h, blocks, cols=slice(0, tm)):
            m_ref = m_scr[h, :, cols]
            acc = acc_scr[h, :, cols]
            rise = rise_scr[:, cols]
            lift = jnp.zeros_like(m_ref)
            for s, vt_blk in blocks:
                t = s - m_ref
                top = jnp.max(t, axis=0, keepdims=True)
                acc = acc + _dot(vt_blk, jnp.exp2(t).astype(BF16))
                rise = jnp.maximum(rise, top)
                lift = jnp.maximum(lift, top)
            acc_scr[h, :, cols] = acc * jnp.exp2(-lift)
            m_scr[h, :, cols] = m_ref + lift
            rise_scr[:, cols] = rise

        key_chunk = lax.broadcasted_iota(jnp.int32, (tm, tm), 0) // CHUNK
        query_chunk = lax.broadcasted_iota(jnp.int32, (tm, tm), 1) // CHUNK
        diag_valid = key_chunk <= query_chunk

        def _start_from_meta_keys():
            s_meta = [_dot(kmeta_scr[h], qt_scr[h]) for h in range(N_HEADS)]
            for h in range(N_HEADS):
                m_scr[h] = jnp.max(s_meta[h], axis=0, keepdims=True)
                acc_scr[h] = jnp.zeros((V_AUG, tm), F32)
            return s_meta

        def _attend_unrolled(update):
            s_meta = _start_from_meta_keys()
            hq = tm // 2
            near_rows = pl.ds(n_earlier * tm, hq)
            far_rows = pl.ds(n_earlier * tm + hq, hq)
            late_queries = slice(hq, tm)
            near_valid = diag_valid[0:hq, :]
            far_valid = diag_valid[hq:, hq:]
            units = []
            for pair in range(0, N_HEADS, 2):
                units += [("near", h) for h in (pair, pair + 1)] + [("far", h) for h in (pair, pair + 1)]
            units += [(j, h) for j in range(n_earlier) for h in range(N_HEADS)]

            def unit_scores(unit):
                j, h = unit
                if j == "near":
                    return jnp.where(near_valid, _scores(h, near_rows), MASK_VALUE)
                if j == "far":
                    return jnp.where(far_valid, _dot(k_scr[h, far_rows, :], qt_scr[h, :, late_queries]),
                                     MASK_VALUE)
                return _scores(h, pl.ds(j * tm, tm))

            s_next = unit_scores(units[0])
            for n, (j, h) in enumerate(units):
                s = s_next
                if n + 1 < len(units):
                    s_next = unit_scores(units[n + 1])
                if n == 0:
                    attn_gate = _dot(xn, w_ag_ref[...])
                if j == "near":
                    update(h, [(s_meta[h], vtmeta_scr[h]), (s, vt_scr[h, :, near_rows])])
                elif j == "far":
                    update(h, [(s, vt_scr[h, :, far_rows])], late_queries)
                else:
                    update(h, [(s, vt_scr[h, :, pl.ds(j * tm, tm)])])
            return attn_gate

        def _attend_looped(update):
            s_meta = _start_from_meta_keys()
            s_next = jnp.where(diag_valid, _scores(0, tile_rows), MASK_VALUE)
            for h in range(N_HEADS):
                s = s_next
                if h + 1 < N_HEADS:
                    s_next = jnp.where(diag_valid, _scores(h + 1, tile_rows), MASK_VALUE)
                else:
                    s_scr[...] = _scores(0, pl.ds(0, tm))
                update(h, [(s_meta[h], vtmeta_scr[h]), (s, vt_scr[h, :, tile_rows])])

            def _earlier_tile(j, carry):
                rows = pl.ds(pl.multiple_of(j * tm, tm), tm)
                next_rows = pl.ds(pl.multiple_of((j + 1) * tm, tm), tm)
                s_next = s_scr[...]
                for h in range(N_HEADS):
                    s = s_next
                    if h + 1 < N_HEADS:
                        s_next = _scores(h + 1, rows)
                    else:
                        s_scr[...] = _scores(0, next_rows)
                    update(h, [(s, vt_scr[h, :, rows])])
                return carry

            lax.fori_loop(0, i, _earlier_tile, 0)
            return _dot(xn, w_ag_ref[...])

        attn_gate = _attend_looped(_exact_update) if n_earlier is None else _attend_unrolled(_stale_update)
        if n_earlier is not None:
            worst_rise_ref[0] = jnp.max(rise_scr[...])

        mix_pool = _dot(pool_out, w_out_ref[0:D_POOL, :])
        attn = []
        for h in range(N_HEADS):
            acc = acc_scr[h]
            attn.append((acc[0:V_HEAD, :] * (1.0 / acc[V_HEAD:V_HEAD + 1, :])).T)
        attn_out = (_silu(attn_gate) * jnp.concatenate(attn, axis=1)).astype(BF16)
        for r in halves:
            mix = mix_pool[r, :] + _dot(attn_out[r, :], w_out_ref[D_POOL:, :])
            out_ref[r, :] = _rms(x_ref[r, :] + mix, final_g_ref[...])

    rise_scr[...] = jnp.full((1, tm), MASK_VALUE, F32)
    for n_earlier in range(tiles_per_seq):
        @pl.when(i == n_earlier)
        def _tile_at(n_earlier=n_earlier):
            _process_tile(n_earlier)

    @pl.when(worst_rise_ref[0] >= EXP_HEADROOM)
    def _redo_exactly():
        _process_tile(None)

    pool_scr[0:HALO, :] = pool_scr[tm:tm + HALO, :]


def _rope_tables(length):
    half = QK_ROPE // 2
    inv_freq = 1.0 / (ROPE_THETA ** (np.arange(half, dtype=np.float64) / half))
    ang = np.arange(length, dtype=np.float64)[:, None] * inv_freq[None, :]
    return np.cos(ang), np.sin(ang)


def kernel(x, meta_tokens, norm_g, w_in, q_norm_g, w_q_b, kv_norm_g, w_kv_b, pool_w, pool_scale, w_out, final_norm_g):
    batch, seq, d_model = x.shape
    assert d_model == D_MODEL and seq % ROW_TILE == 0 and meta_tokens.shape == (N_META, D_MODEL)
    assert norm_g.shape[0] == 1 and w_in.shape == (1, D_MODEL, D_IN), "single-layer block"
    tm = ROW_TILE

    cos, sin = _rope_tables(N_META + seq)
    kcos = np.tile(cos, (1, LANES // (QK_ROPE // 2))).astype(np.float32)
    ksin = np.tile(np.concatenate([-sin, sin], axis=1), (1, LANES // QK_ROPE)).astype(np.float32)
    qcos = np.ascontiguousarray(cos[N_META:].T).astype(np.float32)
    qsin = np.ascontiguousarray(sin[N_META:].T).astype(np.float32)

    def whole(a):
        return pl.BlockSpec(a.shape, lambda b, i: (0,) * a.ndim)

    in_hbm = pl.BlockSpec(memory_space=pl.ANY)

    row_spec = pl.BlockSpec((None, tm, D_MODEL), lambda b, i: (b, i, 0))
    def table_spec(a):
        return pl.BlockSpec(a.shape, lambda b, i: (0, 0), pipeline_mode=pl.Buffered(1))

    w_in_t = jnp.transpose(w_in[0])

    operands = [
        (x, row_spec),
        (meta_tokens, None),
        (kcos[N_META:], table_spec(kcos[N_META:])),
        (ksin[N_META:], table_spec(ksin[N_META:])),
        (kcos[:N_META], None),
        (ksin[:N_META], None),
        (qcos, table_spec(qcos)),
        (qsin, table_spec(qsin)),
        (norm_g, None),
        (w_in_t, in_hbm),
        (q_norm_g, None), (w_q_b, in_hbm),
        (kv_norm_g, None), (w_kv_b, in_hbm),
        (pool_w, in_hbm),
        (pool_scale, None),
        (w_out, in_hbm),
        (final_norm_g.reshape(1, D_MODEL), None),
    ]
    args = [a for a, _ in operands]
    in_specs = [whole(a) if s is None else s for a, s in operands]

    return pl.pallas_call(
        functools.partial(_fused_kernel, seq // tm),
        out_shape=jax.ShapeDtypeStruct(x.shape, x.dtype),
        grid=(batch, seq // tm),
        in_specs=in_specs,
        out_specs=row_spec,
        scratch_shapes=[
            pltpu.VMEM((D_MODEL, D_IN_PADDED), BF16),
            pltpu.VMEM((N_HEADS * (QK_NOPE + QK_ROPE), Q_LORA), BF16),
            pltpu.VMEM((KV_LORA, N_HEADS * QK_NOPE), BF16),
            pltpu.VMEM((N_HEADS * V_HEAD, KV_LORA), BF16),
            pltpu.VMEM((len(POOL_WINDOWS), POOL_GROUP, POOL_GROUP), BF16),
            pltpu.VMEM((D_MODEL, D_MODEL), BF16),
            pltpu.VMEM((N_HEADS, seq, QK_PAD), BF16),
            pltpu.VMEM((N_HEADS, V_AUG, seq), BF16),
            pltpu.VMEM((N_HEADS, N_META, QK_PAD), BF16),
            pltpu.VMEM((N_HEADS, V_AUG, N_META), BF16),
            pltpu.VMEM((HALO + tm, D_POOL), F32),
            pltpu.VMEM((HALO, D_POOL), F32),
            pltpu.VMEM((N_HEADS, QK_PAD, tm), BF16),
            pltpu.VMEM((tm, tm), F32),
            pltpu.VMEM((N_HEADS, 1, tm), F32),
            pltpu.VMEM((N_HEADS, V_AUG, tm), F32),
            pltpu.VMEM((1, tm), F32),
            pltpu.SMEM((1,), F32),
            pltpu.VMEM((D_IN, D_MODEL), F32),
            pltpu.VMEM((D_MODEL, D_MODEL), F32),
            pltpu.VMEM((Q_LORA, N_HEADS * (QK_NOPE + QK_ROPE)), F32),
            pltpu.VMEM((KV_LORA, N_HEADS * (QK_NOPE + V_HEAD)), F32),
            pltpu.VMEM((len(POOL_WINDOWS), POOL_GROUP, POOL_GROUP), F32),
            pltpu.SemaphoreType.DMA((6,)),
        ],
        compiler_params=pltpu.CompilerParams(
            dimension_semantics=("arbitrary", "arbitrary"),
            vmem_limit_bytes=VMEM_LIMIT_BYTES,
        ),
        name="hybrid_pool_mla_block",
    )(*args)
```

```python
import functools
import math

import numpy as np
import jax
import jax.numpy as jnp
from jax import lax
from jax.experimental import pallas as pl
from jax.experimental.pallas import tpu as pltpu

D_MODEL = 1024
N_META = 16
CHUNK = 64
D_POOL = 512
POOL_WINDOWS = (2, 4, 8, 16)
POOL_GROUP = 128
N_HEADS = 4
QK_NOPE = 128
QK_ROPE = 64
V_HEAD = 128
D_ATTN = N_HEADS * V_HEAD
Q_LORA = 256
KV_LORA = 128
ROPE_THETA = 10000.0
EPS = 1e-6
Q_SCALE = (QK_NOPE + QK_ROPE) ** -0.5 * math.log2(math.e)

O_CQ = 2 * D_POOL
O_CKV = O_CQ + Q_LORA
O_KR = O_CKV + KV_LORA
O_AG = O_KR + QK_ROPE
D_IN = O_AG + D_ATTN
S_AG = O_AG + QK_ROPE
D_IN_PADDED = S_AG + D_ATTN

LANES = 128
BF16_ROWS = 16
QK_PAD = 2 * LANES
V_AUG = V_HEAD + BF16_ROWS
HALO = 16
ROW_TILE = 512
EXP_HEADROOM = 32.0
MASK_VALUE = -1e30
VMEM_LIMIT_BYTES = 56 * 1024 * 1024

F32 = jnp.float32
BF16 = jnp.bfloat16


def _dot(a, b):
    return jnp.dot(a, b, preferred_element_type=F32)


def _dot_nt(a, b):
    return lax.dot_general(a, b, (((1,), (1,)), ((), ())), preferred_element_type=F32)


def _rms(x, g):
    return x * lax.rsqrt(jnp.mean(x * x, axis=-1, keepdims=True) + EPS) * g


def _silu(x):
    h = 0.5 * x
    return h + h * jnp.tanh(h)


def _rope_lanes(x, cos_t, sin_t):
    lane = lax.broadcasted_iota(jnp.int32, x.shape, 1)
    first_half = (lane % QK_ROPE) < (QK_ROPE // 2)
    partner = jnp.where(first_half,
                        pltpu.roll(x, LANES - QK_ROPE // 2, axis=1),
                        pltpu.roll(x, QK_ROPE // 2, axis=1))
    return x * cos_t + partner * sin_t


def _latent_kv(ckvr, kv_g_ref, w_kb_ref, w_vbt_ref, cos_t, sin_t):
    ckv_n = _rms(ckvr[:, :KV_LORA], kv_g_ref[...]).astype(BF16)
    k_nope = _dot(ckv_n, w_kb_ref[...])
    v_t = _dot_nt(w_vbt_ref[...], ckv_n)
    k_rope = _rope_lanes(ckvr[:, KV_LORA:], cos_t, sin_t)
    return k_nope, v_t, k_rope


def _window_sum(xg, w):
    y = xg
    span = 1
    while span < w:
        y = y + pltpu.roll(y, span, axis=0)
        span *= 2
    return y


def _denominator_rows(n):
    return (lax.broadcasted_iota(jnp.int32, (BF16_ROWS, n), 0) == 0).astype(BF16)


def _fused_kernel(tiles_per_seq, x_ref, meta_ref, kcos_ref, ksin_ref, mcos_ref, msin_ref, qcos_ref, qsin_ref,
                  norm_g_ref, w_in_ref, q_g_ref, w_qb_ref, kv_g_ref, w_kvb_ref,
                  pool_w_f32_ref, pool_scale_ref, w_out_f32_ref, final_g_ref,
                  out_ref,
                  w_in_s, w_qbt_ref, w_kb_ref, w_vbt_ref, pool_w_ref, w_out_ref,
                  k_scr, vt_scr, kmeta_scr, vtmeta_scr, pool_scr, halo_scr,
                  qt_scr, s_scr, m_scr, acc_scr, rise_scr, worst_rise_ref):
    b = pl.program_id(0)
    i = pl.program_id(1)
    tm = ROW_TILE
    first_step = (b == 0) & (i == 0)

    w_pi_ref = w_in_s.at[:, 0:D_POOL]
    w_pg_ref = w_in_s.at[:, D_POOL:O_CQ]
    w_cq_ref = w_in_s.at[:, O_CQ:O_CKV]
    w_ckv_ref = w_in_s.at[:, O_CKV:S_AG]
    w_ag_ref = w_in_s.at[:, S_AG:S_AG + D_ATTN]

    @pl.when(first_step)
    def _prepare_weights():
        for c in range(O_KR // LANES):
            w_in_s[:, c * LANES:(c + 1) * LANES] = w_in_ref[c * LANES:(c + 1) * LANES, :].T.astype(BF16)
        k_rope_rows = w_in_ref[O_KR:O_AG, :]
        w_in_s[:, O_KR:S_AG] = jnp.concatenate([k_rope_rows, k_rope_rows], axis=0).T.astype(BF16)
        for c in range(D_ATTN // LANES):
            w_in_s[:, S_AG + c * LANES:S_AG + (c + 1) * LANES] = (
                w_in_ref[O_AG + c * LANES:O_AG + (c + 1) * LANES, :].T.astype(BF16))
        quarter = D_MODEL // 4
        for r in range(4):
            rs = slice(r * quarter, (r + 1) * quarter)
            w_out_ref[rs, :] = w_out_f32_ref[rs, :].astype(BF16)
        w_qbt_ref[...] = w_qb_ref[...].T.astype(BF16)
        for h in range(N_HEADS):
            base = h * (QK_NOPE + V_HEAD)
            w_kb_ref[:, h * QK_NOPE:(h + 1) * QK_NOPE] = w_kvb_ref[:, base:base + QK_NOPE].astype(BF16)
            w_vbt_ref[h * V_HEAD:(h + 1) * V_HEAD, :] = w_kvb_ref[:, base + QK_NOPE:base + QK_NOPE + V_HEAD].T.astype(BF16)
        pool_w_ref[...] = pool_w_f32_ref[...].astype(BF16)

    @pl.when(first_step)
    def _meta_prologue():
        xn = _rms(meta_ref[...], norm_g_ref[...]).astype(BF16)
        halo_scr[...] = _dot(xn, w_pi_ref[...])
        k_nope, v_t, k_rope = _latent_kv(_dot(xn, w_ckv_ref[...]), kv_g_ref, w_kb_ref, w_vbt_ref,
                                         mcos_ref[...], msin_ref[...])
        for h in range(N_HEADS):
            kmeta_scr[h, :, 0:LANES] = k_nope[:, h * QK_NOPE:(h + 1) * QK_NOPE].astype(BF16)
            kmeta_scr[h, :, LANES:QK_PAD] = k_rope.astype(BF16)
            vtmeta_scr[h, 0:V_HEAD, :] = v_t[h * V_HEAD:(h + 1) * V_HEAD, :].astype(BF16)
            vtmeta_scr[h, V_HEAD:V_AUG, :] = _denominator_rows(N_META)

    @pl.when(i == 0)
    def _seed_halo():
        pool_scr[0:HALO, :] = halo_scr[...]

    def _process_tile(n_earlier):
        hm = tm // 2
        halves = (slice(0, hm), slice(hm, tm))
        xn_halves = [_rms(x_ref[r, :], norm_g_ref[...]).astype(BF16) for r in halves]

        for r, xn_half in zip(halves, xn_halves):
            pool_scr[HALO + r.start:HALO + r.stop, :] = _dot(xn_half, w_pi_ref[...])
        xn = jnp.concatenate(xn_halves, axis=0)
        cq = _dot(xn, w_cq_ref[...])
        ckvr = _dot(xn, w_ckv_ref[...])

        pooled = []
        for g, w in enumerate(POOL_WINDOWS):
            xg = pool_scr[:, g * POOL_GROUP:(g + 1) * POOL_GROUP]
            win = _window_sum(xg, w)
            pooled.append((win[HALO:, :] * (1.0 / w) - xg[HALO:, :]).astype(BF16))
        mixed = [_dot(pooled[g], pool_w_ref[g]) for g in range(len(POOL_WINDOWS))]
        pool_gate = _dot(xn, w_pg_ref[...])

        cq_n = _rms(cq, q_g_ref[...]).astype(BF16)
        q_t = _dot_nt(w_qbt_ref[...], cq_n) * Q_SCALE
        row0 = pl.multiple_of(i * tm, tm)
        tile_rows = pl.ds(row0, tm)
        table_rows = tile_rows if n_earlier is None else pl.ds(n_earlier * tm, tm)
        k_nope, v_t, k_rope = _latent_kv(ckvr, kv_g_ref, w_kb_ref, w_vbt_ref,
                                         kcos_ref[table_rows, :], ksin_ref[table_rows, :])

        k_rope_bf = k_rope.astype(BF16)
        denominator_rows = _denominator_rows(tm)
        for h in range(N_HEADS):
            k_scr[h, tile_rows, 0:LANES] = k_nope[:, h * QK_NOPE:(h + 1) * QK_NOPE].astype(BF16)
            k_scr[h, tile_rows, LANES:QK_PAD] = k_rope_bf
            vt_scr[h, 0:V_HEAD, tile_rows] = v_t[h * V_HEAD:(h + 1) * V_HEAD, :].astype(BF16)
            vt_scr[h, V_HEAD:V_AUG, tile_rows] = denominator_rows

        cos_q = qcos_ref[:, table_rows]
        sin_q = qsin_ref[:, table_rows]
        half = QK_ROPE // 2
        for h in range(N_HEADS):
            base = h * (QK_NOPE + QK_ROPE)
            x1 = q_t[base + QK_NOPE:base + QK_NOPE + half, :]
            x2 = q_t[base + QK_NOPE + half:base + QK_NOPE + QK_ROPE, :]
            qt_scr[h, 0:QK_NOPE, :] = q_t[base:base + QK_NOPE, :].astype(BF16)
            qt_scr[h, QK_NOPE:QK_NOPE + half, :] = (x1 * cos_q - x2 * sin_q).astype(BF16)
            qt_scr[h, QK_NOPE + half:QK_NOPE + QK_ROPE, :] = (x1 * sin_q + x2 * cos_q).astype(BF16)
            qt_scr[h, QK_NOPE + QK_ROPE:QK_PAD, :] = jnp.zeros((QK_PAD - QK_NOPE - QK_ROPE, tm), BF16)

        pool_out = (_silu(pool_gate) * (jnp.concatenate(mixed, axis=1) * pool_scale_ref[...])).astype(BF16)

        def _scores(h, rows):
            return _dot(k_scr[h, rows, :], qt_scr[h])

        def _exact_update(h, blocks):
            m_prev = m_scr[h]
            m_new = m_prev
            for s, _ in blocks:
                m_new = jnp.maximum(m_new, jnp.max(s, axis=0, keepdims=True))
            acc = jnp.exp2(m_prev - m_new) * acc_scr[h]
            for s, vt_blk in blocks:
                acc = acc + _dot(vt_blk, jnp.exp2(s - m_new).astype(BF16))
            acc_scr[h] = acc
            m_scr[h] = m_new

        def _stale_update(h, blocks, cols=slice(0, tm)):
            m_ref = m_scr[h, :, cols]
            acc = acc_scr[h, :, cols]
            rise = rise_scr[:, cols]
            lift = jnp.zeros_like(m_ref)
            for s, vt_blk in blocks:
                t = s - m_ref
                top = jnp.max(t, axis=0, keepdims=True)
                acc = acc + _dot(vt_blk, jnp.exp2(t).astype(BF16))
                rise = jnp.maximum(rise, top)
                lift = jnp.maximum(lift, top)
            acc_scr[h, :, cols] = acc * jnp.exp2(-lift)
            m_scr[h, :, cols] = m_ref + lift
            rise_scr[:, cols] = rise

        key_chunk = lax.broadcasted_iota(jnp.int32, (tm, tm), 0) // CHUNK
        query_chunk = lax.broadcasted_iota(jnp.int32, (tm, tm), 1) // CHUNK
        diag_valid = key_chunk <= query_chunk

        def _start_from_meta_keys():
            s_meta = [_dot(kmeta_scr[h], qt_scr[h]) for h in range(N_HEADS)]
            for h in range(N_HEADS):
                m_scr[h] = jnp.max(s_meta[h], axis=0, keepdims=True)
                acc_scr[h] = jnp.zeros((V_AUG, tm), F32)
            return s_meta

        def _attend_unrolled(update):
            hq = tm // 2
            near_rows = pl.ds(n_earlier * tm, hq)
            far_rows = pl.ds(n_earlier * tm + hq, hq)
            late_queries = slice(hq, tm)
            near_valid = diag_valid[0:hq, :]
            far_valid = diag_valid[hq:, hq:]
            units = []
            for pair in range(0, N_HEADS, 2):
                units += [("near", h) for h in (pair, pair + 1)] + [("far", h) for h in (pair, pair + 1)]
            units += [(j, h) for j in range(n_earlier) for h in range(N_HEADS)]

            def unit_scores(unit):
                j, h = unit
                if j == "near":
                    return _dot(jnp.concatenate([kmeta_scr[h], k_scr[h, near_rows, :]], axis=0), qt_scr[h])
                if j == "far":
                    return jnp.where(far_valid, _dot(k_scr[h, far_rows, :], qt_scr[h, :, late_queries]),
                                     MASK_VALUE)
                return _scores(h, pl.ds(j * tm, tm))

            s_next = unit_scores(units[0])
            for n, (j, h) in enumerate(units):
                s = s_next
                if n + 1 < len(units):
                    s_next = unit_scores(units[n + 1])
                if n == 0:
                    attn_gate = _dot(xn, w_ag_ref[...])
                if j == "near":
                    s_meta = s[0:N_META, :]
                    m_scr[h] = jnp.max(s_meta, axis=0, keepdims=True)
                    acc_scr[h] = jnp.zeros((V_AUG, tm), F32)
                    update(h, [(s_meta, vtmeta_scr[h]),
                               (jnp.where(near_valid, s[N_META:, :], MASK_VALUE), vt_scr[h, :, near_rows])])
                elif j == "far":
                    update(h, [(s, vt_scr[h, :, far_rows])], late_queries)
                else:
                    update(h, [(s, vt_scr[h, :, pl.ds(j * tm, tm)])])
            return attn_gate

        def _attend_looped(update):
            s_meta = _start_from_meta_keys()
            s_next = jnp.where(diag_valid, _scores(0, tile_rows), MASK_VALUE)
            for h in range(N_HEADS):
                s = s_next
                if h + 1 < N_HEADS:
                    s_next = jnp.where(diag_valid, _scores(h + 1, tile_rows), MASK_VALUE)
                else:
                    s_scr[...] = _scores(0, pl.ds(0, tm))
                update(h, [(s_meta[h], vtmeta_scr[h]), (s, vt_scr[h, :, tile_rows])])

            def _earlier_tile(j, carry):
                rows = pl.ds(pl.multiple_of(j * tm, tm), tm)
                next_rows = pl.ds(pl.multiple_of((j + 1) * tm, tm), tm)
                s_next = s_scr[...]
                for h in range(N_HEADS):
                    s = s_next
                    if h + 1 < N_HEADS:
                        s_next = _scores(h + 1, rows)
                    else:
                        s_scr[...] = _scores(0, next_rows)
                    update(h, [(s, vt_scr[h, :, rows])])
                return carry

            lax.fori_loop(0, i, _earlier_tile, 0)
            return _dot(xn, w_ag_ref[...])

        attn_gate = _attend_looped(_exact_update) if n_earlier is None else _attend_unrolled(_stale_update)
        if n_earlier is not None:
            worst_rise_ref[0] = jnp.max(rise_scr[...])

        mix_pool = _dot(pool_out, w_out_ref[0:D_POOL, :])
        attn = []
        for h in range(N_HEADS):
            acc = acc_scr[h]
            attn.append((acc[0:V_HEAD, :] * (1.0 / acc[V_HEAD:V_HEAD + 1, :])).T)
        attn_out = (_silu(attn_gate) * jnp.concatenate(attn, axis=1)).astype(BF16)
        for r in halves:
            mix = mix_pool[r, :] + _dot(attn_out[r, :], w_out_ref[D_POOL:, :])
            out_ref[r, :] = _rms(x_ref[r, :] + mix, final_g_ref[...])

    rise_scr[...] = jnp.full((1, tm), MASK_VALUE, F32)
    for n_earlier in range(tiles_per_seq):
        @pl.when(i == n_earlier)
        def _tile_at(n_earlier=n_earlier):
            _process_tile(n_earlier)

    @pl.when(worst_rise_ref[0] >= EXP_HEADROOM)
    def _redo_exactly():
        _process_tile(None)

    pool_scr[0:HALO, :] = pool_scr[tm:tm + HALO, :]


def _rope_tables(length):
    half = QK_ROPE // 2
    inv_freq = 1.0 / (ROPE_THETA ** (np.arange(half, dtype=np.float64) / half))
    ang = np.arange(length, dtype=np.float64)[:, None] * inv_freq[None, :]
    return np.cos(ang), np.sin(ang)


def kernel(x, meta_tokens, norm_g, w_in, q_norm_g, w_q_b, kv_norm_g, w_kv_b, pool_w, pool_scale, w_out, final_norm_g):
    batch, seq, d_model = x.shape
    assert d_model == D_MODEL and seq % ROW_TILE == 0 and meta_tokens.shape == (N_META, D_MODEL)
    assert norm_g.shape[0] == 1 and w_in.shape == (1, D_MODEL, D_IN), "single-layer block"
    tm = ROW_TILE

    cos, sin = _rope_tables(N_META + seq)
    kcos = np.tile(cos, (1, LANES // (QK_ROPE // 2))).astype(np.float32)
    ksin = np.tile(np.concatenate([-sin, sin], axis=1), (1, LANES // QK_ROPE)).astype(np.float32)
    qcos = np.ascontiguousarray(cos[N_META:].T).astype(np.float32)
    qsin = np.ascontiguousarray(sin[N_META:].T).astype(np.float32)

    def whole(a):
        return pl.BlockSpec(a.shape, lambda b, i: (0,) * a.ndim)

    def fetched_once(a, squeeze_layer):
        shape = ((None,) + a.shape[1:]) if squeeze_layer else a.shape
        return pl.BlockSpec(shape, lambda b, i: (0,) * a.ndim, pipeline_mode=pl.Buffered(1))

    row_spec = pl.BlockSpec((None, tm, D_MODEL), lambda b, i: (b, i, 0))
    def table_spec(a):
        return pl.BlockSpec(a.shape, lambda b, i: (0, 0), pipeline_mode=pl.Buffered(1))

    w_in_t = jnp.transpose(w_in[0])

    operands = [
        (x, row_spec),
        (meta_tokens, None),
        (kcos[N_META:], table_spec(kcos[N_META:])),
        (ksin[N_META:], table_spec(ksin[N_META:])),
        (kcos[:N_META], None),
        (ksin[:N_META], None),
        (qcos, table_spec(qcos)),
        (qsin, table_spec(qsin)),
        (norm_g, None),
        (w_in_t, fetched_once(w_in_t, False)),
        (q_norm_g, None), (w_q_b, fetched_once(w_q_b, True)),
        (kv_norm_g, None), (w_kv_b, fetched_once(w_kv_b, True)),
        (pool_w, fetched_once(pool_w, True)),
        (pool_scale, None),
        (w_out, fetched_once(w_out, True)),
        (final_norm_g.reshape(1, D_MODEL), None),
    ]
    args = [a for a, _ in operands]
    in_specs = [whole(a) if s is None else s for a, s in operands]

    return pl.pallas_call(
        functools.partial(_fused_kernel, seq // tm),
        out_shape=jax.ShapeDtypeStruct(x.shape, x.dtype),
        grid=(batch, seq // tm),
        in_specs=in_specs,
        out_specs=row_spec,
        scratch_shapes=[
            pltpu.VMEM((D_MODEL, D_IN_PADDED), BF16),
            pltpu.VMEM((N_HEADS * (QK_NOPE + QK_ROPE), Q_LORA), BF16),
            pltpu.VMEM((KV_LORA, N_HEADS * QK_NOPE), BF16),
            pltpu.VMEM((N_HEADS * V_HEAD, KV_LORA), BF16),
            pltpu.VMEM((len(POOL_WINDOWS), POOL_GROUP, POOL_GROUP), BF16),
            pltpu.VMEM((D_MODEL, D_MODEL), BF16),
            pltpu.VMEM((N_HEADS, seq, QK_PAD), BF16),
            pltpu.VMEM((N_HEADS, V_AUG, seq), BF16),
            pltpu.VMEM((N_HEADS, N_META, QK_PAD), BF16),
            pltpu.VMEM((N_HEADS, V_AUG, N_META), BF16),
            pltpu.VMEM((HALO + tm, D_POOL), F32),
            pltpu.VMEM((HALO, D_POOL), F32),
            pltpu.VMEM((N_HEADS, QK_PAD, tm), BF16),
            pltpu.VMEM((tm, tm), F32),
            pltpu.VMEM((N_HEADS, 1, tm), F32),
            pltpu.VMEM((N_HEADS, V_AUG, tm), F32),
            pltpu.VMEM((1, tm), F32),
            pltpu.SMEM((1,), F32),
        ],
        compiler_params=pltpu.CompilerParams(
            dimension_semantics=("arbitrary", "arbitrary"),
            vmem_limit_bytes=VMEM_LIMIT_BYTES,
        ),
        name="hybrid_pool_mla_block",
    )(*args)
```

```python
import functools
import math

import numpy as np
import jax
import jax.numpy as jnp
from jax import lax
from jax.experimental import pallas as pl
from jax.experimental.pallas import tpu as pltpu

D_MODEL = 1024
N_META = 16
CHUNK = 64
D_POOL = 512
POOL_WINDOWS = (2, 4, 8, 16)
POOL_GROUP = 128
N_HEADS = 4
QK_NOPE = 128
QK_ROPE = 64
V_HEAD = 128
D_ATTN = N_HEADS * V_HEAD
Q_LORA = 256
KV_LORA = 128
ROPE_THETA = 10000.0
EPS = 1e-6
Q_SCALE = (QK_NOPE + QK_ROPE) ** -0.5 * math.log2(math.e)

O_CQ = 2 * D_POOL
O_CKV = O_CQ + Q_LORA
O_KR = O_CKV + KV_LORA
O_AG = O_KR + QK_ROPE
D_IN = O_AG + D_ATTN
S_AG = O_AG + QK_ROPE
D_IN_PADDED = S_AG + D_ATTN

LANES = 128
BF16_ROWS = 16
QK_PAD = 2 * LANES
V_AUG = V_HEAD + BF16_ROWS
HALO = 16
ROW_TILE = 512
EXP_HEADROOM = 32.0
MASK_VALUE = -1e30
VMEM_LIMIT_BYTES = 56 * 1024 * 1024

F32 = jnp.float32
BF16 = jnp.bfloat16


def _dot(a, b):
    return jnp.dot(a, b, preferred_element_type=F32)


def _dot_nt(a, b):
    return lax.dot_general(a, b, (((1,), (1,)), ((), ())), preferred_element_type=F32)


def _rms(x, g):
    return x * lax.rsqrt(jnp.mean(x * x, axis=-1, keepdims=True) + EPS) * g


def _silu(x):
    h = 0.5 * x
    return h + h * jnp.tanh(h)


def _rope_lanes(x, cos_t, sin_t):
    lane = lax.broadcasted_iota(jnp.int32, x.shape, 1)
    first_half = (lane % QK_ROPE) < (QK_ROPE // 2)
    partner = jnp.where(first_half,
                        pltpu.roll(x, LANES - QK_ROPE // 2, axis=1),
                        pltpu.roll(x, QK_ROPE // 2, axis=1))
    return x * cos_t + partner * sin_t


def _latent_kv(ckvr, kv_g_ref, w_kb_ref, w_vbt_ref, cos_t, sin_t):
    ckv_n = _rms(ckvr[:, :KV_LORA], kv_g_ref[...]).astype(BF16)
    k_nope = _dot(ckv_n, w_kb_ref[...])
    v_t = _dot_nt(w_vbt_ref[...], ckv_n)
    k_rope = _rope_lanes(ckvr[:, KV_LORA:], cos_t, sin_t)
    return k_nope, v_t, k_rope


def _window_sum(xg, w):
    y = xg
    span = 1
    while span < w:
        y = y + pltpu.roll(y, span, axis=0)
        span *= 2
    return y


def _denominator_rows(n):
    return (lax.broadcasted_iota(jnp.int32, (BF16_ROWS, n), 0) == 0).astype(BF16)


def _fused_kernel(tiles_per_seq, x_ref, meta_ref, kcos_ref, ksin_ref, mcos_ref, msin_ref, qcos_ref, qsin_ref,
                  norm_g_ref, w_in_ref, q_g_ref, w_qb_ref, kv_g_ref, w_kvb_ref,
                  pool_w_f32_ref, pool_scale_ref, w_out_f32_ref, final_g_ref,
                  out_ref,
                  w_in_s, w_qbt_ref, w_kb_ref, w_vbt_ref, pool_w_ref, w_out_ref,
                  k_scr, vt_scr, kmeta_scr, vtmeta_scr, pool_scr, halo_scr,
                  qt_scr, s_scr, m_scr, acc_scr, rise_scr, worst_rise_ref):
    b = pl.program_id(0)
    i = pl.program_id(1)
    tm = ROW_TILE
    first_step = (b == 0) & (i == 0)

    w_pi_ref = w_in_s.at[:, 0:D_POOL]
    w_pg_ref = w_in_s.at[:, D_POOL:O_CQ]
    w_cq_ref = w_in_s.at[:, O_CQ:O_CKV]
    w_ckv_ref = w_in_s.at[:, O_CKV:S_AG]
    w_ag_ref = w_in_s.at[:, S_AG:S_AG + D_ATTN]

    @pl.when(first_step)
    def _prepare_weights():
        for c in range(O_KR // LANES):
            w_in_s[:, c * LANES:(c + 1) * LANES] = w_in_ref[c * LANES:(c + 1) * LANES, :].T.astype(BF16)
        k_rope_rows = w_in_ref[O_KR:O_AG, :]
        w_in_s[:, O_KR:S_AG] = jnp.concatenate([k_rope_rows, k_rope_rows], axis=0).T.astype(BF16)
        for c in range(D_ATTN // LANES):
            w_in_s[:, S_AG + c * LANES:S_AG + (c + 1) * LANES] = (
                w_in_ref[O_AG + c * LANES:O_AG + (c + 1) * LANES, :].T.astype(BF16))
        quarter = D_MODEL // 4
        for r in range(4):
            rs = slice(r * quarter, (r + 1) * quarter)
            w_out_ref[rs, :] = w_out_f32_ref[rs, :].astype(BF16)
        w_qbt_ref[...] = w_qb_ref[...].T.astype(BF16)
        for h in range(N_HEADS):
            base = h * (QK_NOPE + V_HEAD)
            w_kb_ref[:, h * QK_NOPE:(h + 1) * QK_NOPE] = w_kvb_ref[:, base:base + QK_NOPE].astype(BF16)
            w_vbt_ref[h * V_HEAD:(h + 1) * V_HEAD, :] = w_kvb_ref[:, base + QK_NOPE:base + QK_NOPE + V_HEAD].T.astype(BF16)
        pool_w_ref[...] = pool_w_f32_ref[...].astype(BF16)

    @pl.when(first_step)
    def _meta_prologue():
        xn = _rms(meta_ref[...], norm_g_ref[...]).astype(BF16)
        halo_scr[...] = _dot(xn, w_pi_ref[...])
        k_nope, v_t, k_rope = _latent_kv(_dot(xn, w_ckv_ref[...]), kv_g_ref, w_kb_ref, w_vbt_ref,
                                         mcos_ref[...], msin_ref[...])
        for h in range(N_HEADS):
            kmeta_scr[h, :, 0:LANES] = k_nope[:, h * QK_NOPE:(h + 1) * QK_NOPE].astype(BF16)
            kmeta_scr[h, :, LANES:QK_PAD] = k_rope.astype(BF16)
            vtmeta_scr[h, 0:V_HEAD, :] = v_t[h * V_HEAD:(h + 1) * V_HEAD, :].astype(BF16)
            vtmeta_scr[h, V_HEAD:V_AUG, :] = _denominator_rows(N_META)

    @pl.when(i == 0)
    def _seed_halo():
        pool_scr[0:HALO, :] = halo_scr[...]

    def _process_tile(n_earlier):
        hm = tm // 2
        halves = (slice(0, hm), slice(hm, tm))
        xn_halves = [_rms(x_ref[r, :], norm_g_ref[...]).astype(BF16) for r in halves]

        for r, xn_half in zip(halves, xn_halves):
            pool_scr[HALO + r.start:HALO + r.stop, :] = _dot(xn_half, w_pi_ref[...])
        xn = jnp.concatenate(xn_halves, axis=0)
        cq = _dot(xn, w_cq_ref[...])
        ckvr = _dot(xn, w_ckv_ref[...])

        pooled = []
        for g, w in enumerate(POOL_WINDOWS):
            xg = pool_scr[:, g * POOL_GROUP:(g + 1) * POOL_GROUP]
            win = _window_sum(xg, w)
            pooled.append((win[HALO:, :] * (1.0 / w) - xg[HALO:, :]).astype(BF16))
        mixed = [_dot(pooled[g], pool_w_ref[g]) for g in range(len(POOL_WINDOWS))]
        pool_gate = _dot(xn, w_pg_ref[...])

        cq_n = _rms(cq, q_g_ref[...]).astype(BF16)
        q_t = _dot_nt(w_qbt_ref[...], cq_n) * Q_SCALE
        row0 = pl.multiple_of(i * tm, tm)
        tile_rows = pl.ds(row0, tm)
        table_rows = tile_rows if n_earlier is None else pl.ds(n_earlier * tm, tm)
        k_nope, v_t, k_rope = _latent_kv(ckvr, kv_g_ref, w_kb_ref, w_vbt_ref,
                                         kcos_ref[table_rows, :], ksin_ref[table_rows, :])

        k_rope_bf = k_rope.astype(BF16)
        denominator_rows = _denominator_rows(tm)
        for h in range(N_HEADS):
            k_scr[h, tile_rows, 0:LANES] = k_nope[:, h * QK_NOPE:(h + 1) * QK_NOPE].astype(BF16)
            k_scr[h, tile_rows, LANES:QK_PAD] = k_rope_bf
            vt_scr[h, 0:V_HEAD, tile_rows] = v_t[h * V_HEAD:(h + 1) * V_HEAD, :].astype(BF16)
            vt_scr[h, V_HEAD:V_AUG, tile_rows] = denominator_rows

        cos_q = qcos_ref[:, table_rows]
        sin_q = qsin_ref[:, table_rows]
        half = QK_ROPE // 2
        for h in range(N_HEADS):
            base = h * (QK_NOPE + QK_ROPE)
            x1 = q_t[base + QK_NOPE:base + QK_NOPE + half, :]
            x2 = q_t[base + QK_NOPE + half:base + QK_NOPE + QK_ROPE, :]
            qt_scr[h, 0:QK_NOPE, :] = q_t[base:base + QK_NOPE, :].astype(BF16)
            qt_scr[h, QK_NOPE:QK_NOPE + half, :] = (x1 * cos_q - x2 * sin_q).astype(BF16)
            qt_scr[h, QK_NOPE + half:QK_NOPE + QK_ROPE, :] = (x1 * sin_q + x2 * cos_q).astype(BF16)
            qt_scr[h, QK_NOPE + QK_ROPE:QK_PAD, :] = jnp.zeros((QK_PAD - QK_NOPE - QK_ROPE, tm), BF16)

        pool_out = (_silu(pool_gate) * (jnp.concatenate(mixed, axis=1) * pool_scale_ref[...])).astype(BF16)

        def _scores(h, rows):
            return _dot(k_scr[h, rows, :], qt_scr[h])

        def _exact_update(h, blocks):
            m_prev = m_scr[h]
            m_new = m_prev
            for s, _ in blocks:
                m_new = jnp.maximum(m_new, jnp.max(s, axis=0, keepdims=True))
            acc = jnp.exp2(m_prev - m_new) * acc_scr[h]
            for s, vt_blk in blocks:
                acc = acc + _dot(vt_blk, jnp.exp2(s - m_new).astype(BF16))
            acc_scr[h] = acc
            m_scr[h] = m_new

        def _stale_update(h, blocks, cols=slice(0, tm), first=False):
            m_ref = m_scr[h, :, cols]
            acc = None if first else acc_scr[h, :, cols]
            rise = rise_scr[:, cols]
            lift = jnp.zeros_like(m_ref)
            for s, vt_blk in blocks:
                t = s - m_ref
                top = jnp.max(t, axis=0, keepdims=True)
                weighted = _dot(vt_blk, jnp.exp2(t).astype(BF16))
                acc = weighted if acc is None else acc + weighted
                rise = jnp.maximum(rise, top)
                lift = jnp.maximum(lift, top)
            acc_scr[h, :, cols] = acc * jnp.exp2(-lift)
            m_scr[h, :, cols] = m_ref + lift
            rise_scr[:, cols] = rise

        key_chunk = lax.broadcasted_iota(jnp.int32, (tm, tm), 0) // CHUNK
        query_chunk = lax.broadcasted_iota(jnp.int32, (tm, tm), 1) // CHUNK
        diag_valid = key_chunk <= query_chunk

        def _start_from_meta_keys(zero_accumulators):
            s_meta = [_dot(kmeta_scr[h], qt_scr[h]) for h in range(N_HEADS)]
            for h in range(N_HEADS):
                m_scr[h] = jnp.max(s_meta[h], axis=0, keepdims=True)
                if zero_accumulators:
                    acc_scr[h] = jnp.zeros((V_AUG, tm), F32)
            return s_meta

        def _attend_unrolled(update):
            s_meta = _start_from_meta_keys(zero_accumulators=False)
            hq = tm // 2
            near_rows = pl.ds(n_earlier * tm, hq)
            far_rows = pl.ds(n_earlier * tm + hq, hq)
            late_queries = slice(hq, tm)
            near_valid = diag_valid[0:hq, :]
            far_valid = diag_valid[hq:, hq:]
            units = []
            for pair in range(0, N_HEADS, 2):
                units += [("near", h) for h in (pair, pair + 1)] + [("far", h) for h in (pair, pair + 1)]
            units += [(j, h) for j in range(n_earlier) for h in range(N_HEADS)]

            def unit_scores(unit):
                j, h = unit
                if j == "near":
                    return jnp.where(near_valid, _scores(h, near_rows), MASK_VALUE)
                if j == "far":
                    return jnp.where(far_valid, _dot(k_scr[h, far_rows, :], qt_scr[h, :, late_queries]),
                                     MASK_VALUE)
                return _scores(h, pl.ds(j * tm, tm))

            s_next = unit_scores(units[0])
            for n, (j, h) in enumerate(units):
                s = s_next
                if n + 1 < len(units):
                    s_next = unit_scores(units[n + 1])
                if n == 0:
                    attn_gate = _dot(xn, w_ag_ref[...])
                if j == "near":
                    update(h, [(s_meta[h], vtmeta_scr[h]), (s, vt_scr[h, :, near_rows])], slice(0, tm), True)
                elif j == "far":
                    update(h, [(s, vt_scr[h, :, far_rows])], late_queries)
                else:
                    update(h, [(s, vt_scr[h, :, pl.ds(j * tm, tm)])])
            return attn_gate

        def _attend_looped(update):
            s_meta = _start_from_meta_keys(zero_accumulators=True)
            s_next = jnp.where(diag_valid, _scores(0, tile_rows), MASK_VALUE)
            for h in range(N_HEADS):
                s = s_next
                if h + 1 < N_HEADS:
                    s_next = jnp.where(diag_valid, _scores(h + 1, tile_rows), MASK_VALUE)
                else:
                    s_scr[...] = _scores(0, pl.ds(0, tm))
                update(h, [(s_meta[h], vtmeta_scr[h]), (s, vt_scr[h, :, tile_rows])])

            def _earlier_tile(j, carry):
                rows = pl.ds(pl.multiple_of(j * tm, tm), tm)
                next_rows = pl.ds(pl.multiple_of((j + 1) * tm, tm), tm)
                s_next = s_scr[...]
                for h in range(N_HEADS):
                    s = s_next
                    if h + 1 < N_HEADS:
                        s_next = _scores(h + 1, rows)
                    else:
                        s_scr[...] = _scores(0, next_rows)
                    update(h, [(s, vt_scr[h, :, rows])])
                return carry

            lax.fori_loop(0, i, _earlier_tile, 0)
            return _dot(xn, w_ag_ref[...])

        attn_gate = _attend_looped(_exact_update) if n_earlier is None else _attend_unrolled(_stale_update)
        if n_earlier is not None:
            worst_rise_ref[0] = jnp.max(rise_scr[...])

        mix_pool = _dot(pool_out, w_out_ref[0:D_POOL, :])
        attn = []
        for h in range(N_HEADS):
            acc = acc_scr[h]
            attn.append((acc[0:V_HEAD, :] * (1.0 / acc[V_HEAD:V_HEAD + 1, :])).T)
        attn_out = (_silu(attn_gate) * jnp.concatenate(attn, axis=1)).astype(BF16)
        for r in halves:
            mix = mix_pool[r, :] + _dot(attn_out[r, :], w_out_ref[D_POOL:, :])
            out_ref[r, :] = _rms(x_ref[r, :] + mix, final_g_ref[...])

    rise_scr[...] = jnp.full((1, tm), MASK_VALUE, F32)
    for n_earlier in range(tiles_per_seq):
        @pl.when(i == n_earlier)
        def _tile_at(n_earlier=n_earlier):
            _process_tile(n_earlier)

    @pl.when(worst_rise_ref[0] >= EXP_HEADROOM)
    def _redo_exactly():
        _process_tile(None)

    pool_scr[0:HALO, :] = pool_scr[tm:tm + HALO, :]


def _rope_tables(length):
    half = QK_ROPE // 2
    inv_freq = 1.0 / (ROPE_THETA ** (np.arange(half, dtype=np.float64) / half))
    ang = np.arange(length, dtype=np.float64)[:, None] * inv_freq[None, :]
    return np.cos(ang), np.sin(ang)


def kernel(x, meta_tokens, norm_g, w_in, q_norm_g, w_q_b, kv_norm_g, w_kv_b, pool_w, pool_scale, w_out, final_norm_g):
    batch, seq, d_model = x.shape
    assert d_model == D_MODEL and seq % ROW_TILE == 0 and meta_tokens.shape == (N_META, D_MODEL)
    assert norm_g.shape[0] == 1 and w_in.shape == (1, D_MODEL, D_IN), "single-layer block"
    tm = ROW_TILE

    cos, sin = _rope_tables(N_META + seq)
    kcos = np.tile(cos, (1, LANES // (QK_ROPE // 2))).astype(np.float32)
    ksin = np.tile(np.concatenate([-sin, sin], axis=1), (1, LANES // QK_ROPE)).astype(np.float32)
    qcos = np.ascontiguousarray(cos[N_META:].T).astype(np.float32)
    qsin = np.ascontiguousarray(sin[N_META:].T).astype(np.float32)

    def whole(a):
        return pl.BlockSpec(a.shape, lambda b, i: (0,) * a.ndim)

    def fetched_once(a, squeeze_layer):
        shape = ((None,) + a.shape[1:]) if squeeze_layer else a.shape
        return pl.BlockSpec(shape, lambda b, i: (0,) * a.ndim, pipeline_mode=pl.Buffered(1))

    row_spec = pl.BlockSpec((None, tm, D_MODEL), lambda b, i: (b, i, 0))
    def table_spec(a):
        return pl.BlockSpec(a.shape, lambda b, i: (0, 0), pipeline_mode=pl.Buffered(1))

    w_in_t = jnp.transpose(w_in[0])

    operands = [
        (x, row_spec),
        (meta_tokens, None),
        (kcos[N_META:], table_spec(kcos[N_META:])),
        (ksin[N_META:], table_spec(ksin[N_META:])),
        (kcos[:N_META], None),
        (ksin[:N_META], None),
        (qcos, table_spec(qcos)),
        (qsin, table_spec(qsin)),
        (norm_g, None),
        (w_in_t, fetched_once(w_in_t, False)),
        (q_norm_g, None), (w_q_b, fetched_once(w_q_b, True)),
        (kv_norm_g, None), (w_kv_b, fetched_once(w_kv_b, True)),
        (pool_w, fetched_once(pool_w, True)),
        (pool_scale, None),
        (w_out, fetched_once(w_out, True)),
        (final_norm_g.reshape(1, D_MODEL), None),
    ]
    args = [a for a, _ in operands]
    in_specs = [whole(a) if s is None else s for a, s in operands]

    return pl.pallas_call(
        functools.partial(_fused_kernel, seq // tm),
        out_shape=jax.ShapeDtypeStruct(x.shape, x.dtype),
        grid=(batch, seq // tm),
        in_specs=in_specs,
        out_specs=row_spec,
        scratch_shapes=[
            pltpu.VMEM((D_MODEL, D_IN_PADDED), BF16),
            pltpu.VMEM((N_HEADS * (QK_NOPE + QK_ROPE), Q_LORA), BF16),
            pltpu.VMEM((KV_LORA, N_HEADS * QK_NOPE), BF16),
            pltpu.VMEM((N_HEADS * V_HEAD, KV_LORA), BF16),
            pltpu.VMEM((len(POOL_WINDOWS), POOL_GROUP, POOL_GROUP), BF16),
            pltpu.VMEM((D_MODEL, D_MODEL), BF16),
            pltpu.VMEM((N_HEADS, seq, QK_PAD), BF16),
            pltpu.VMEM((N_HEADS, V_AUG, seq), BF16),
            pltpu.VMEM((N_HEADS, N_META, QK_PAD), BF16),
            pltpu.VMEM((N_HEADS, V_AUG, N_META), BF16),
            pltpu.VMEM((HALO + tm, D_POOL), F32),
            pltpu.VMEM((HALO, D_POOL), F32),
            pltpu.VMEM((N_HEADS, QK_PAD, tm), BF16),
            pltpu.VMEM((tm, tm), F32),
            pltpu.VMEM((N_HEADS, 1, tm), F32),
            pltpu.VMEM((N_HEADS, V_AUG, tm), F32),
            pltpu.VMEM((1, tm), F32),
            pltpu.SMEM((1,), F32),
        ],
        compiler_params=pltpu.CompilerParams(
            dimension_semantics=("arbitrary", "arbitrary"),
            vmem_limit_bytes=VMEM_LIMIT_BYTES,
        ),
        name="hybrid_pool_mla_block",
    )(*args)
```

```python
import functools
import math

import numpy as np
import jax
import jax.numpy as jnp
from jax import lax
from jax.experimental import pallas as pl
from jax.experimental.pallas import tpu as pltpu

D_MODEL = 1024
N_META = 16
CHUNK = 64
D_POOL = 512
POOL_WINDOWS = (2, 4, 8, 16)
POOL_GROUP = 128
N_HEADS = 4
QK_NOPE = 128
QK_ROPE = 64
V_HEAD = 128
D_ATTN = N_HEADS * V_HEAD
Q_LORA = 256
KV_LORA = 128
ROPE_THETA = 10000.0
EPS = 1e-6
Q_SCALE = (QK_NOPE + QK_ROPE) ** -0.5 * math.log2(math.e)

O_CQ = 2 * D_POOL
O_CKV = O_CQ + Q_LORA
O_KR = O_CKV + KV_LORA
O_AG = O_KR + QK_ROPE
D_IN = O_AG + D_ATTN
S_AG = O_AG + QK_ROPE
D_IN_PADDED = S_AG + D_ATTN

LANES = 128
BF16_ROWS = 16
QK_PAD = 2 * LANES
V_AUG = V_HEAD + BF16_ROWS
HALO = 16
ROW_TILE = 512
EXP_HEADROOM = 32.0
MASK_VALUE = -1e30
VMEM_LIMIT_BYTES = 56 * 1024 * 1024

F32 = jnp.float32
BF16 = jnp.bfloat16


def _dot(a, b):
    return jnp.dot(a, b, preferred_element_type=F32)


def _dot_nt(a, b):
    return lax.dot_general(a, b, (((1,), (1,)), ((), ())), preferred_element_type=F32)


def _rms(x, g):
    return x * lax.rsqrt(jnp.mean(x * x, axis=-1, keepdims=True) + EPS) * g


def _silu(x):
    h = 0.5 * x
    return h + h * jnp.tanh(h)


def _rope_lanes(x, cos_t, sin_t):
    lane = lax.broadcasted_iota(jnp.int32, x.shape, 1)
    first_half = (lane % QK_ROPE) < (QK_ROPE // 2)
    partner = jnp.where(first_half,
                        pltpu.roll(x, LANES - QK_ROPE // 2, axis=1),
                        pltpu.roll(x, QK_ROPE // 2, axis=1))
    return x * cos_t + partner * sin_t


def _latent_kv(ckvr, kv_g_ref, w_kb_ref, w_vbt_ref, cos_t, sin_t):
    ckv_n = _rms(ckvr[:, :KV_LORA], kv_g_ref[...]).astype(BF16)
    k_nope = _dot(ckv_n, w_kb_ref[...])
    v_t = _dot_nt(w_vbt_ref[...], ckv_n)
    k_rope = _rope_lanes(ckvr[:, KV_LORA:], cos_t, sin_t)
    return k_nope, v_t, k_rope


def _window_sum(xg, w):
    y = xg
    span = 1
    while span < w:
        y = y + pltpu.roll(y, span, axis=0)
        span *= 2
    return y


def _denominator_rows(n):
    return (lax.broadcasted_iota(jnp.int32, (BF16_ROWS, n), 0) == 0).astype(BF16)


def _fused_kernel(tiles_per_seq, x_ref, meta_ref, kcos_ref, ksin_ref, mcos_ref, msin_ref, qcos_ref, qsin_ref,
                  norm_g_ref, w_in_ref, q_g_ref, w_qb_ref, kv_g_ref, w_kvb_ref,
                  pool_w_f32_ref, pool_scale_ref, w_out_f32_ref, final_g_ref,
                  out_ref,
                  w_in_s, w_qbt_ref, w_kb_ref, w_vbt_ref, pool_w_ref, w_out_ref,
                  k_scr, vt_scr, kmeta_scr, vtmeta_scr, pool_scr, halo_scr,
                  qt_scr, s_scr, m_scr, acc_scr, rise_scr, worst_rise_ref):
    b = pl.program_id(0)
    i = pl.program_id(1)
    tm = ROW_TILE
    first_step = (b == 0) & (i == 0)

    w_pi_ref = w_in_s.at[:, 0:D_POOL]
    w_pg_ref = w_in_s.at[:, D_POOL:O_CQ]
    w_cq_ref = w_in_s.at[:, O_CQ:O_CKV]
    w_ckv_ref = w_in_s.at[:, O_CKV:S_AG]
    w_ag_ref = w_in_s.at[:, S_AG:S_AG + D_ATTN]

    @pl.when(first_step)
    def _prepare_weights():
        for c in range(O_KR // LANES):
            w_in_s[:, c * LANES:(c + 1) * LANES] = w_in_ref[c * LANES:(c + 1) * LANES, :].T.astype(BF16)
        k_rope_rows = w_in_ref[O_KR:O_AG, :]
        w_in_s[:, O_KR:S_AG] = jnp.concatenate([k_rope_rows, k_rope_rows], axis=0).T.astype(BF16)
        for c in range(D_ATTN // LANES):
            w_in_s[:, S_AG + c * LANES:S_AG + (c + 1) * LANES] = (
                w_in_ref[O_AG + c * LANES:O_AG + (c + 1) * LANES, :].T.astype(BF16))
        quarter = D_MODEL // 4
        for r in range(4):
            rs = slice(r * quarter, (r + 1) * quarter)
            w_out_ref[rs, :] = w_out_f32_ref[rs, :].astype(BF16)
        w_qbt_ref[...] = w_qb_ref[...].T.astype(BF16)
        for h in range(N_HEADS):
            base = h * (QK_NOPE + V_HEAD)
            w_kb_ref[:, h * QK_NOPE:(h + 1) * QK_NOPE] = w_kvb_ref[:, base:base + QK_NOPE].astype(BF16)
            w_vbt_ref[h * V_HEAD:(h + 1) * V_HEAD, :] = w_kvb_ref[:, base + QK_NOPE:base + QK_NOPE + V_HEAD].T.astype(BF16)
        pool_w_ref[...] = pool_w_f32_ref[...].astype(BF16)

    @pl.when(first_step)
    def _meta_prologue():
        xn = _rms(meta_ref[...], norm_g_ref[...]).astype(BF16)
        halo_scr[...] = _dot(xn, w_pi_ref[...])
        k_nope, v_t, k_rope = _latent_kv(_dot(xn, w_ckv_ref[...]), kv_g_ref, w_kb_ref, w_vbt_ref,
                                         mcos_ref[...], msin_ref[...])
        for h in range(N_HEADS):
            kmeta_scr[h, :, 0:LANES] = k_nope[:, h * QK_NOPE:(h + 1) * QK_NOPE].astype(BF16)
            kmeta_scr[h, :, LANES:QK_PAD] = k_rope.astype(BF16)
            vtmeta_scr[h, 0:V_HEAD, :] = v_t[h * V_HEAD:(h + 1) * V_HEAD, :].astype(BF16)
            vtmeta_scr[h, V_HEAD:V_AUG, :] = _denominator_rows(N_META)

    @pl.when(i == 0)
    def _seed_halo():
        pool_scr[0:HALO, :] = halo_scr[...]

    def _process_tile(n_earlier):
        hm = tm // 2
        halves = (slice(0, hm), slice(hm, tm))
        xn_halves = [_rms(x_ref[r, :], norm_g_ref[...]).astype(BF16) for r in halves]

        for r, xn_half in zip(halves, xn_halves):
            pool_scr[HALO + r.start:HALO + r.stop, :] = _dot(xn_half, w_pi_ref[...])
        xn = jnp.concatenate(xn_halves, axis=0)
        cq = _dot(xn, w_cq_ref[...])
        ckvr = _dot(xn, w_ckv_ref[...])

        pooled = []
        for g, w in enumerate(POOL_WINDOWS):
            xg = pool_scr[:, g * POOL_GROUP:(g + 1) * POOL_GROUP]
            win = _window_sum(xg, w)
            pooled.append((win[HALO:, :] * (1.0 / w) - xg[HALO:, :]).astype(BF16))
        mixed = [_dot(pooled[g], pool_w_ref[g]) for g in range(len(POOL_WINDOWS))]
        pool_gate = _dot(xn, w_pg_ref[...])

        cq_n = _rms(cq, q_g_ref[...]).astype(BF16)
        q_t = _dot_nt(w_qbt_ref[...], cq_n) * Q_SCALE
        row0 = pl.multiple_of(i * tm, tm)
        tile_rows = pl.ds(row0, tm)
        table_rows = tile_rows if n_earlier is None else pl.ds(n_earlier * tm, tm)
        k_nope, v_t, k_rope = _latent_kv(ckvr, kv_g_ref, w_kb_ref, w_vbt_ref,
                                         kcos_ref[table_rows, :], ksin_ref[table_rows, :])

        k_rope_bf = k_rope.astype(BF16)
        denominator_rows = _denominator_rows(tm)
        for h in range(N_HEADS):
            k_scr[h, tile_rows, 0:LANES] = k_nope[:, h * QK_NOPE:(h + 1) * QK_NOPE].astype(BF16)
            k_scr[h, tile_rows, LANES:QK_PAD] = k_rope_bf
            vt_scr[h, 0:V_HEAD, tile_rows] = v_t[h * V_HEAD:(h + 1) * V_HEAD, :].astype(BF16)
            vt_scr[h, V_HEAD:V_AUG, tile_rows] = denominator_rows

        cos_q = qcos_ref[:, table_rows]
        sin_q = qsin_ref[:, table_rows]
        half = QK_ROPE // 2
        for h in range(N_HEADS):
            base = h * (QK_NOPE + QK_ROPE)
            x1 = q_t[base + QK_NOPE:base + QK_NOPE + half, :]
            x2 = q_t[base + QK_NOPE + half:base + QK_NOPE + QK_ROPE, :]
            qt_scr[h, 0:QK_NOPE, :] = q_t[base:base + QK_NOPE, :].astype(BF16)
            qt_scr[h, QK_NOPE:QK_NOPE + half, :] = (x1 * cos_q - x2 * sin_q).astype(BF16)
            qt_scr[h, QK_NOPE + half:QK_NOPE + QK_ROPE, :] = (x1 * sin_q + x2 * cos_q).astype(BF16)
            qt_scr[h, QK_NOPE + QK_ROPE:QK_PAD, :] = jnp.zeros((QK_PAD - QK_NOPE - QK_ROPE, tm), BF16)

        pool_out = (_silu(pool_gate) * (jnp.concatenate(mixed, axis=1) * pool_scale_ref[...])).astype(BF16)

        def _scores(h, rows):
            return _dot(k_scr[h, rows, :], qt_scr[h])

        def _exact_update(h, blocks):
            m_prev = m_scr[h]
            m_new = m_prev
            for s, _ in blocks:
                m_new = jnp.maximum(m_new, jnp.max(s, axis=0, keepdims=True))
            acc = jnp.exp2(m_prev - m_new) * acc_scr[h]
            for s, vt_blk in blocks:
                acc = acc + _dot(vt_blk, jnp.exp2(s - m_new).astype(BF16))
            acc_scr[h] = acc
            m_scr[h] = m_new

        def _stale_update(h, blocks, cols=slice(0, tm)):
            m_ref = m_scr[h, :, cols]
            acc = acc_scr[h, :, cols]
            rise = rise_scr[:, cols]
            lift = jnp.zeros_like(m_ref)
            for s, vt_blk in blocks:
                t = s - m_ref
                top = jnp.max(t, axis=0, keepdims=True)
                acc = acc + _dot(vt_blk, jnp.exp2(t).astype(BF16))
                rise = jnp.maximum(rise, top)
                lift = jnp.maximum(lift, top)
            acc_scr[h, :, cols] = acc * jnp.exp2(-lift)
            m_scr[h, :, cols] = m_ref + lift
            rise_scr[:, cols] = rise

        key_chunk = lax.broadcasted_iota(jnp.int32, (tm, tm), 0) // CHUNK
        query_chunk = lax.broadcasted_iota(jnp.int32, (tm, tm), 1) // CHUNK
        diag_valid = key_chunk <= query_chunk

        def _start_from_meta_keys():
            s_meta = [_dot(kmeta_scr[h], qt_scr[h]) for h in range(N_HEADS)]
            for h in range(N_HEADS):
                m_scr[h] = jnp.max(s_meta[h], axis=0, keepdims=True)
                acc_scr[h] = jnp.zeros((V_AUG, tm), F32)
            return s_meta

        def _attend_unrolled(update):
            s_meta = _start_from_meta_keys()
            hq = tm // 2
            near_rows = pl.ds(n_earlier * tm, hq)
            far_rows = pl.ds(n_earlier * tm + hq, hq)
            late_queries = slice(hq, tm)
            near_valid = diag_valid[0:hq, :]
            far_valid = diag_valid[hq:, hq:]
            units = []
            for pair in range(0, N_HEADS, 2):
                units += [("near", h) for h in (pair, pair + 1)] + [("far", h) for h in (pair, pair + 1)]
            units += [(j, h) for j in range(n_earlier) for h in range(N_HEADS)]

            def unit_scores(unit):
                j, h = unit
                if j == "near":
                    return jnp.where(near_valid, _scores(h, near_rows), MASK_VALUE)
                if j == "far":
                    return jnp.where(far_valid, _dot(k_scr[h, far_rows, :], qt_scr[h, :, late_queries]),
                                     MASK_VALUE)
                return _scores(h, pl.ds(j * tm, tm))

            s_next = unit_scores(units[0])
            for n, (j, h) in enumerate(units):
                s = s_next
                if n + 1 < len(units):
                    s_next = unit_scores(units[n + 1])
                if n == 0:
                    attn_gate = _dot(xn, w_ag_ref[...])
                if j == "near":
                    update(h, [(s_meta[h], vtmeta_scr[h]), (s, vt_scr[h, :, near_rows])])
                elif j == "far":
                    update(h, [(s, vt_scr[h, :, far_rows])], late_queries)
                else:
                    update(h, [(s, vt_scr[h, :, pl.ds(j * tm, tm)])])
            return attn_gate

        def _attend_looped(update):
            s_meta = _start_from_meta_keys()
            s_next = jnp.where(diag_valid, _scores(0, tile_rows), MASK_VALUE)
            for h in range(N_HEADS):
                s = s_next
                if h + 1 < N_HEADS:
                    s_next = jnp.where(diag_valid, _scores(h + 1, tile_rows), MASK_VALUE)
                else:
                    s_scr[...] = _scores(0, pl.ds(0, tm))
                update(h, [(s_meta[h], vtmeta_scr[h]), (s, vt_scr[h, :, tile_rows])])

            def _earlier_tile(j, carry):
                rows = pl.ds(pl.multiple_of(j * tm, tm), tm)
                next_rows = pl.ds(pl.multiple_of((j + 1) * tm, tm), tm)
                s_next = s_scr[...]
                for h in range(N_HEADS):
                    s = s_next
                    if h + 1 < N_HEADS:
                        s_next = _scores(h + 1, rows)
                    else:
                        s_scr[...] = _scores(0, next_rows)
                    update(h, [(s, vt_scr[h, :, rows])])
                return carry

            lax.fori_loop(0, i, _earlier_tile, 0)
            return _dot(xn, w_ag_ref[...])

        attn_gate = _attend_looped(_exact_update) if n_earlier is None else _attend_unrolled(_stale_update)
        if n_earlier is not None:
            worst_rise_ref[0] = jnp.max(rise_scr[...])

        mix_pool = _dot(pool_out, w_out_ref[0:D_POOL, :])
        for r in halves:
            attn = []
            for h in range(N_HEADS):
                attn.append((acc_scr[h, 0:V_HEAD, r] * (1.0 / acc_scr[h, V_HEAD:V_HEAD + 1, r])).T)
            attn_out = (_silu(attn_gate[r, :]) * jnp.concatenate(attn, axis=1)).astype(BF16)
            mix = mix_pool[r, :] + _dot(attn_out, w_out_ref[D_POOL:, :])
            out_ref[r, :] = _rms(x_ref[r, :] + mix, final_g_ref[...])

    rise_scr[...] = jnp.full((1, tm), MASK_VALUE, F32)
    for n_earlier in range(tiles_per_seq):
        @pl.when(i == n_earlier)
        def _tile_at(n_earlier=n_earlier):
            _process_tile(n_earlier)

    @pl.when(worst_rise_ref[0] >= EXP_HEADROOM)
    def _redo_exactly():
        _process_tile(None)

    pool_scr[0:HALO, :] = pool_scr[tm:tm + HALO, :]


def _rope_tables(length):
    half = QK_ROPE // 2
    inv_freq = 1.0 / (ROPE_THETA ** (np.arange(half, dtype=np.float64) / half))
    ang = np.arange(length, dtype=np.float64)[:, None] * inv_freq[None, :]
    return np.cos(ang), np.sin(ang)


def kernel(x, meta_tokens, norm_g, w_in, q_norm_g, w_q_b, kv_norm_g, w_kv_b, pool_w, pool_scale, w_out, final_norm_g):
    batch, seq, d_model = x.shape
    assert d_model == D_MODEL and seq % ROW_TILE == 0 and meta_tokens.shape == (N_META, D_MODEL)
    assert norm_g.shape[0] == 1 and w_in.shape == (1, D_MODEL, D_IN), "single-layer block"
    tm = ROW_TILE

    cos, sin = _rope_tables(N_META + seq)
    kcos = np.tile(cos, (1, LANES // (QK_ROPE // 2))).astype(np.float32)
    ksin = np.tile(np.concatenate([-sin, sin], axis=1), (1, LANES // QK_ROPE)).astype(np.float32)
    qcos = np.ascontiguousarray(cos[N_META:].T).astype(np.float32)
    qsin = np.ascontiguousarray(sin[N_META:].T).astype(np.float32)

    def whole(a):
        return pl.BlockSpec(a.shape, lambda b, i: (0,) * a.ndim)

    def fetched_once(a, squeeze_layer):
        shape = ((None,) + a.shape[1:]) if squeeze_layer else a.shape
        return pl.BlockSpec(shape, lambda b, i: (0,) * a.ndim, pipeline_mode=pl.Buffered(1))

    row_spec = pl.BlockSpec((None, tm, D_MODEL), lambda b, i: (b, i, 0))
    def table_spec(a):
        return pl.BlockSpec(a.shape, lambda b, i: (0, 0), pipeline_mode=pl.Buffered(1))

    w_in_t = jnp.transpose(w_in[0])

    operands = [
        (x, row_spec),
        (meta_tokens, None),
        (kcos[N_META:], table_spec(kcos[N_META:])),
        (ksin[N_META:], table_spec(ksin[N_META:])),
        (kcos[:N_META], None),
        (ksin[:N_META], None),
        (qcos, table_spec(qcos)),
        (qsin, table_spec(qsin)),
        (norm_g, None),
        (w_in_t, fetched_once(w_in_t, False)),
        (q_norm_g, None), (w_q_b, fetched_once(w_q_b, True)),
        (kv_norm_g, None), (w_kv_b, fetched_once(w_kv_b, True)),
        (pool_w, fetched_once(pool_w, True)),
        (pool_scale, None),
        (w_out, fetched_once(w_out, True)),
        (final_norm_g.reshape(1, D_MODEL), None),
    ]
    args = [a for a, _ in operands]
    in_specs = [whole(a) if s is None else s for a, s in operands]

    return pl.pallas_call(
        functools.partial(_fused_kernel, seq // tm),
        out_shape=jax.ShapeDtypeStruct(x.shape, x.dtype),
        grid=(batch, seq // tm),
        in_specs=in_specs,
        out_specs=row_spec,
        scratch_shapes=[
            pltpu.VMEM((D_MODEL, D_IN_PADDED), BF16),
            pltpu.VMEM((N_HEADS * (QK_NOPE + QK_ROPE), Q_LORA), BF16),
            pltpu.VMEM((KV_LORA, N_HEADS * QK_NOPE), BF16),
            pltpu.VMEM((N_HEADS * V_HEAD, KV_LORA), BF16),
            pltpu.VMEM((len(POOL_WINDOWS), POOL_GROUP, POOL_GROUP), BF16),
            pltpu.VMEM((D_MODEL, D_MODEL), BF16),
            pltpu.VMEM((N_HEADS, seq, QK_PAD), BF16),
            pltpu.VMEM((N_HEADS, V_AUG, seq), BF16),
            pltpu.VMEM((N_HEADS, N_META, QK_PAD), BF16),
            pltpu.VMEM((N_HEADS, V_AUG, N_META), BF16),
            pltpu.VMEM((HALO + tm, D_POOL), F32),
            pltpu.VMEM((HALO, D_POOL), F32),
            pltpu.VMEM((N_HEADS, QK_PAD, tm), BF16),
            pltpu.VMEM((tm, tm), F32),
            pltpu.VMEM((N_HEADS, 1, tm), F32),
            pltpu.VMEM((N_HEADS, V_AUG, tm), F32),
            pltpu.VMEM((1, tm), F32),
            pltpu.SMEM((1,), F32),
        ],
        compiler_params=pltpu.CompilerParams(
            dimension_semantics=("arbitrary", "arbitrary"),
            vmem_limit_bytes=VMEM_LIMIT_BYTES,
        ),
        name="hybrid_pool_mla_block",
    )(*args)
```
